```python
import jax, jax.numpy as jnp
from jax import lax
import numpy as np

D_MODEL = 2048
BATCH = 4
SEQ = 8192
DEPTH = 1

HEAD_DIM = 128
NSA_HEADS = D_MODEL // (2 * HEAD_DIM)
NSA_KV_HEADS = 2
NSA_GROUP = NSA_HEADS // NSA_KV_HEADS
NSA_WIDTH = NSA_HEADS * HEAD_DIM
KV_WIDTH = NSA_KV_HEADS * HEAD_DIM
CMP_BLOCK = 32
CMP_STRIDE = 16
CMP_HIDDEN = 256
SEL_BLOCK = 64
SEL_TOPK = 16
SEL_LOCAL = 2
WINDOW = 512
Q_BLOCK = 128
GLA_HEADS = 4
GLA_DV = D_MODEL // (2 * GLA_HEADS)
GLA_DK = GLA_DV // 2
GLA_WIDTH = GLA_HEADS * GLA_DV
GLA_GATE_RANK = 16
GLA_GATE_TAU = 16.0
GLA_CHUNK = 64
MIX_WIDTH = NSA_WIDTH + GLA_WIDTH
N_EXPERTS = 32
TOP_K = 4
D_FF = D_MODEL
SWIGLU_LIMIT = 7.0
SWIGLU_ALPHA = 1.702
MOE_BLOCK = 256
LN_EPS = 1e-5
RMS_EPS = 1e-6
NEG_INF = -1e30
FORCE_SCORE = 1e6
DN_ALPHA = (2 * DEPTH) ** 0.25
DN_BETA = (8 * DEPTH) ** -0.25
PROJ_WIDTHS = (NSA_WIDTH, KV_WIDTH, KV_WIDTH, KV_WIDTH, KV_WIDTH, KV_WIDTH, KV_WIDTH, 3 * NSA_HEADS,
               GLA_HEADS * GLA_DK, GLA_HEADS * GLA_DK, GLA_WIDTH, GLA_GATE_RANK, GLA_WIDTH)
IN_WIDTH = sum(PROJ_WIDTHS)
VALUE_SEGMENTS = (2, 4, 6, 10)

kernel_name = "hybrid_nsa_gla_moe_block"


def split_cols(p):
    outs = []
    off = 0
    for w in PROJ_WIDTHS:
        outs.append(p[..., off:off + w])
        off += w
    return outs


def layer_norm(x, g, b):
    xf = x.astype(jnp.float32)
    mu = jnp.mean(xf, axis=-1, keepdims=True)
    var = jnp.mean(jnp.square(xf - mu), axis=-1, keepdims=True)
    return ((xf - mu) * lax.rsqrt(var + LN_EPS) * g + b).astype(x.dtype)


def alibi_slopes(n):
    return 2.0 ** (-8.0 * jnp.arange(1, n + 1, dtype=jnp.float32) / n)


def compress_tokens(kv, pe, w1, b1, w2):
    bsz, seq = kv.shape[:2]
    n_cmp = (seq - CMP_BLOCK) // CMP_STRIDE + 1
    idx = jnp.arange(n_cmp)[:, None] * CMP_STRIDE + jnp.arange(CMP_BLOCK)[None, :]
    blocks = kv[:, idx] + pe[None, None, :, None, :]
    flat = blocks.transpose(0, 1, 3, 2, 4).reshape(bsz, n_cmp, NSA_KV_HEADS, CMP_BLOCK * HEAD_DIM)
    return jax.nn.gelu(flat @ w1 + b1) @ w2


def nsa_attention(q, k_cmp, v_cmp, k_sel, v_sel, k_win, v_win, gates,
                  pe_k, w1_k, b1_k, w2_k, pe_v, w1_v, b1_v, w2_v):
    bsz, seq = q.shape[:2]
    G, R, Dh = NSA_KV_HEADS, NSA_GROUP, HEAD_DIM
    scale = Dh ** -0.5
    slopes = alibi_slopes(NSA_HEADS).reshape(G, R)
    kc = compress_tokens(k_cmp, pe_k, w1_k, b1_k, w2_k)
    vc = compress_tokens(v_cmp, pe_v, w1_v, b1_v, w2_v)
    n_cmp = kc.shape[1]
    cmp_start = jnp.arange(n_cmp) * CMP_STRIDE
    cmp_end = cmp_start + CMP_BLOCK - 1
    cmp_mid = cmp_start.astype(jnp.float32) + (CMP_BLOCK - 1) / 2.0
    n_sel = seq // SEL_BLOCK
    n_topk = min(SEL_TOPK, n_sel)
    sel_start = jnp.arange(n_sel) * SEL_BLOCK
    overlap = ((cmp_start[:, None] < sel_start[None, :] + SEL_BLOCK) &
               (cmp_start[:, None] + CMP_BLOCK > sel_start[None, :])).astype(jnp.float32)
    kb = k_sel.reshape(bsz, n_sel, SEL_BLOCK, G, Dh).transpose(0, 3, 1, 2, 4)
    vb = v_sel.reshape(bsz, n_sel, SEL_BLOCK, G, Dh).transpose(0, 3, 1, 2, 4)
    kw = jnp.pad(k_win, ((0, 0), (WINDOW, 0), (0, 0), (0, 0)))
    vw = jnp.pad(v_win, ((0, 0), (WINDOW, 0), (0, 0), (0, 0)))
    qg = q.reshape(bsz, seq, G, R, Dh)
    gg = gates.reshape(bsz, seq, G, R, 3)
    gather_blocks = jax.vmap(jax.vmap(lambda tbl, ix: tbl[ix]))

    def query_block(i):
        q0 = i * Q_BLOCK
        qb = lax.dynamic_slice_in_dim(qg, q0, Q_BLOCK, axis=1) * scale
        gb = lax.dynamic_slice_in_dim(gg, q0, Q_BLOCK, axis=1)
        pos = q0 + jnp.arange(Q_BLOCK)
        posf = pos.astype(jnp.float32)
        s = jnp.einsum('bqgrd,bcgd->bgrqc', qb, kc).astype(jnp.float32)
        s = s - slopes[None, :, :, None, None] * jnp.abs(posf[:, None] - cmp_mid[None, :])
        ok = cmp_end[None, :] <= pos[:, None]
        p_cmp = jax.nn.softmax(jnp.where(ok, s, NEG_INF), axis=-1) * jnp.any(ok, axis=-1)[:, None].astype(jnp.float32)
        o_cmp = jnp.einsum('bgrqc,bcgd->bqgrd', p_cmp.astype(vc.dtype), vc)
        imp = jnp.einsum('bgrqc,cn->bgqn', p_cmp, overlap)
        blk = jnp.arange(n_sel)[None, :]
        cur = (pos // SEL_BLOCK)[:, None]
        forced = (blk == 0) | ((blk <= cur) & (blk > cur - SEL_LOCAL))
        imp = jnp.where(forced, FORCE_SCORE, jnp.where(blk > cur, -1.0, imp))
        _, idx = lax.top_k(imp, n_topk)
        ks = gather_blocks(kb, idx)
        vs = gather_blocks(vb, idx)
        s = jnp.einsum('bqgrd,bgqkld->bgrqkl', qb, ks).astype(jnp.float32)
        kpos = idx[..., None] * SEL_BLOCK + jnp.arange(SEL_BLOCK)
        rel = pos[None, None, :, None, None] - kpos
        s = s - slopes[None, :, :, None, None, None] * jnp.abs(rel).astype(jnp.float32)[:, :, None]
        s = jnp.where((rel >= 0)[:, :, None], s, NEG_INF)
        p_sel = jax.nn.softmax(s.reshape(bsz, G, R, Q_BLOCK, -1), axis=-1).reshape(s.shape)
        o_sel = jnp.einsum('bgrqkl,bgqkld->bqgrd', p_sel.astype(vs.dtype), vs)
        kwb = lax.dynamic_slice_in_dim(kw, q0, Q_BLOCK + WINDOW, axis=1)
        vwb = lax.dynamic_slice_in_dim(vw, q0, Q_BLOCK + WINDOW, axis=1)
        kwpos = q0 - WINDOW + jnp.arange(Q_BLOCK + WINDOW)
        relw = pos[:, None] - kwpos[None, :]
        okw = (relw >= 0) & (relw < WINDOW) & (kwpos[None, :] >= 0)
        s = jnp.einsum('bqgrd,bkgd->bgrqk', qb, kwb).astype(jnp.float32)
        s = s - slopes[None, :, :, None, None] * jnp.abs(relw).astype(jnp.float32)
        p_win = jax.nn.softmax(jnp.where(okw, s, NEG_INF), axis=-1)
        o_win = jnp.einsum('bgrqk,bkgd->bqgrd', p_win.astype(vwb.dtype), vwb)
        o = gb[..., 0:1] * o_cmp + gb[..., 1:2] * o_sel + gb[..., 2:3] * o_win
        return o.reshape(bsz, Q_BLOCK, NSA_WIDTH)

    out = lax.map(query_block, jnp.arange(seq // Q_BLOCK))
    return out.transpose(1, 0, 2, 3).reshape(bsz, seq, NSA_WIDTH)


def gla_attention(q, k, v, log_a, r, norm_w):
    bsz, seq = q.shape[:2]
    H, C = GLA_HEADS, GLA_CHUNK
    nc = seq // C

    def chunks(t, d):
        return t.astype(jnp.float32).reshape(bsz, nc, C, H, d).transpose(1, 0, 3, 2, 4)

    qc = chunks(q, GLA_DK) * GLA_DK ** -0.5
    kc = chunks(k, GLA_DK)
    vc = chunks(v, GLA_DV)
    ac = chunks(log_a, GLA_DK)
    causal = jnp.tril(jnp.ones((C, C), dtype=bool))[:, :, None]

    def step(state, inp):
        qi, ki, vi, ai = inp
        b = jnp.cumsum(ai, axis=2)
        diff = b[:, :, :, None, :] - b[:, :, None, :, :]
        decay = jnp.exp(jnp.where(causal, diff, -jnp.inf))
        scores = jnp.einsum('bhtd,bhsd,bhtsd->bhts', qi, ki, decay)
        o = scores @ vi + (qi * jnp.exp(b)) @ state
        b_last = b[:, :, -1:, :]
        state = (jnp.exp(b_last[:, :, 0, :, None]) * state +
                 jnp.einsum('bhsd,bhsv->bhdv', ki * jnp.exp(b_last - b), vi))
        return state, o

    s0 = jnp.zeros((bsz, H, GLA_DK, GLA_DV), jnp.float32)
    _, o = lax.scan(step, s0, (qc, kc, vc, ac))
    o = o.transpose(1, 0, 3, 2, 4).reshape(bsz, seq, H, GLA_DV)
    o = o * lax.rsqrt(jnp.mean(jnp.square(o), axis=-1, keepdims=True) + RMS_EPS) * norm_w
    return (o.reshape(bsz, seq, GLA_WIDTH) * jax.nn.silu(r.astype(jnp.float32))).astype(r.dtype)


def moe_ffn(h, w_router, b_router, w_gate_up, b_gate_up, w_down, b_down):
    bsz, seq, d = h.shape
    n_tok = bsz * seq
    n_asg = n_tok * TOP_K
    hf = h.reshape(n_tok, d)
    logits = (hf @ w_router + b_router).astype(jnp.float32)
    top_logits, top_idx = lax.top_k(logits, TOP_K)
    top_w = jax.nn.softmax(top_logits, axis=-1)
    flat_e = top_idx.reshape(n_asg)
    order = jnp.argsort(flat_e)
    sorted_e = flat_e[order]
    counts = jnp.bincount(flat_e, length=N_EXPERTS)
    padded = (counts + MOE_BLOCK - 1) // MOE_BLOCK * MOE_BLOCK
    start = jnp.cumsum(counts) - counts
    ends_p = jnp.cumsum(padded)
    start_p = ends_p - padded
    dest = start_p[sorted_e] + jnp.arange(n_asg, dtype=jnp.int32) - start[sorted_e]
    slot_of = jnp.zeros(n_asg, jnp.int32).at[order].set(dest.astype(jnp.int32))
    n_blocks = -(-n_asg // MOE_BLOCK) + N_EXPERTS
    n_slots = n_blocks * MOE_BLOCK
    slot_token = jnp.zeros(n_slots, jnp.int32).at[slot_of].set(jnp.arange(n_asg, dtype=jnp.int32) // TOP_K)
    slot_w = jnp.zeros(n_slots, jnp.float32).at[slot_of].set(top_w.reshape(n_asg))
    block_expert = jnp.minimum(jnp.searchsorted(ends_p, jnp.arange(n_blocks) * MOE_BLOCK, side='right'),
                               N_EXPERTS - 1)
    xbuf = hf[slot_token].reshape(n_blocks, MOE_BLOCK, d)

    def expert_block(args):
        xb, e = args
        gu = xb @ w_gate_up[e] + b_gate_up[e]
        gate = jnp.minimum(gu[:, :D_FF], SWIGLU_LIMIT)
        up = jnp.clip(gu[:, D_FF:], -SWIGLU_LIMIT, SWIGLU_LIMIT)
        act = (up + 1.0) * gate * jax.nn.sigmoid(SWIGLU_ALPHA * gate)
        return act @ w_down[e] + b_down[e]

    ybuf = lax.map(expert_block, (xbuf, block_expert)).reshape(n_slots, d)
    y = jax.ops.segment_sum(ybuf * slot_w[:, None].astype(ybuf.dtype), slot_token, num_segments=n_tok)
    return y.reshape(bsz, seq, d)


def setup_inputs(seed: int = 0) -> dict:
    key = jax.random.key(seed)
    ks = jax.random.split(key, 27)
    L = DEPTH

    def nrm(k, shape, s):
        return s * jax.random.normal(k, shape, jnp.float32)

    col_scale = jnp.concatenate([jnp.full((w,), DN_BETA if i in VALUE_SEGMENTS else 1.0, jnp.float32)
                                 for i, w in enumerate(PROJ_WIDTHS)])
    return {
        "x": nrm(ks[0], (BATCH, SEQ, D_MODEL), 1.0),
        "c": nrm(ks[1], (BATCH, D_MODEL), 1.0),
        "w_ada": nrm(ks[2], (L, D_MODEL, 6 * D_MODEL), 0.5 * D_MODEL ** -0.5),
        "b_ada": nrm(ks[3], (L, 6 * D_MODEL), 0.02),
        "w_in": nrm(ks[4], (L, D_MODEL, IN_WIDTH), D_MODEL ** -0.5) * col_scale,
        "cmp_pe_k": nrm(ks[5], (L, CMP_BLOCK, HEAD_DIM), 0.02),
        "cmp_w1_k": nrm(ks[6], (L, CMP_BLOCK * HEAD_DIM, CMP_HIDDEN), (CMP_BLOCK * HEAD_DIM) ** -0.5),
        "cmp_b1_k": nrm(ks[7], (L, CMP_HIDDEN), 0.02),
        "cmp_w2_k": nrm(ks[8], (L, CMP_HIDDEN, HEAD_DIM), CMP_HIDDEN ** -0.5),
        "cmp_pe_v": nrm(ks[9], (L, CMP_BLOCK, HEAD_DIM), 0.02),
        "cmp_w1_v": nrm(ks[10], (L, CMP_BLOCK * HEAD_DIM, CMP_HIDDEN), (CMP_BLOCK * HEAD_DIM) ** -0.5),
        "cmp_b1_v": nrm(ks[11], (L, CMP_HIDDEN), 0.02),
        "cmp_w2_v": nrm(ks[12], (L, CMP_HIDDEN, HEAD_DIM), CMP_HIDDEN ** -0.5),
        "gla_w_gate": nrm(ks[13], (L, GLA_GATE_RANK, GLA_HEADS * GLA_DK), GLA_GATE_RANK ** -0.5),
        "gla_b_gate": nrm(ks[14], (L, GLA_HEADS * GLA_DK), 0.1),
        "gla_norm_w": 1.0 + nrm(ks[15], (L, GLA_DV), 0.02),
        "w_out": nrm(ks[16], (L, MIX_WIDTH, D_MODEL), DN_BETA * MIX_WIDTH ** -0.5),
        "ln1_g": 1.0 + nrm(ks[17], (L, D_MODEL), 0.02),
        "ln1_b": nrm(ks[18], (L, D_MODEL), 0.02),
        "w_router": nrm(ks[19], (L, D_MODEL, N_EXPERTS), D_MODEL ** -0.5),
        "b_router": nrm(ks[20], (L, N_EXPERTS), 0.01),
        "w_gate_up": nrm(ks[21], (L, N_EXPERTS, D_MODEL, 2 * D_FF), D_MODEL ** -0.5),
        "b_gate_up": nrm(ks[22], (L, N_EXPERTS, 2 * D_FF), 0.02),
        "w_down": nrm(ks[23], (L, N_EXPERTS, D_FF, D_MODEL), DN_BETA * D_FF ** -0.5),
        "b_down": nrm(ks[24], (L, N_EXPERTS, D_MODEL), 0.02),
        "ln2_g": 1.0 + nrm(ks[25], (L, D_MODEL), 0.02),
        "ln2_b": nrm(ks[26], (L, D_MODEL), 0.02),
    }


def reference(x, c, w_ada, b_ada, w_in, cmp_pe_k, cmp_w1_k, cmp_b1_k, cmp_w2_k,
              cmp_pe_v, cmp_w1_v, cmp_b1_v, cmp_w2_v, gla_w_gate, gla_b_gate, gla_norm_w,
              w_out, ln1_g, ln1_b, w_router, b_router, w_gate_up, b_gate_up, w_down, b_down,
              ln2_g, ln2_b):
    bsz, seq, _ = x.shape
    for l in range(DEPTH):
        mod = jax.nn.silu(c) @ w_ada[l] + b_ada[l]
        sh1, sc1, g1, sh2, sc2, g2 = [m[:, None, :] for m in jnp.split(mod, 6, axis=-1)]
        h = x * (1.0 + sc1) + sh1
        (q_nsa, k_cmp, v_cmp, k_sel, v_sel, k_win, v_win, g_nsa,
         q_gla, k_gla, v_gla, a_gla, r_gla) = split_cols(h @ w_in[l])
        kv_heads = lambda t: t.reshape(bsz, seq, NSA_KV_HEADS, HEAD_DIM)
        y_nsa = nsa_attention(q_nsa.reshape(bsz, seq, NSA_HEADS, HEAD_DIM),
                              kv_heads(k_cmp), kv_heads(v_cmp), kv_heads(k_sel), kv_heads(v_sel),
                              kv_heads(k_win), kv_heads(v_win),
                              jax.nn.sigmoid(g_nsa.reshape(bsz, seq, NSA_HEADS, 3)),
                              cmp_pe_k[l], cmp_w1_k[l], cmp_b1_k[l], cmp_w2_k[l],
                              cmp_pe_v[l], cmp_w1_v[l], cmp_b1_v[l], cmp_w2_v[l])
        log_a = jax.nn.log_sigmoid((a_gla @ gla_w_gate[l] + gla_b_gate[l]).astype(jnp.float32)) / GLA_GATE_TAU
        y_gla = gla_attention(q_gla, k_gla, v_gla, log_a, r_gla, gla_norm_w[l])
        mix = jnp.concatenate([y_nsa, y_gla], axis=-1) @ w_out[l]
        x = layer_norm(DN_ALPHA * x + g1 * mix, ln1_g[l], ln1_b[l])
        h = x * (1.0 + sc2) + sh2
        ffn = moe_ffn(h, w_router[l], b_router[l], w_gate_up[l], b_gate_up[l], w_down[l], b_down[l])
        x = layer_norm(DN_ALPHA * x + g2 * ffn, ln2_g[l], ln2_b[l])
    return x
```

```python
import functools

import jax
import jax.numpy as jnp
from jax import lax
from jax.experimental import pallas as pl
from jax.experimental.pallas import tpu as pltpu

F32 = jnp.float32
BF16 = jnp.bfloat16
U32 = jnp.uint32
I32 = jnp.int32
HIGHEST = lax.Precision.HIGHEST

D_MODEL = 2048
HEAD_DIM = 128
NSA_HEADS = 8
NSA_KV_HEADS = 2
NSA_GROUP = 4
NSA_WIDTH = 1024
KV_WIDTH = 256
CMP_BLOCK = 32
CMP_STRIDE = 16
CMP_HIDDEN = 256
SEL_BLOCK = 64
SEL_TOPK = 16
SEL_LOCAL = 2
WINDOW = 512
GLA_HEADS = 4
GLA_DV = 256
GLA_DK = 128
GLA_WIDTH = 1024
GLA_GATE_RANK = 16
GLA_GATE_TAU = 16.0
GLA_CHUNK = 64
GLA_SUB = 16
N_EXPERTS = 32
TOP_K = 4
D_FF = 2048
SWIGLU_LIMIT = 7.0
SWIGLU_ALPHA = 1.702
LN_EPS = 1e-5
RMS_EPS = 1e-6
NEG_INF = -1e30
FORCE_SCORE = 1e6
DN_ALPHA = 2.0 ** 0.25
PROJ_WIDTHS = (1024, 256, 256, 256, 256, 256, 256, 24, 512, 512, 1024, 16, 1024)

LANES = 128
VMEM_LIMIT = 56 * 1024 * 1024

P_QNSA = 0
P_VGLA = 1024
P_RGLA = 2048
P_KVCMP = 3072
P_KSEL = 3584
P_VSEL = 3840
P_KWIN = 4096
P_VWIN = 4352
P_QGLA = 4608
P_KGLA = 5120
P_WIDTH = 5632
S_WIDTH = 256


def _cparams(sem, vmem=VMEM_LIMIT):
    return pltpu.CompilerParams(dimension_semantics=sem, vmem_limit_bytes=vmem)


def _sigmoid(x):
    return 1.0 / (1.0 + jnp.exp(-x))


def _ada_kernel(c_ref, w_ref, b_ref, o_ref):
    c = c_ref[...]
    sc = c * _sigmoid(c)
    o_ref[...] = jnp.dot(sc, w_ref[...], preferred_element_type=F32, precision=HIGHEST) + b_ref[...]


def ada_mod(c, w_ada, b_ada, tn=1024):
    bsz, d = c.shape
    n = w_ada.shape[1]
    return pl.pallas_call(
        _ada_kernel,
        grid=(n // tn,),
        in_specs=[pl.BlockSpec((bsz, d), lambda j: (0, 0)),
                  pl.BlockSpec((d, tn), lambda j: (0, j)),
                  pl.BlockSpec((1, tn), lambda j: (0, j))],
        out_specs=pl.BlockSpec((bsz, tn), lambda j: (0, j)),
        out_shape=jax.ShapeDtypeStruct((bsz, n), F32),
        compiler_params=_cparams(("arbitrary",)),
        name="ada_mod",
    )(c, w_ada, b_ada.reshape(1, n))


def _inproj_kernel(x_ref, mod_ref, w_ref, o_ref, hb_ref):
    @pl.when(pl.program_id(1) == 0)
    def _():
        sc = mod_ref[0, 0:1, :]
        sh = mod_ref[0, 1:2, :]
        hb_ref[...] = (x_ref[...] * (1.0 + sc) + sh).astype(BF16)

    o_ref[...] = jnp.dot(hb_ref[...], w_ref[...], preferred_element_type=F32).astype(o_ref.dtype)


def in_proj(x2, mod1, w_bf16, seq, tm=1024, tn=512):
    n, d = x2.shape
    wn = w_bf16.shape[1]
    tm = min(tm, seq)
    per_b = seq // tm
    return pl.pallas_call(
        _inproj_kernel,
        grid=(n // tm, wn // tn),
        in_specs=[pl.BlockSpec((tm, d), lambda i, j: (i, 0)),
                  pl.BlockSpec((1, 2, d), lambda i, j: (i // per_b, 0, 0)),
                  pl.BlockSpec((d, tn), lambda i, j: (0, j))],
        out_specs=pl.BlockSpec((tm, tn), lambda i, j: (i, j)),
        out_shape=jax.ShapeDtypeStruct((n, wn), BF16),
        scratch_shapes=[pltpu.VMEM((tm, d), BF16)],
        compiler_params=_cparams(("arbitrary", "arbitrary")),
        name="in_proj",
    )(x2, mod1, w_bf16)


def _inproj_small_kernel(x_ref, mod_ref, w_ref, o_ref):
    sc = mod_ref[0, 0:1, :]
    sh = mod_ref[0, 1:2, :]
    h = x_ref[...] * (1.0 + sc) + sh
    o_ref[...] = jnp.dot(h, w_ref[...], preferred_element_type=F32, precision=HIGHEST)


def in_proj_small(x2, mod1, w_f32, seq, tm=512):
    n, d = x2.shape
    wn = w_f32.shape[1]
    tm = min(tm, seq)
    per_b = seq // tm
    return pl.pallas_call(
        _inproj_small_kernel,
        grid=(n // tm,),
        in_specs=[pl.BlockSpec((tm, d), lambda i: (i, 0)),
                  pl.BlockSpec((1, 2, d), lambda i: (i // per_b, 0, 0)),
                  pl.BlockSpec((d, wn), lambda i: (0, 0))],
        out_specs=pl.BlockSpec((tm, wn), lambda i: (i, 0)),
        out_shape=jax.ShapeDtypeStruct((n, wn), F32),
        compiler_params=_cparams(("arbitrary",)),
        name="in_proj_small",
    )(x2, mod1, w_f32)


def _gelu_tanh(x):
    return 0.5 * x * (1.0 + jnp.tanh(0.7978845608028654 * (x + 0.044715 * (x * x * x))))


def _compress_kernel(g_ref, pe_ref, w1_ref, b1_ref, w2_ref, o_ref):
    half = CMP_STRIDE * HEAD_DIM
    g = g_ref[0, 0].astype(F32)
    ga = (g + pe_ref[0, 0:1, :]).astype(BF16)
    gb = (g + pe_ref[0, 1:2, :]).astype(BF16)
    u = jnp.dot(ga, w1_ref[0, 0:half, :], preferred_element_type=F32)
    v = jnp.dot(gb, w1_ref[0, half:2 * half, :], preferred_element_type=F32)
    ng = u.shape[0]
    v_next = pltpu.roll(v, ng - 1, 0)
    h = _gelu_tanh(u + v_next + b1_ref[0])
    o_ref[0, 0] = jnp.dot(h.astype(BF16), w2_ref[0], preferred_element_type=F32).astype(o_ref.dtype)


def compress(kvg, pe2, w1, b1, w2):
    bsz, four, ng, gd = kvg.shape
    return pl.pallas_call(
        _compress_kernel,
        grid=(bsz, four),
        in_specs=[pl.BlockSpec((1, 1, ng, gd), lambda b, j: (b, j, 0, 0)),
                  pl.BlockSpec((1, 2, gd), lambda b, j: (j // 2, 0, 0)),
                  pl.BlockSpec((1, 2 * gd, CMP_HIDDEN), lambda b, j: (j // 2, 0, 0)),
                  pl.BlockSpec((1, 1, CMP_HIDDEN), lambda b, j: (j // 2, 0, 0)),
                  pl.BlockSpec((1, CMP_HIDDEN, HEAD_DIM), lambda b, j: (j // 2, 0, 0))],
        out_specs=pl.BlockSpec((1, 1, ng, HEAD_DIM), lambda b, j: (b, j, 0, 0)),
        out_shape=jax.ShapeDtypeStruct((bsz, four, ng, HEAD_DIM), BF16),
        compiler_params=_cparams(("arbitrary", "arbitrary")),
        name="compress",
    )(kvg, pe2, w1, b1, w2)


def _head_slope(h):
    return 2.0 ** (-(h + 1.0))


def _cmp_attn_kernel(q_ref, kv_ref, gate_ref, ovl_ref, o_ref, imp_ref, *, tq):
    i = pl.program_id(1)
    ncp = kv_ref.shape[2]
    scale = HEAD_DIM ** -0.5
    qrel = lax.broadcasted_iota(I32, (tq, 1), 0)
    pos = i * tq + qrel
    cidx = lax.broadcasted_iota(I32, (1, ncp), 1)
    ok = (cidx * CMP_STRIDE + (CMP_BLOCK - 1)) <= pos
    any_ok = (pos >= CMP_BLOCK - 1).astype(F32)
    mid_rel = (cidx * CMP_STRIDE - i * tq).astype(F32) + (CMP_BLOCK - 1) / 2.0
    gates = _sigmoid(gate_ref[...])
    for g in range(NSA_KV_HEADS):
        kc = kv_ref[0, g]
        vc = kv_ref[0, NSA_KV_HEADS + g]
        psum = jnp.zeros((tq, ncp), F32)
        for r in range(NSA_GROUP):
            h = g * NSA_GROUP + r
            qh = q_ref[:, h * HEAD_DIM:(h + 1) * HEAD_DIM]
            s = lax.dot_general(qh, kc, (((1,), (1,)), ((), ())), preferred_element_type=F32) * scale
            s = s + _head_slope(h) * mid_rel
            s = jnp.where(ok, s, NEG_INF)
            m = jnp.max(s, axis=-1, keepdims=True)
            e = jnp.exp(s - m)
            p = e * (any_ok / jnp.sum(e, axis=-1, keepdims=True))
            psum = psum + p
            o = jnp.dot(p.astype(BF16), vc, preferred_element_type=F32)
            o_ref[:, h * HEAD_DIM:(h + 1) * HEAD_DIM] = o * gates[:, 3 * h:3 * h + 1]
        imp_ref[0, g] = jnp.dot(psum, ovl_ref[...], preferred_element_type=F32, precision=HIGHEST)


def cmp_attention(p_buf, kvc, s_buf, ovl, bsz, seq, tq=128):
    n = p_buf.shape[0]
    ncp = kvc.shape[2]
    nsel = ovl.shape[1]
    per_b = seq // tq
    return pl.pallas_call(
        functools.partial(_cmp_attn_kernel, tq=tq),
        grid=(bsz, per_b),
        in_specs=[pl.BlockSpec((tq, NSA_WIDTH), lambda b, i: (b * per_b + i, P_QNSA // NSA_WIDTH)),
                  pl.BlockSpec((1, 4, ncp, HEAD_DIM), lambda b, i: (b, 0, 0, 0)),
                  pl.BlockSpec((tq, LANES), lambda b, i: (b * per_b + i, 0)),
                  pl.BlockSpec((ncp, nsel), lambda b, i: (0, 0))],
        out_specs=[pl.BlockSpec((tq, NSA_WIDTH), lambda b, i: (b * per_b + i, 0)),
                   pl.BlockSpec((1, NSA_KV_HEADS, tq, nsel), lambda b, i: (b, 0, i, 0))],
        out_shape=[jax.ShapeDtypeStruct((n, NSA_WIDTH), F32),
                   jax.ShapeDtypeStruct((bsz, NSA_KV_HEADS, seq, nsel), F32)],
        compiler_params=_cparams(("arbitrary", "arbitrary")),
        name="cmp_attention",
    )(p_buf, kvc, s_buf, ovl)


def _topk_kernel(imp_ref, o_ref, *, seq, n_topk):
    rows, nsel = imp_ref.shape
    row0 = pl.program_id(0) * rows
    pos = (row0 + lax.broadcasted_iota(I32, (rows, 1), 0)) % seq
    cur = pos // SEL_BLOCK
    blk = lax.broadcasted_iota(I32, (1, nsel), 1)
    blkf = blk.astype(F32)
    forced = jnp.logical_or(blk == 0, jnp.logical_and(blk <= cur, blk > cur - SEL_LOCAL))
    work = jnp.where(forced, FORCE_SCORE, jnp.where(blk > cur, -1.0, imp_ref[...]))
    sel = jnp.zeros((rows, nsel), F32)
    for _ in range(n_topk):
        m = jnp.max(work, axis=-1, keepdims=True)
        first = jnp.min(jnp.where(work == m, blkf, float(nsel)), axis=-1, keepdims=True)
        pick = blkf == first
        sel = jnp.where(pick, 1.0, sel)
        work = jnp.where(pick, -2.0, work)
    keep = jnp.logical_and(sel > 0.5, blk <= cur)
    o_ref[...] = jnp.where(keep, 0.0, NEG_INF).astype(o_ref.dtype)


def topk_select(imp2, seq, rows=1024):
    nrow, nsel = imp2.shape
    rows = min(rows, nrow)
    return pl.pallas_call(
        functools.partial(_topk_kernel, seq=seq, n_topk=min(SEL_TOPK, nsel)),
        grid=(nrow // rows,),
        in_specs=[pl.BlockSpec((rows, nsel), lambda i: (i, 0))],
        out_specs=pl.BlockSpec((rows, nsel), lambda i: (i, 0)),
        out_shape=jax.ShapeDtypeStruct((nrow, nsel), BF16),
        compiler_params=_cparams(("arbitrary",)),
        name="topk_select",
    )(imp2)


def _sel_win_kernel(q_ref, ks_ref, vs_ref, kw_ref, vw_ref, sb_ref, e_ref, oc_ref, gate_ref, o_ref,
                    acc_ref, m_ref, l_ref, *, tq, tk, seq):
    g = pl.program_id(1)
    i = pl.program_id(2)
    q0 = i * tq
    scale = HEAD_DIM ** -0.5
    R = NSA_GROUP
    gslope = jnp.where(g == 0, 1.0, 2.0 ** (-NSA_GROUP))
    q = q_ref[...]
    qs = jnp.concatenate([q[:, r * HEAD_DIM:(r + 1) * HEAD_DIM] for r in range(R)], axis=0)
    qs = (qs.astype(F32) * scale).astype(BF16)
    selb = sb_ref[...]
    qrel = lax.broadcasted_iota(I32, (tq, 1), 0)
    lane = lax.broadcasted_iota(I32, (1, tk), 1)

    acc_ref[...] = jnp.zeros_like(acc_ref)
    m_ref[...] = jnp.full_like(m_ref, NEG_INF)
    l_ref[...] = jnp.zeros_like(l_ref)

    def tile(t, masked):
        k0 = pl.multiple_of(t * tk, tk)
        k = ks_ref[pl.ds(k0, tk), :]
        v = vs_ref[pl.ds(k0, tk), :]
        s = lax.dot_general(qs, k, (((1,), (1,)), ((), ())), preferred_element_type=F32)
        bsel = jnp.dot(selb, e_ref[t], preferred_element_type=F32)
        krel = k0 - q0 + lane
        krel_f = krel.astype(F32)
        if masked:
            bsel = jnp.where(krel <= qrel, bsel, NEG_INF)
        for r in range(R):
            rows = pl.ds(r * tq, tq)
            sr = s[r * tq:(r + 1) * tq] + (bsel + (gslope * 2.0 ** (-(r + 1))) * krel_f)
            m_prev = m_ref[rows, :]
            m_new = jnp.maximum(m_prev, jnp.max(sr, axis=-1, keepdims=True))
            alpha = jnp.exp(m_prev - m_new)
            p = jnp.exp(sr - m_new[:, 0:1])
            l_ref[rows, :] = alpha * l_ref[rows, :] + jnp.sum(p, axis=-1, keepdims=True)
            acc_ref[rows, :] = alpha * acc_ref[rows, :] + jnp.dot(p.astype(BF16), v,
                                                                  preferred_element_type=F32)
            m_ref[rows, :] = m_new

    t_last = q0 // tk

    def body(t, carry):
        tile(t, False)
        return carry

    lax.fori_loop(0, t_last, body, 0)
    tile(t_last, True)

    wk = WINDOW + tq
    w0 = pl.multiple_of(jnp.maximum(q0 - WINDOW, 0), tq)
    kw = kw_ref[pl.ds(w0, wk), :]
    vw = vw_ref[pl.ds(w0, wk), :]
    sw = lax.dot_general(qs, kw, (((1,), (1,)), ((), ())), preferred_element_type=F32)
    wrel = w0 - q0 + lax.broadcasted_iota(I32, (1, wk), 1)
    dist = qrel - wrel
    okw = jnp.logical_and(dist >= 0, dist < WINDOW)
    wrel_f = wrel.astype(F32)
    gates = _sigmoid(gate_ref[...])
    for r in range(R):
        sr = sw[r * tq:(r + 1) * tq] + (gslope * 2.0 ** (-(r + 1))) * wrel_f
        sr = jnp.where(okw, sr, NEG_INF)
        m = jnp.max(sr, axis=-1, keepdims=True)
        e = jnp.exp(sr - m)
        p = e * (1.0 / jnp.sum(e, axis=-1, keepdims=True))
        ow = jnp.dot(p.astype(BF16), vw, preferred_element_type=F32)
        osel = acc_ref[pl.ds(r * tq, tq), :] * (1.0 / l_ref[pl.ds(r * tq, tq), :][:, 0:1])
        g_sel = jnp.where(g == 0, gates[:, 3 * r + 1:3 * r + 2],
                          gates[:, 3 * (R + r) + 1:3 * (R + r) + 2])
        g_win = jnp.where(g == 0, gates[:, 3 * r + 2:3 * r + 3],
                          gates[:, 3 * (R + r) + 2:3 * (R + r) + 3])
        y = oc_ref[:, r * HEAD_DIM:(r + 1) * HEAD_DIM] + g_sel * osel + g_win * ow
        o_ref[:, r * HEAD_DIM:(r + 1) * HEAD_DIM] = y.astype(o_ref.dtype)


def sel_win_attention(p_buf, selb, emat, ocmp, s_buf, bsz, seq, tq=128, tk=512):
    n = p_buf.shape[0]
    tk = min(tk, seq)
    per_b = seq // tq
    nsel = selb.shape[1]
    gw = NSA_GROUP * HEAD_DIM
    G = NSA_KV_HEADS

    def colblk(off):
        return lambda b, g, i: (b, off // HEAD_DIM + g)

    return pl.pallas_call(
        functools.partial(_sel_win_kernel, tq=tq, tk=tk, seq=seq),
        grid=(bsz, G, per_b),
        in_specs=[pl.BlockSpec((tq, gw), lambda b, g, i: (b * per_b + i, P_QNSA // gw + g)),
                  pl.BlockSpec((seq, HEAD_DIM), colblk(P_KSEL)),
                  pl.BlockSpec((seq, HEAD_DIM), colblk(P_VSEL)),
                  pl.BlockSpec((seq, HEAD_DIM), colblk(P_KWIN)),
                  pl.BlockSpec((seq, HEAD_DIM), colblk(P_VWIN)),
                  pl.BlockSpec((tq, nsel), lambda b, g, i: ((b * G + g) * per_b + i, 0)),
                  pl.BlockSpec((seq // tk, nsel, tk), lambda b, g, i: (0, 0, 0)),
                  pl.BlockSpec((tq, gw), lambda b, g, i: (b * per_b + i, g)),
                  pl.BlockSpec((tq, LANES), lambda b, g, i: (b * per_b + i, 0))],
        out_specs=pl.BlockSpec((tq, gw), lambda b, g, i: (b * per_b + i, g)),
        out_shape=jax.ShapeDtypeStruct((n, NSA_WIDTH), BF16),
        scratch_shapes=[pltpu.VMEM((NSA_GROUP * tq, HEAD_DIM), F32),
                        pltpu.VMEM((NSA_GROUP * tq, LANES), F32),
                        pltpu.VMEM((NSA_GROUP * tq, LANES), F32)],
        compiler_params=_cparams(("arbitrary", "arbitrary", "arbitrary")),
        name="sel_win_attention",
    )(p_buf, p_buf, p_buf, p_buf, p_buf, selb, emat, ocmp, s_buf)


def _gla_kernel(q_ref, k_ref, v_ref, r_ref, a_ref, wg_ref, bg_ref, nw_ref, tril_ref, o_ref,
                st_ref, b_ref, *, tc):
    C, SUB, H = GLA_CHUNK, GLA_SUB, GLA_HEADS

    @pl.when(pl.program_id(1) == 0)
    def _():
        st_ref[...] = jnp.zeros_like(st_ref)

    x = jnp.dot(a_ref[...], wg_ref[...], preferred_element_type=F32, precision=HIGHEST) + bg_ref[...]
    la = (jnp.minimum(x, 0.0) - jnp.log(1.0 + jnp.exp(-jnp.abs(x)))) * (1.0 / GLA_GATE_TAU)
    tril = tril_ref[...]
    for c in range(tc // C):
        b_ref[c * C:(c + 1) * C, :] = jnp.dot(tril, la[c * C:(c + 1) * C, :],
                                              preferred_element_type=F32, precision=HIGHEST)

    row16 = lax.broadcasted_iota(I32, (SUB, C), 0)
    lane64 = lax.broadcasted_iota(I32, (SUB, C), 1)
    nw = nw_ref[...]

    def chunk(c, carry):
        r0 = pl.multiple_of(c * C, C)
        for h in range(H):
            bh = b_ref[pl.ds(r0, C), h * GLA_DK:(h + 1) * GLA_DK]
            qc = q_ref[pl.ds(r0, C), h * GLA_DK:(h + 1) * GLA_DK].astype(F32) * (GLA_DK ** -0.5)
            kc = k_ref[pl.ds(r0, C), h * GLA_DK:(h + 1) * GLA_DK].astype(F32)
            vc = v_ref[pl.ds(r0, C), h * GLA_DV:(h + 1) * GLA_DV]
            st = st_ref[h]
            b_last = bh[C - 1:C, :]
            qe = (qc * jnp.exp(bh)).astype(BF16)
            o = lax.dot_general(qe, st.astype(BF16), (((1,), (1,)), ((), ())),
                                preferred_element_type=F32)
            a_rows = []
            for sb in range(C // SUB):
                s0 = sb * SUB
                bi = bh[s0:s0 + SUB, :]
                qi = qc[s0:s0 + SUB, :]
                ki = kc[s0:s0 + SUB, :]
                beta = bh[s0:s0 + 1, :]
                if sb > 0:
                    qd = (qi * jnp.exp(bi - beta)).astype(BF16)
                    kd = (kc * jnp.exp(jnp.minimum(beta - bh, 0.0))).astype(BF16)
                    a_i = lax.dot_general(qd, kd, (((1,), (1,)), ((), ())), preferred_element_type=F32)
                    a_i = jnp.where(lane64 < s0, a_i, 0.0)
                else:
                    a_i = jnp.zeros((SUB, C), F32)
                for s in range(SUB):
                    e = jnp.exp(jnp.minimum(bi - bi[s:s + 1, :], 0.0))
                    col = jnp.sum(qi * ki[s:s + 1, :] * e, axis=-1, keepdims=True)
                    a_i = jnp.where(jnp.logical_and(lane64 == s0 + s, row16 >= s), col, a_i)
                a_rows.append(a_i)
            a_mat = jnp.concatenate(a_rows, axis=0)
            o = o + jnp.dot(a_mat.astype(BF16), vc, preferred_element_type=F32)
            kdec = (kc * jnp.exp(b_last - bh)).astype(BF16)
            upd = lax.dot_general(vc, kdec, (((0,), (0,)), ((), ())), preferred_element_type=F32)
            st_ref[h] = st * jnp.exp(b_last) + upd
            rms = lax.rsqrt(jnp.mean(o * o, axis=-1, keepdims=True) + RMS_EPS)
            rr = r_ref[pl.ds(r0, C), h * GLA_DV:(h + 1) * GLA_DV].astype(F32)
            y = o * rms * nw * (rr * _sigmoid(rr))
            o_ref[pl.ds(r0, C), h * GLA_DV:(h + 1) * GLA_DV] = y.astype(o_ref.dtype)
        return carry

    lax.fori_loop(0, tc // C, chunk, 0)


def gla_attention(p_buf, s_buf, wg_pad, bg, nw, tril, bsz, seq, tc=512):
    n = p_buf.shape[0]
    tc = min(tc, seq)
    per_b = seq // tc
    kw = GLA_HEADS * GLA_DK
    return pl.pallas_call(
        functools.partial(_gla_kernel, tc=tc),
        grid=(bsz, per_b),
        in_specs=[pl.BlockSpec((tc, kw), lambda b, j: (b * per_b + j, P_QGLA // kw)),
                  pl.BlockSpec((tc, kw), lambda b, j: (b * per_b + j, P_KGLA // kw)),
                  pl.BlockSpec((tc, GLA_WIDTH), lambda b, j: (b * per_b + j, P_VGLA // GLA_WIDTH)),
                  pl.BlockSpec((tc, GLA_WIDTH), lambda b, j: (b * per_b + j, P_RGLA // GLA_WIDTH)),
                  pl.BlockSpec((tc, LANES), lambda b, j: (b * per_b + j, 1)),
                  pl.BlockSpec((LANES, kw), lambda b, j: (0, 0)),
                  pl.BlockSpec((1, kw), lambda b, j: (0, 0)),
                  pl.BlockSpec((1, GLA_DV), lambda b, j: (0, 0)),
                  pl.BlockSpec((GLA_CHUNK, GLA_CHUNK), lambda b, j: (0, 0))],
        out_specs=pl.BlockSpec((tc, GLA_WIDTH), lambda b, j: (b * per_b + j, 0)),
        out_shape=jax.ShapeDtypeStruct((n, GLA_WIDTH), BF16),
        scratch_shapes=[pltpu.VMEM((GLA_HEADS, GLA_DV, GLA_DK), F32),
                        pltpu.VMEM((tc, kw), F32)],
        compiler_params=_cparams(("arbitrary", "arbitrary")),
        name="gla_attention",
    )(p_buf, p_buf, p_buf, p_buf, s_buf, wg_pad, bg, nw, tril)


def _pack_pairs(v):
    half = v.shape[-1] // 2
    lo = pltpu.bitcast(v[:, :half].astype(BF16).astype(F32), U32)
    hi = pltpu.bitcast(v[:, half:].astype(BF16).astype(F32), U32)
    return jnp.bitwise_or(hi, jnp.right_shift(lo, jnp.uint32(16)))


def _unpack_pairs(w):
    lo = pltpu.bitcast(jnp.left_shift(w, jnp.uint32(16)), F32)
    hi = pltpu.bitcast(jnp.bitwise_and(w, jnp.uint32(0xFFFF0000)), F32)
    return lo, hi


def _layer_norm(z, g, b):
    mu = jnp.mean(z, axis=-1, keepdims=True)
    zc = z - mu
    var = jnp.mean(zc * zc, axis=-1, keepdims=True)
    return zc * lax.rsqrt(var + LN_EPS) * g + b


def _outproj_kernel(yn_ref, yg_ref, wo_ref, x_ref, mod_ref, lng_ref, lnb_ref, wr_ref, br_ref, ltri_ref,
                    x1_ref, hw_ref, idx_ref, tw_ref, rank_ref, cnt_ref, carry_ref):
    step = pl.program_id(0)

    @pl.when(step == 0)
    def _():
        carry_ref[...] = jnp.zeros_like(carry_ref)

    half = yn_ref.shape[1]
    mix = (jnp.dot(yn_ref[...], wo_ref[0:half, :], preferred_element_type=F32) +
           jnp.dot(yg_ref[...], wo_ref[half:2 * half, :], preferred_element_type=F32))
    g1 = mod_ref[0, 0:1, :]
    sc2 = mod_ref[0, 1:2, :]
    sh2 = mod_ref[0, 2:3, :]
    x1 = _layer_norm(DN_ALPHA * x_ref[...] + g1 * mix, lng_ref[...], lnb_ref[...])
    x1_ref[...] = x1
    h2 = x1 * (1.0 + sc2) + sh2
    hw_ref[...] = _pack_pairs(h2)

    tm = h2.shape[0]
    logits = jnp.dot(h2, wr_ref[...], preferred_element_type=F32, precision=HIGHEST) + br_ref[...]
    ne = logits.shape[1]
    lane = lax.broadcasted_iota(I32, (1, ne), 1).astype(F32)
    out_lane = lax.broadcasted_iota(I32, (1, LANES), 1)
    work = logits
    idx_out = jnp.zeros((tm, LANES), F32)
    w_out = jnp.zeros((tm, LANES), F32)
    onehot = jnp.zeros((tm, ne), F32)
    picks = []
    top0 = None
    den = jnp.zeros((tm, 1), F32)
    for k in range(TOP_K):
        m = jnp.max(work, axis=-1, keepdims=True)
        first = jnp.min(jnp.where(work == m, lane, float(ne)), axis=-1, keepdims=True)
        pick = lane == first
        if k == 0:
            top0 = m
        e = jnp.exp(m - top0)
        den = den + e
        idx_out = jnp.where(out_lane == k, first, idx_out)
        w_out = jnp.where(out_lane == k, e, w_out)
        onehot = jnp.where(pick, 1.0, onehot)
        picks.append(pick)
        work = jnp.where(pick, -jnp.inf, work)
    idx_ref[...] = idx_out.astype(I32)
    tw_ref[...] = w_out * (1.0 / den)

    before = jnp.dot(ltri_ref[...], onehot.astype(BF16), preferred_element_type=F32) + carry_ref[0:1, :]
    rank_out = jnp.zeros((tm, LANES), F32)
    for k in range(TOP_K):
        rk = jnp.sum(jnp.where(picks[k], before, 0.0), axis=-1, keepdims=True)
        rank_out = jnp.where(out_lane == k, rk, rank_out)
    rank_ref[...] = rank_out.astype(I32)
    carry_ref[...] = carry_ref[...] + jnp.sum(onehot, axis=0, keepdims=True)
    cnt_ref[...] = carry_ref[...].astype(I32)


def out_proj_router(ynsa, ygla, wo_bf16, x2, mod2, ln_g, ln_b, w_router, b_router, ltri, seq, tm=512):
    n, d = x2.shape
    tm = min(tm, seq)
    per_b = seq // tm
    ne = w_router.shape[1]
    row = lambda i: (i, 0)
    const = lambda i: (0, 0)
    return pl.pallas_call(
        _outproj_kernel,
        grid=(n // tm,),
        in_specs=[pl.BlockSpec((tm, NSA_WIDTH), row),
                  pl.BlockSpec((tm, GLA_WIDTH), row),
                  pl.BlockSpec((d, d), const),
                  pl.BlockSpec((tm, d), row),
                  pl.BlockSpec((1, 3, d), lambda i: (i // per_b, 0, 0)),
                  pl.BlockSpec((1, d), const),
                  pl.BlockSpec((1, d), const),
                  pl.BlockSpec((d, ne), const),
                  pl.BlockSpec((1, ne), const),
                  pl.BlockSpec((tm, tm), const)],
        out_specs=[pl.BlockSpec((tm, d), row),
                   pl.BlockSpec((tm, d // 2), row),
                   pl.BlockSpec((tm, LANES), row),
                   pl.BlockSpec((tm, LANES), row),
                   pl.BlockSpec((tm, LANES), row),
                   pl.BlockSpec((8, ne), const)],
        out_shape=[jax.ShapeDtypeStruct((n, d), F32),
                   jax.ShapeDtypeStruct((n, d // 2), U32),
                   jax.ShapeDtypeStruct((n, LANES), I32),
                   jax.ShapeDtypeStruct((n, LANES), F32),
                   jax.ShapeDtypeStruct((n, LANES), I32),
                   jax.ShapeDtypeStruct((8, ne), I32)],
        scratch_shapes=[pltpu.VMEM((8, ne), F32)],
        compiler_params=_cparams(("arbitrary",)),
        name="out_proj_router",
    )(ynsa, ygla, wo_bf16, x2, mod2, ln_g, ln_b, w_router, b_router, ltri)


def _gather_rows_kernel(nused_ref, idx_ref, src_ref, o_ref, sem, *, rows):
    blk = pl.program_id(0)

    @pl.when(blk < nused_ref[0])
    def _():
        def issue(r, carry):
            tok = idx_ref[0, 0, r]
            pltpu.make_async_copy(src_ref.at[pl.ds(tok, 1)], o_ref.at[pl.ds(r, 1)], sem).start()
            return carry

        lax.fori_loop(0, rows, issue, 0)
        pltpu.make_async_copy(src_ref.at[pl.ds(0, rows)], o_ref, sem).wait()

    @pl.when(blk >= nused_ref[0])
    def _():
        o_ref[...] = jnp.zeros_like(o_ref)


def gather_rows(nused, idx3, src, rows):
    nblk = idx3.shape[0]
    w = src.shape[1]
    return pl.pallas_call(
        functools.partial(_gather_rows_kernel, rows=rows),
        grid_spec=pltpu.PrefetchScalarGridSpec(
            num_scalar_prefetch=1,
            grid=(nblk,),
            in_specs=[pl.BlockSpec((1, 1, rows), lambda i, nu: (i, 0, 0), memory_space=pltpu.SMEM),
                      pl.BlockSpec(memory_space=pl.ANY)],
            out_specs=pl.BlockSpec((rows, w), lambda i, nu: (i, 0)),
            scratch_shapes=[pltpu.SemaphoreType.DMA(())]),
        out_shape=jax.ShapeDtypeStruct((nblk * rows, w), src.dtype),
        compiler_params=_cparams(("arbitrary",)),
        name="dispatch_gather",
    )(nused, idx3, src)


def _moe_kernel(be_ref, nused_ref, x_ref, wg_ref, wu_ref, bg_ref, bu_ref, wd_ref, bd_ref, o_ref,
                xb_ref, acc_ref):
    i = pl.program_id(0)
    j = pl.program_id(1)
    nj = pl.num_programs(1)
    used = i < nused_ref[0]

    @pl.when(jnp.logical_and(used, j == 0))
    def _():
        lo, hi = _unpack_pairs(x_ref[...])
        half = lo.shape[1]
        xb_ref[:, 0:half] = lo.astype(BF16)
        xb_ref[:, half:2 * half] = hi.astype(BF16)
        acc_ref[...] = jnp.zeros_like(acc_ref)

    @pl.when(used)
    def _():
        xb = xb_ref[...]
        gt = jnp.dot(xb, wg_ref[0].astype(BF16), preferred_element_type=F32) + bg_ref[0]
        up = jnp.dot(xb, wu_ref[0].astype(BF16), preferred_element_type=F32) + bu_ref[0]
        gt = jnp.minimum(gt, SWIGLU_LIMIT)
        up = jnp.clip(up, -SWIGLU_LIMIT, SWIGLU_LIMIT)
        act = (up + 1.0) * gt * _sigmoid(SWIGLU_ALPHA * gt)
        acc_ref[...] += jnp.dot(act.astype(BF16), wd_ref[0].astype(BF16), preferred_element_type=F32)

    @pl.when(jnp.logical_and(used, j == nj - 1))
    def _():
        o_ref[...] = _pack_pairs(acc_ref[...] + bd_ref[0])

    @pl.when(jnp.logical_and(jnp.logical_not(used), j == nj - 1))
    def _():
        o_ref[...] = jnp.zeros_like(o_ref)


def moe_experts(block_expert, nused, xbuf, w_gate_up, b_gate_up, w_down, b_down, tm, tf=256):
    nslots, half = xbuf.shape
    d = 2 * half
    ne, _, f2 = w_gate_up.shape
    f = f2 // 2
    tf = min(tf, f)
    nfj = f // tf
    nblk = nslots // tm
    b_gu3 = b_gate_up.reshape(ne, 1, f2)
    b_d3 = b_down.reshape(ne, 1, d)

    def jj(i, j, nu):
        return jnp.where(i < nu[0], j, nfj - 1)

    return pl.pallas_call(
        _moe_kernel,
        grid_spec=pltpu.PrefetchScalarGridSpec(
            num_scalar_prefetch=2,
            grid=(nblk, nfj),
            in_specs=[pl.BlockSpec((tm, half), lambda i, j, be, nu: (i, 0)),
                      pl.BlockSpec((1, d, tf), lambda i, j, be, nu: (be[i], 0, jj(i, j, nu))),
                      pl.BlockSpec((1, d, tf), lambda i, j, be, nu: (be[i], 0, nfj + jj(i, j, nu))),
                      pl.BlockSpec((1, 1, tf), lambda i, j, be, nu: (be[i], 0, jj(i, j, nu))),
                      pl.BlockSpec((1, 1, tf), lambda i, j, be, nu: (be[i], 0, nfj + jj(i, j, nu))),
                      pl.BlockSpec((1, tf, d), lambda i, j, be, nu: (be[i], jj(i, j, nu), 0)),
                      pl.BlockSpec((1, 1, d), lambda i, j, be, nu: (be[i], 0, 0))],
            out_specs=pl.BlockSpec((tm, half), lambda i, j, be, nu: (i, 0)),
            scratch_shapes=[pltpu.VMEM((tm, d), BF16), pltpu.VMEM((tm, d), F32)]),
        out_shape=jax.ShapeDtypeStruct((nslots, half), U32),
        compiler_params=_cparams(("arbitrary", "arbitrary")),
        name="moe_experts",
    )(block_expert, nused, xbuf, w_gate_up, w_gate_up, b_gu3, b_gu3, w_down, b_d3)


def _combine_kernel(slot_ref, y_ref, tw_ref, x1_ref, mod_ref, lng_ref, lnb_ref, o_ref, yb_ref, sem, *, tc):
    def issue(a, carry):
        slot = slot_ref[0, 0, a]
        pltpu.make_async_copy(y_ref.at[pl.ds(slot, 1)], yb_ref.at[pl.ds(a, 1)], sem).start()
        return carry

    lax.fori_loop(0, TOP_K * tc, issue, 0)
    pltpu.make_async_copy(y_ref.at[pl.ds(0, TOP_K * tc)], yb_ref, sem).wait()

    half = yb_ref.shape[1]
    tw = tw_ref[...]
    ffn_lo = jnp.zeros((tc, half), F32)
    ffn_hi = jnp.zeros((tc, half), F32)
    for k in range(TOP_K):
        lo, hi = _unpack_pairs(yb_ref[k * tc:(k + 1) * tc, :])
        wk = tw[:, k:k + 1]
        ffn_lo = ffn_lo + wk * lo
        ffn_hi = ffn_hi + wk * hi
    ffn = jnp.concatenate([ffn_lo, ffn_hi], axis=-1)
    g2 = mod_ref[0, 0:1, :]
    o_ref[...] = _layer_norm(DN_ALPHA * x1_ref[...] + g2 * ffn, lng_ref[...], lnb_ref[...])


def combine(slot3, ybuf, top_w, x1, modg2, ln_g, ln_b, seq, tc=256):
    n, d = x1.shape
    tc = min(tc, seq)
    per_b = seq // tc
    half = ybuf.shape[1]
    return pl.pallas_call(
        functools.partial(_combine_kernel, tc=tc),
        grid=(n // tc,),
        in_specs=[pl.BlockSpec((1, 1, TOP_K * tc), lambda i: (i, 0, 0), memory_space=pltpu.SMEM),
                  pl.BlockSpec(memory_space=pl.ANY),
                  pl.BlockSpec((tc, LANES), lambda i: (i, 0)),
                  pl.BlockSpec((tc, d), lambda i: (i, 0)),
                  pl.BlockSpec((1, 1, d), lambda i: (i // per_b, 0, 0)),
                  pl.BlockSpec((1, d), lambda i: (0, 0)),
                  pl.BlockSpec((1, d), lambda i: (0, 0))],
        out_specs=pl.BlockSpec((tc, d), lambda i: (i, 0)),
        out_shape=jax.ShapeDtypeStruct((n, d), F32),
        scratch_shapes=[pltpu.VMEM((TOP_K * tc, half), U32), pltpu.SemaphoreType.DMA(())],
        compiler_params=_cparams(("arbitrary",)),
        name="combine",
    )(slot3, ybuf, top_w, x1, modg2, ln_g, ln_b)


def _split_w_in(w_in):
    offs = [0]
    for w in PROJ_WIDTHS:
        offs.append(offs[-1] + w)
    seg = [w_in[:, offs[k]:offs[k + 1]] for k in range(len(PROJ_WIDTHS))]
    (q_nsa, k_cmp, v_cmp, k_sel, v_sel, k_win, v_win, g_nsa, q_gla, k_gla, v_gla, a_gla, r_gla) = seg
    big = jnp.concatenate([q_nsa, v_gla, r_gla, k_cmp, v_cmp, k_sel, v_sel, k_win, v_win, q_gla, k_gla],
                          axis=1).astype(BF16)
    pad = lambda w: jnp.pad(w, ((0, 0), (0, LANES - w.shape[1])))
    small = jnp.concatenate([pad(g_nsa), pad(a_gla)], axis=1)
    return big, small


def _overlap_matrix(ncp, nsel):
    cs = jnp.arange(ncp)[:, None] * CMP_STRIDE
    ss = jnp.arange(nsel)[None, :] * SEL_BLOCK
    return jnp.logical_and(cs < ss + SEL_BLOCK, cs + CMP_BLOCK > ss).astype(F32)


def _expand_matrix(seq, nsel, tk):
    key_blk = jnp.arange(seq) // SEL_BLOCK
    e = (jnp.arange(nsel)[:, None] == key_blk[None, :]).astype(BF16)
    return e.reshape(nsel, seq // tk, tk).transpose(1, 0, 2)


MOE_TM = 1024


def kernel(x, c, w_ada, b_ada, w_in, cmp_pe_k, cmp_w1_k, cmp_b1_k, cmp_w2_k, cmp_pe_v, cmp_w1_v, cmp_b1_v,
           cmp_w2_v, gla_w_gate, gla_b_gate, gla_norm_w, w_out, ln1_g, ln1_b, w_router, b_router, w_gate_up,
           b_gate_up, w_down, b_down, ln2_g, ln2_b):
    return _forward(x, c, w_ada, b_ada, w_in, cmp_pe_k, cmp_w1_k, cmp_b1_k, cmp_w2_k, cmp_pe_v, cmp_w1_v,
                    cmp_b1_v, cmp_w2_v, gla_w_gate, gla_b_gate, gla_norm_w, w_out, ln1_g, ln1_b, w_router,
                    b_router, w_gate_up, b_gate_up, w_down, b_down, ln2_g, ln2_b, moe_tm=MOE_TM)


def _forward(x, c, w_ada, b_ada, w_in, cmp_pe_k, cmp_w1_k, cmp_b1_k, cmp_w2_k, cmp_pe_v, cmp_w1_v, cmp_b1_v,
             cmp_w2_v, gla_w_gate, gla_b_gate, gla_norm_w, w_out, ln1_g, ln1_b, w_router, b_router, w_gate_up,
             b_gate_up, w_down, b_down, ln2_g, ln2_b, *, moe_tm):
    bsz, seq, d = x.shape
    n = bsz * seq
    l = 0
    x2 = x.reshape(n, d)

    mod = ada_mod(c, w_ada[l], b_ada[l])
    sh1, sc1, g1, sh2, sc2, g2 = jnp.split(mod, 6, axis=-1)
    mod1 = jnp.stack([sc1, sh1], axis=1)
    mod2 = jnp.stack([g1, sc2, sh2], axis=1)
    modg2 = g2[:, None, :]

    w_big, w_small = _split_w_in(w_in[l])
    p_buf = in_proj(x2, mod1, w_big, seq)
    s_buf = in_proj_small(x2, mod1, w_small, seq)

    ng = seq // CMP_STRIDE
    kvg = p_buf[:, P_KVCMP:P_KVCMP + 2 * KV_WIDTH].reshape(bsz, ng, CMP_STRIDE, 4, HEAD_DIM)
    kvg = kvg.transpose(0, 3, 1, 2, 4).reshape(bsz, 4, ng, CMP_STRIDE * HEAD_DIM)
    half = CMP_STRIDE * HEAD_DIM
    pe2 = jnp.stack([cmp_pe_k[l].reshape(2, half), cmp_pe_v[l].reshape(2, half)])
    w1 = jnp.stack([cmp_w1_k[l], cmp_w1_v[l]]).astype(BF16)
    b1 = jnp.stack([cmp_b1_k[l], cmp_b1_v[l]])[:, None, :]
    w2 = jnp.stack([cmp_w2_k[l], cmp_w2_v[l]]).astype(BF16)
    kvc = compress(kvg, pe2, w1, b1, w2)

    nsel = seq // SEL_BLOCK
    ovl = _overlap_matrix(ng, nsel)
    ocmp, imp = cmp_attention(p_buf, kvc, s_buf, ovl, bsz, seq)
    selb = topk_select(imp.reshape(bsz * NSA_KV_HEADS * seq, nsel), seq)
    tk = min(512, seq)
    emat = _expand_matrix(seq, nsel, tk)
    y_nsa = sel_win_attention(p_buf, selb, emat, ocmp, s_buf, bsz, seq, tk=tk)

    wg_pad = jnp.pad(gla_w_gate[l], ((0, LANES - GLA_GATE_RANK), (0, 0)))
    tril = jnp.tril(jnp.ones((GLA_CHUNK, GLA_CHUNK), F32))
    y_gla = gla_attention(p_buf, s_buf, wg_pad, gla_b_gate[l][None, :], gla_norm_w[l][None, :], tril, bsz, seq)

    tm_r = min(512, seq)
    ltri = jnp.tril(jnp.ones((tm_r, tm_r), F32), k=-1).astype(BF16)
    x1, hw, top_idx, top_w, rank, counts = out_proj_router(
        y_nsa, y_gla, w_out[l].astype(BF16), x2, mod2, ln1_g[l][None, :], ln1_b[l][None, :],
        w_router[l], b_router[l][None, :], ltri, seq, tm=tm_r)

    tm = moe_tm
    n_asg = n * TOP_K
    cnt = counts[0]
    padded = (cnt + tm - 1) // tm * tm
    ends_p = jnp.cumsum(padded)
    start_p = ends_p - padded
    idx4 = top_idx[:, :TOP_K]
    slot_of = (start_p[idx4] + rank[:, :TOP_K]).astype(I32)
    n_blocks = -(-n_asg // tm) + N_EXPERTS
    n_slots = n_blocks * tm
    tok_of_asg = jnp.arange(n_asg, dtype=I32) // TOP_K
    slot_token = jnp.zeros((n_slots,), I32).at[slot_of.reshape(-1)].set(tok_of_asg)
    nused = (ends_p[-1] // tm).astype(I32).reshape(1)
    blk_start = jnp.minimum(jnp.arange(n_blocks, dtype=I32), nused - 1) * tm
    block_expert = jnp.minimum(jnp.searchsorted(ends_p, blk_start, side='right'), N_EXPERTS - 1).astype(I32)

    xbuf = gather_rows(nused, slot_token.reshape(n_blocks, 1, tm), hw, tm)
    ybuf = moe_experts(block_expert, nused, xbuf, w_gate_up[l], b_gate_up[l], w_down[l], b_down[l], tm)

    tc = min(256, seq)
    slot3 = slot_of.reshape(n // tc, tc, TOP_K).transpose(0, 2, 1).reshape(n // tc, 1, TOP_K * tc)
    out = combine(slot3, ybuf, top_w, x1, modg2, ln2_g[l][None, :], ln2_b[l][None, :], seq, tc=tc)
    return out.reshape(bsz, seq, d)
```

```python
import functools

import jax
import jax.numpy as jnp
from jax import lax
from jax.experimental import pallas as pl
from jax.experimental.pallas import tpu as pltpu

F32 = jnp.float32
BF16 = jnp.bfloat16
U32 = jnp.uint32
I32 = jnp.int32
HIGHEST = lax.Precision.HIGHEST

D_MODEL = 2048
HEAD_DIM = 128
NSA_HEADS = 8
NSA_KV_HEADS = 2
NSA_GROUP = 4
NSA_WIDTH = 1024
KV_WIDTH = 256
CMP_BLOCK = 32
CMP_STRIDE = 16
CMP_HIDDEN = 256
SEL_BLOCK = 64
SEL_TOPK = 16
SEL_LOCAL = 2
WINDOW = 512
GLA_HEADS = 4
GLA_DV = 256
GLA_DK = 128
GLA_WIDTH = 1024
GLA_GATE_RANK = 16
GLA_GATE_TAU = 16.0
GLA_CHUNK = 64
GLA_SUB = 16
N_EXPERTS = 32
TOP_K = 4
D_FF = 2048
SWIGLU_LIMIT = 7.0
SWIGLU_ALPHA = 1.702
LN_EPS = 1e-5
RMS_EPS = 1e-6
NEG_INF = -1e30
FORCE_SCORE = 1e6
DN_ALPHA = 2.0 ** 0.25
PROJ_WIDTHS = (1024, 256, 256, 256, 256, 256, 256, 24, 512, 512, 1024, 16, 1024)

LANES = 128
VMEM_LIMIT = 56 * 1024 * 1024
WORD_TILES = D_MODEL // (2 * LANES)
GATHER_UNROLL = 8
PIPE_UNROLL = 1

P_QNSA = 0
P_VGLA = 1024
P_RGLA = 2048
P_KVCMP = 3072
P_KSEL = 3584
P_VSEL = 3840
P_KWIN = 4096
P_VWIN = 4352
P_QGLA = 4608
P_KGLA = 5120
P_WIDTH = 5632
S_WIDTH = 256


def _cparams(sem, vmem=VMEM_LIMIT):
    return pltpu.CompilerParams(dimension_semantics=sem, vmem_limit_bytes=vmem)


def _sigmoid(x):
    return 1.0 / (1.0 + jnp.exp(-x))


def _ada_kernel(c_ref, w_ref, b_ref, o_ref):
    c = c_ref[...]
    sc = c * _sigmoid(c)
    o_ref[...] = jnp.dot(sc, w_ref[...], preferred_element_type=F32, precision=HIGHEST) + b_ref[...]


def ada_mod(c, w_ada, b_ada, tn=1024):
    bsz, d = c.shape
    n = w_ada.shape[1]
    return pl.pallas_call(
        _ada_kernel,
        grid=(n // tn,),
        in_specs=[pl.BlockSpec((bsz, d), lambda j: (0, 0)),
                  pl.BlockSpec((d, tn), lambda j: (0, j)),
                  pl.BlockSpec((1, tn), lambda j: (0, j))],
        out_specs=pl.BlockSpec((bsz, tn), lambda j: (0, j)),
        out_shape=jax.ShapeDtypeStruct((bsz, n), F32),
        compiler_params=_cparams(("arbitrary",)),
        name="ada_mod",
    )(c, w_ada, b_ada.reshape(1, n))


def _inproj_kernel(x_ref, mod_ref, w_ref, o_ref, hb_ref):
    @pl.when(pl.program_id(1) == 0)
    def _():
        sc = mod_ref[0, 0:1, :]
        sh = mod_ref[0, 1:2, :]
        hb_ref[...] = (x_ref[...] * (1.0 + sc) + sh).astype(BF16)

    o_ref[...] = jnp.dot(hb_ref[...], w_ref[...], preferred_element_type=F32).astype(o_ref.dtype)


def in_proj(x2, mod1, w_bf16, seq, tm=1024, tn=512):
    n, d = x2.shape
    wn = w_bf16.shape[1]
    tm = min(tm, seq)
    per_b = seq // tm
    return pl.pallas_call(
        _inproj_kernel,
        grid=(n // tm, wn // tn),
        in_specs=[pl.BlockSpec((tm, d), lambda i, j: (i, 0)),
                  pl.BlockSpec((1, 2, d), lambda i, j: (i // per_b, 0, 0)),
                  pl.BlockSpec((d, tn), lambda i, j: (0, j))],
        out_specs=pl.BlockSpec((tm, tn), lambda i, j: (i, j)),
        out_shape=jax.ShapeDtypeStruct((n, wn), BF16),
        scratch_shapes=[pltpu.VMEM((tm, d), BF16)],
        compiler_params=_cparams(("arbitrary", "arbitrary")),
        name="in_proj",
    )(x2, mod1, w_bf16)


def _inproj_small_kernel(x_ref, mod_ref, w_ref, o_ref):
    sc = mod_ref[0, 0:1, :]
    sh = mod_ref[0, 1:2, :]
    h = x_ref[...] * (1.0 + sc) + sh
    o_ref[...] = jnp.dot(h, w_ref[...], preferred_element_type=F32, precision=HIGHEST)


def in_proj_small(x2, mod1, w_f32, seq, tm=512):
    n, d = x2.shape
    wn = w_f32.shape[1]
    tm = min(tm, seq)
    per_b = seq // tm
    return pl.pallas_call(
        _inproj_small_kernel,
        grid=(n // tm,),
        in_specs=[pl.BlockSpec((tm, d), lambda i: (i, 0)),
                  pl.BlockSpec((1, 2, d), lambda i: (i // per_b, 0, 0)),
                  pl.BlockSpec((d, wn), lambda i: (0, 0))],
        out_specs=pl.BlockSpec((tm, wn), lambda i: (i, 0)),
        out_shape=jax.ShapeDtypeStruct((n, wn), F32),
        compiler_params=_cparams(("arbitrary",)),
        name="in_proj_small",
    )(x2, mod1, w_f32)


def _gelu_tanh(x):
    return 0.5 * x * (1.0 + jnp.tanh(0.7978845608028654 * (x + 0.044715 * (x * x * x))))


def _compress_kernel(g_ref, pe_ref, w1_ref, b1_ref, w2_ref, o_ref):
    half = CMP_STRIDE * HEAD_DIM
    g = g_ref[0, 0].astype(F32)
    ga = (g + pe_ref[0, 0:1, :]).astype(BF16)
    gb = (g + pe_ref[0, 1:2, :]).astype(BF16)
    u = jnp.dot(ga, w1_ref[0, 0:half, :], preferred_element_type=F32)
    v = jnp.dot(gb, w1_ref[0, half:2 * half, :], preferred_element_type=F32)
    ng = u.shape[0]
    v_next = pltpu.roll(v, ng - 1, 0)
    h = _gelu_tanh(u + v_next + b1_ref[0])
    o_ref[0, 0] = jnp.dot(h.astype(BF16), w2_ref[0], preferred_element_type=F32).astype(o_ref.dtype)


def compress(kvg, pe2, w1, b1, w2):
    bsz, four, ng, gd = kvg.shape
    return pl.pallas_call(
        _compress_kernel,
        grid=(bsz, four),
        in_specs=[pl.BlockSpec((1, 1, ng, gd), lambda b, j: (b, j, 0, 0)),
                  pl.BlockSpec((1, 2, gd), lambda b, j: (j // 2, 0, 0)),
                  pl.BlockSpec((1, 2 * gd, CMP_HIDDEN), lambda b, j: (j // 2, 0, 0)),
                  pl.BlockSpec((1, 1, CMP_HIDDEN), lambda b, j: (j // 2, 0, 0)),
                  pl.BlockSpec((1, CMP_HIDDEN, HEAD_DIM), lambda b, j: (j // 2, 0, 0))],
        out_specs=pl.BlockSpec((1, 1, ng, HEAD_DIM), lambda b, j: (b, j, 0, 0)),
        out_shape=jax.ShapeDtypeStruct((bsz, four, ng, HEAD_DIM), BF16),
        compiler_params=_cparams(("arbitrary", "arbitrary")),
        name="compress",
    )(kvg, pe2, w1, b1, w2)


def _head_slope(h):
    return 2.0 ** (-(h + 1.0))


def _cmp_attn_kernel(q_ref, kv_ref, gate_ref, ovl_ref, o_ref, imp_ref, *, tq):
    i = pl.program_id(1)
    ncp = kv_ref.shape[2]
    scale = HEAD_DIM ** -0.5
    qrel = lax.broadcasted_iota(I32, (tq, 1), 0)
    pos = i * tq + qrel
    cidx = lax.broadcasted_iota(I32, (1, ncp), 1)
    ok = (cidx * CMP_STRIDE + (CMP_BLOCK - 1)) <= pos
    any_ok = (pos >= CMP_BLOCK - 1).astype(F32)
    mid_rel = (cidx * CMP_STRIDE - i * tq).astype(F32) + (CMP_BLOCK - 1) / 2.0
    gates = _sigmoid(gate_ref[...])
    for g in range(NSA_KV_HEADS):
        kc = kv_ref[0, g]
        vc = kv_ref[0, NSA_KV_HEADS + g]
        psum = jnp.zeros((tq, ncp), F32)
        for r in range(NSA_GROUP):
            h = g * NSA_GROUP + r
            qh = q_ref[:, h * HEAD_DIM:(h + 1) * HEAD_DIM]
            s = lax.dot_general(qh, kc, (((1,), (1,)), ((), ())), preferred_element_type=F32) * scale
            s = s + _head_slope(h) * mid_rel
            s = jnp.where(ok, s, NEG_INF)
            m = jnp.max(s, axis=-1, keepdims=True)
            e = jnp.exp(s - m)
            p = e * (any_ok / jnp.sum(e, axis=-1, keepdims=True))
            psum = psum + p
            o = jnp.dot(p.astype(BF16), vc, preferred_element_type=F32)
            o_ref[:, h * HEAD_DIM:(h + 1) * HEAD_DIM] = o * gates[:, 3 * h:3 * h + 1]
        imp_ref[0, g] = jnp.dot(psum, ovl_ref[...], preferred_element_type=F32, precision=HIGHEST)


def cmp_attention(p_buf, kvc, s_buf, ovl, bsz, seq, tq=128):
    n = p_buf.shape[0]
    ncp = kvc.shape[2]
    nsel = ovl.shape[1]
    per_b = seq // tq
    return pl.pallas_call(
        functools.partial(_cmp_attn_kernel, tq=tq),
        grid=(bsz, per_b),
        in_specs=[pl.BlockSpec((tq, NSA_WIDTH), lambda b, i: (b * per_b + i, P_QNSA // NSA_WIDTH)),
                  pl.BlockSpec((1, 4, ncp, HEAD_DIM), lambda b, i: (b, 0, 0, 0)),
                  pl.BlockSpec((tq, LANES), lambda b, i: (b * per_b + i, 0)),
                  pl.BlockSpec((ncp, nsel), lambda b, i: (0, 0))],
        out_specs=[pl.BlockSpec((tq, NSA_WIDTH), lambda b, i: (b * per_b + i, 0)),
                   pl.BlockSpec((1, NSA_KV_HEADS, tq, nsel), lambda b, i: (b, 0, i, 0))],
        out_shape=[jax.ShapeDtypeStruct((n, NSA_WIDTH), F32),
                   jax.ShapeDtypeStruct((bsz, NSA_KV_HEADS, seq, nsel), F32)],
        compiler_params=_cparams(("arbitrary", "arbitrary")),
        name="cmp_attention",
    )(p_buf, kvc, s_buf, ovl)


def _topk_kernel(imp_ref, o_ref, *, seq, n_topk):
    rows, nsel = imp_ref.shape
    row0 = pl.program_id(0) * rows
    pos = (row0 + lax.broadcasted_iota(I32, (rows, 1), 0)) % seq
    cur = pos // SEL_BLOCK
    blk = lax.broadcasted_iota(I32, (1, nsel), 1)
    blkf = blk.astype(F32)
    forced = jnp.logical_or(blk == 0, jnp.logical_and(blk <= cur, blk > cur - SEL_LOCAL))
    work = jnp.where(forced, FORCE_SCORE, jnp.where(blk > cur, -1.0, imp_ref[...]))
    sel = jnp.zeros((rows, nsel), F32)
    for _ in range(n_topk):
        m = jnp.max(work, axis=-1, keepdims=True)
        first = jnp.min(jnp.where(work == m, blkf, float(nsel)), axis=-1, keepdims=True)
        pick = blkf == first
        sel = jnp.where(pick, 1.0, sel)
        work = jnp.where(pick, -2.0, work)
    keep = jnp.logical_and(sel > 0.5, blk <= cur)
    res = jnp.where(keep, 0.0, NEG_INF)
    if nsel < LANES:
        res = jnp.concatenate([res, jnp.full((rows, LANES - nsel), NEG_INF, F32)], axis=1)
    o_ref[...] = res.astype(o_ref.dtype)


def topk_select(imp2, seq, rows=1024):
    nrow, nsel = imp2.shape
    rows = min(rows, nrow)
    return pl.pallas_call(
        functools.partial(_topk_kernel, seq=seq, n_topk=min(SEL_TOPK, nsel)),
        grid=(nrow // rows,),
        in_specs=[pl.BlockSpec((rows, nsel), lambda i: (i, 0))],
        out_specs=pl.BlockSpec((rows, LANES), lambda i: (i, 0)),
        out_shape=jax.ShapeDtypeStruct((nrow, LANES), BF16),
        compiler_params=_cparams(("arbitrary",)),
        name="topk_select",
    )(imp2)


def _sel_win_kernel(q_ref, ks_ref, vst_ref, kw_ref, vwt_ref, sb_ref, kau_ref, kad_ref, oc_ref, gate_ref,
                    o_ref, qau_ref, qad_ref, qaw_ref, acc_ref, s_ref, p_ref, *, tq, tk, seq):
    g = pl.program_id(1)
    i = pl.program_id(2)
    q0 = i * tq
    scale = HEAD_DIM ** -0.5
    R = NSA_GROUP
    C = R * tq
    D = HEAD_DIM
    dn = (((1,), (1,)), ((), ()))
    gslope = jnp.where(g == 0, 1.0, 2.0 ** (-NSA_GROUP))
    slopes = [gslope * 2.0 ** (-(r + 1)) for r in range(R)]

    qrel_col = jnp.concatenate([lax.broadcasted_iota(I32, (1, tq), 1)] * R, axis=1)
    slope_col = jnp.concatenate([jnp.full((1, tq), 1.0, F32) * slopes[r] for r in range(R)], axis=1)

    q = q_ref[...]
    selb = sb_ref[...].astype(F32)
    lane = lax.broadcasted_iota(I32, (1, LANES), 1)
    nrel = ((lane - q0 // SEL_BLOCK) * SEL_BLOCK).astype(F32)
    for r in range(R):
        rows = slice(r * tq, (r + 1) * tq)
        qs = (q[:, r * D:(r + 1) * D].astype(F32) * scale).astype(BF16)
        sb_r = selb + slopes[r] * nrel
        qau_ref[rows, 0:D] = qs
        qau_ref[rows, D:2 * D] = jnp.where(lane == LANES - 1, slopes[r], sb_r).astype(BF16)
        qad_ref[rows, 0:D] = qs
        qad_ref[rows, D:2 * D] = sb_r.astype(BF16)
        qaw_ref[rows, 0:D] = qs
        aw = jnp.where(lane == 0, slopes[r] * SEL_BLOCK, jnp.where(lane == 1, slopes[r], 0.0))
        qaw_ref[rows, D:2 * D] = jnp.broadcast_to(aw, (tq, D)).astype(BF16)

    acc_ref[...] = jnp.zeros_like(acc_ref)

    def online_update(st, t, m, l):
        m_new = jnp.maximum(m, jnp.max(st, axis=0, keepdims=True))
        alpha = jnp.exp(m - m_new)
        p = jnp.exp(st - m_new)
        l_new = alpha * l + jnp.sum(p, axis=0, keepdims=True)
        pv = jnp.dot(vst_ref[t], p.astype(BF16), preferred_element_type=F32)
        acc_ref[...] = acc_ref[...] * alpha + pv
        return m_new, l_new

    S_SPARE, S_NEGINF, P_ZERO = 2, 3, 2

    @pl.when(jnp.logical_and(pl.program_id(0) == 0, jnp.logical_and(g == 0, i == 0)))
    def _():
        s_ref[S_NEGINF] = jnp.full((tk, C), -jnp.inf, F32)
        p_ref[P_ZERO] = jnp.zeros((tk, C), BF16)

    t_last = q0 // tk

    def pipelined(t, carry):
        m, l, alpha_c = carry
        tc = jnp.clip(t - 2, 0, t_last)
        pv = jnp.dot(vst_ref[tc], p_ref[jnp.where(t >= 2, t % 2, P_ZERO)], preferred_element_type=F32)
        acc_ref[...] = acc_ref[...] * alpha_c + pv
        tb = t - 1
        st = s_ref[jnp.where(jnp.logical_and(tb >= 0, tb < t_last), (t + 1) % 2, S_NEGINF)]
        m_new = jnp.maximum(m, jnp.max(st, axis=0, keepdims=True))
        alpha = jnp.exp(m - m_new)
        p = jnp.exp(st - m_new)
        l_new = alpha * l + jnp.sum(p, axis=0, keepdims=True)
        p_ref[(t + 1) % 2] = p.astype(BF16)
        ta = jnp.minimum(t, t_last - 1)
        k0 = pl.multiple_of(ta * tk, tk)
        ka = jnp.concatenate([ks_ref[pl.ds(k0, tk), :], kau_ref[pl.ds(k0, tk), :]], axis=1)
        s_ref[jnp.where(t < t_last, t % 2, S_SPARE)] = lax.dot_general(ka, qau_ref[...], dn,
                                                                       preferred_element_type=F32)
        return m_new, l_new, alpha

    m0 = jnp.full((1, C), NEG_INF, F32)
    l0 = jnp.zeros((1, C), F32)
    n_trip = jnp.where(t_last > 0, (t_last + 2 + PIPE_UNROLL - 1) // PIPE_UNROLL, 0)

    def trip(u, carry):
        for v in range(PIPE_UNROLL):
            carry = pipelined(u * PIPE_UNROLL + v, carry)
        return carry

    m, l, _ = lax.fori_loop(0, n_trip, trip, (m0, l0, jnp.ones((1, C), F32)))

    k0 = pl.multiple_of(t_last * tk, tk)
    ka = jnp.concatenate([ks_ref[pl.ds(k0, tk), :], kad_ref[pl.ds(k0, tk), :]], axis=1)
    st = lax.dot_general(ka, qad_ref[...], dn, preferred_element_type=F32)
    krow = lax.broadcasted_iota(I32, (tk, 1), 0)
    koff = (krow % SEL_BLOCK).astype(F32)
    st = st + koff * slope_col
    st = jnp.where(krow + (k0 - q0) <= qrel_col, st, NEG_INF)
    m, l = online_update(st, t_last, m, l)
    inv_l = 1.0 / l

    wk = WINDOW + tq
    w0 = pl.multiple_of(jnp.maximum(q0 - WINDOW, 0), tq)
    wrow = lax.broadcasted_iota(I32, (wk, 1), 0) + (w0 - q0)
    w_hi = jnp.floor(wrow.astype(F32) * (1.0 / SEL_BLOCK))
    w_lo = wrow.astype(F32) - w_hi * SEL_BLOCK
    kaw = jnp.where(lane == 0, w_hi, jnp.where(lane == 1, w_lo, 0.0)).astype(BF16)
    kwa = jnp.concatenate([kw_ref[pl.ds(w0, wk), :], kaw], axis=1)
    sw = lax.dot_general(kwa, qaw_ref[...], dn, preferred_element_type=F32)
    dist = qrel_col - wrow
    sw = jnp.where(jnp.logical_and(dist >= 0, dist < WINDOW), sw, NEG_INF)
    mw = jnp.max(sw, axis=0, keepdims=True)
    pw = jnp.exp(sw - mw)
    inv_lw = 1.0 / jnp.sum(pw, axis=0, keepdims=True)
    c0 = w0 // tq
    vwt = jnp.concatenate([vwt_ref[c0 + c] for c in range(wk // tq)], axis=1)
    ow = jnp.dot(vwt, pw.astype(BF16), preferred_element_type=F32)

    gates_t = _sigmoid(gate_ref[...]).T
    osel = acc_ref[...] * inv_l
    ow = ow * inv_lw
    for r in range(R):
        cols = slice(r * tq, (r + 1) * tq)
        g_sel = jnp.where(g == 0, gates_t[3 * r + 1:3 * r + 2, :], gates_t[3 * (R + r) + 1:3 * (R + r) + 2, :])
        g_win = jnp.where(g == 0, gates_t[3 * r + 2:3 * r + 3, :], gates_t[3 * (R + r) + 2:3 * (R + r) + 3, :])
        y_t = g_sel * osel[:, cols] + g_win * ow[:, cols]
        y = oc_ref[:, r * D:(r + 1) * D] + y_t.T
        o_ref[:, r * D:(r + 1) * D] = y.astype(o_ref.dtype)


def _key_side_constants(seq):
    nsel = seq // SEL_BLOCK
    key = jnp.arange(seq)
    blk = key // SEL_BLOCK
    lane = jnp.arange(LANES)[None, :]
    onehot = (lane == blk[:, None])
    kad = onehot.astype(BF16)
    kau = jnp.where(lane == LANES - 1, (key % SEL_BLOCK)[:, None].astype(F32),
                    jnp.logical_and(onehot, lane < nsel - SEL_LOCAL).astype(F32)).astype(BF16)
    return kau, kad


def sel_win_attention(p_buf, selb, ocmp, s_buf, bsz, seq, tq=128, tk=512):
    n = p_buf.shape[0]
    tk = min(tk, seq)
    per_b = seq // tq
    gw = NSA_GROUP * HEAD_DIM
    G = NSA_KV_HEADS
    D = HEAD_DIM
    kau, kad = _key_side_constants(seq)
    vsel = p_buf[:, P_VSEL:P_VSEL + KV_WIDTH].reshape(bsz, seq // tk, tk, G, D).transpose(0, 3, 1, 4, 2)
    vwin = p_buf[:, P_VWIN:P_VWIN + KV_WIDTH].reshape(bsz, seq // tq, tq, G, D).transpose(0, 3, 1, 4, 2)

    def colblk(off):
        return lambda b, g, i: (b, off // D + g)

    return pl.pallas_call(
        functools.partial(_sel_win_kernel, tq=tq, tk=tk, seq=seq),
        grid=(bsz, G, per_b),
        in_specs=[pl.BlockSpec((tq, gw), lambda b, g, i: (b * per_b + i, P_QNSA // gw + g)),
                  pl.BlockSpec((seq, D), colblk(P_KSEL)),
                  pl.BlockSpec((None, None, seq // tk, D, tk), lambda b, g, i: (b, g, 0, 0, 0)),
                  pl.BlockSpec((seq, D), colblk(P_KWIN)),
                  pl.BlockSpec((None, None, seq // tq, D, tq), lambda b, g, i: (b, g, 0, 0, 0)),
                  pl.BlockSpec((tq, LANES), lambda b, g, i: ((b * G + g) * per_b + i, 0)),
                  pl.BlockSpec((seq, LANES), lambda b, g, i: (0, 0)),
                  pl.BlockSpec((seq, LANES), lambda b, g, i: (0, 0)),
                  pl.BlockSpec((tq, gw), lambda b, g, i: (b * per_b + i, g)),
                  pl.BlockSpec((tq, LANES), lambda b, g, i: (b * per_b + i, 0))],
        out_specs=pl.BlockSpec((tq, gw), lambda b, g, i: (b * per_b + i, g)),
        out_shape=jax.ShapeDtypeStruct((n, NSA_WIDTH), BF16),
        scratch_shapes=[pltpu.VMEM((NSA_GROUP * tq, 2 * D), BF16),
                        pltpu.VMEM((NSA_GROUP * tq, 2 * D), BF16),
                        pltpu.VMEM((NSA_GROUP * tq, 2 * D), BF16),
                        pltpu.VMEM((D, NSA_GROUP * tq), F32),
                        pltpu.VMEM((4, tk, NSA_GROUP * tq), F32),
                        pltpu.VMEM((3, tk, NSA_GROUP * tq), BF16)],
        compiler_params=_cparams(("arbitrary", "arbitrary", "arbitrary")),
        name="sel_win_attention",
    )(p_buf, p_buf, vsel, p_buf, vwin, selb, kau, kad, ocmp, s_buf)


def _gla_kernel(q_ref, k_ref, v_ref, r_ref, a_ref, wg_ref, bg_ref, nw_ref, tril_ref, o_ref,
                st_ref, b_ref, *, tc):
    C, SUB, H = GLA_CHUNK, GLA_SUB, GLA_HEADS

    @pl.when(pl.program_id(1) == 0)
    def _():
        st_ref[...] = jnp.zeros_like(st_ref)

    x = jnp.dot(a_ref[...], wg_ref[...], preferred_element_type=F32, precision=HIGHEST) + bg_ref[...]
    la = (jnp.minimum(x, 0.0) - jnp.log(1.0 + jnp.exp(-jnp.abs(x)))) * (1.0 / GLA_GATE_TAU)
    tril = tril_ref[...]
    for c in range(tc // C):
        b_ref[c * C:(c + 1) * C, :] = jnp.dot(tril, la[c * C:(c + 1) * C, :],
                                              preferred_element_type=F32, precision=HIGHEST)

    row16 = lax.broadcasted_iota(I32, (SUB, C), 0)
    lane64 = lax.broadcasted_iota(I32, (SUB, C), 1)
    nw = nw_ref[...]

    def chunk(c, carry):
        r0 = pl.multiple_of(c * C, C)
        for h in range(H):
            bh = b_ref[pl.ds(r0, C), h * GLA_DK:(h + 1) * GLA_DK]
            qc = q_ref[pl.ds(r0, C), h * GLA_DK:(h + 1) * GLA_DK].astype(F32) * (GLA_DK ** -0.5)
            kc = k_ref[pl.ds(r0, C), h * GLA_DK:(h + 1) * GLA_DK].astype(F32)
            vc = v_ref[pl.ds(r0, C), h * GLA_DV:(h + 1) * GLA_DV]
            st = st_ref[h]
            b_last = bh[C - 1:C, :]
            qe = (qc * jnp.exp(bh)).astype(BF16)
            o = lax.dot_general(qe, st.astype(BF16), (((1,), (1,)), ((), ())),
                                preferred_element_type=F32)
            a_rows = []
            for sb in range(C // SUB):
                s0 = sb * SUB
                bi = bh[s0:s0 + SUB, :]
                qi = qc[s0:s0 + SUB, :]
                ki = kc[s0:s0 + SUB, :]
                beta = bh[s0:s0 + 1, :]
                if sb > 0:
                    qd = (qi * jnp.exp(bi - beta)).astype(BF16)
                    kd = (kc * jnp.exp(jnp.minimum(beta - bh, 0.0))).astype(BF16)
                    a_i = lax.dot_general(qd, kd, (((1,), (1,)), ((), ())), preferred_element_type=F32)
                    a_i = jnp.where(lane64 < s0, a_i, 0.0)
                else:
                    a_i = jnp.zeros((SUB, C), F32)
                for s in range(SUB):
                    e = jnp.exp(jnp.minimum(bi - bi[s:s + 1, :], 0.0))
                    col = jnp.sum(qi * ki[s:s + 1, :] * e, axis=-1, keepdims=True)
                    a_i = jnp.where(jnp.logical_and(lane64 == s0 + s, row16 >= s), col, a_i)
                a_rows.append(a_i)
            a_mat = jnp.concatenate(a_rows, axis=0)
            o = o + jnp.dot(a_mat.astype(BF16), vc, preferred_element_type=F32)
            kdec = (kc * jnp.exp(b_last - bh)).astype(BF16)
            upd = lax.dot_general(vc, kdec, (((0,), (0,)), ((), ())), preferred_element_type=F32)
            st_ref[h] = st * jnp.exp(b_last) + upd
            rms = lax.rsqrt(jnp.mean(o * o, axis=-1, keepdims=True) + RMS_EPS)
            rr = r_ref[pl.ds(r0, C), h * GLA_DV:(h + 1) * GLA_DV].astype(F32)
            y = o * rms * nw * (rr * _sigmoid(rr))
            o_ref[pl.ds(r0, C), h * GLA_DV:(h + 1) * GLA_DV] = y.astype(o_ref.dtype)
        return carry

    lax.fori_loop(0, tc // C, chunk, 0)


def gla_attention(p_buf, s_buf, wg_pad, bg, nw, tril, bsz, seq, tc=512):
    n = p_buf.shape[0]
    tc = min(tc, seq)
    per_b = seq // tc
    kw = GLA_HEADS * GLA_DK
    return pl.pallas_call(
        functools.partial(_gla_kernel, tc=tc),
        grid=(bsz, per_b),
        in_specs=[pl.BlockSpec((tc, kw), lambda b, j: (b * per_b + j, P_QGLA // kw)),
                  pl.BlockSpec((tc, kw), lambda b, j: (b * per_b + j, P_KGLA // kw)),
                  pl.BlockSpec((tc, GLA_WIDTH), lambda b, j: (b * per_b + j, P_VGLA // GLA_WIDTH)),
                  pl.BlockSpec((tc, GLA_WIDTH), lambda b, j: (b * per_b + j, P_RGLA // GLA_WIDTH)),
                  pl.BlockSpec((tc, LANES), lambda b, j: (b * per_b + j, 1)),
                  pl.BlockSpec((LANES, kw), lambda b, j: (0, 0)),
                  pl.BlockSpec((1, kw), lambda b, j: (0, 0)),
                  pl.BlockSpec((1, GLA_DV), lambda b, j: (0, 0)),
                  pl.BlockSpec((GLA_CHUNK, GLA_CHUNK), lambda b, j: (0, 0))],
        out_specs=pl.BlockSpec((tc, GLA_WIDTH), lambda b, j: (b * per_b + j, 0)),
        out_shape=jax.ShapeDtypeStruct((n, GLA_WIDTH), BF16),
        scratch_shapes=[pltpu.VMEM((GLA_HEADS, GLA_DV, GLA_DK), F32),
                        pltpu.VMEM((tc, kw), F32)],
        compiler_params=_cparams(("arbitrary", "arbitrary")),
        name="gla_attention",
    )(p_buf, p_buf, p_buf, p_buf, s_buf, wg_pad, bg, nw, tril)


def _pack_words(lo, hi):
    lo_b = pltpu.bitcast(lo.astype(BF16).astype(F32), U32)
    hi_b = pltpu.bitcast(hi.astype(BF16).astype(F32), U32)
    return jnp.bitwise_or(hi_b, jnp.right_shift(lo_b, jnp.uint32(16)))


def _unpack_words(w):
    lo = pltpu.bitcast(jnp.left_shift(w, jnp.uint32(16)), F32)
    hi = pltpu.bitcast(jnp.bitwise_and(w, jnp.uint32(0xFFFF0000)), F32)
    return lo, hi


def _store_packed(ref, v, first_tile=0):
    rows = v.shape[0]
    n_tiles = ref.shape[0] // rows
    for u in range(v.shape[1] // (2 * LANES)):
        w = _pack_words(v[:, 2 * u * LANES:(2 * u + 1) * LANES], v[:, (2 * u + 1) * LANES:(2 * u + 2) * LANES])
        ref[pl.ds(first_tile + u, rows, stride=n_tiles), :] = w


def _load_word_tile(ref, s, rows, base=0):
    n_tiles = WORD_TILES
    return ref[pl.ds(base * n_tiles + s, rows, stride=n_tiles), :]


def _layer_norm(z, g, b):
    mu = jnp.mean(z, axis=-1, keepdims=True)
    zc = z - mu
    var = jnp.mean(zc * zc, axis=-1, keepdims=True)
    return zc * lax.rsqrt(var + LN_EPS) * g + b


def _outproj_kernel(yn_ref, yg_ref, wo_ref, x_ref, mod_ref, lng_ref, lnb_ref, wr_ref, br_ref, ltri_ref,
                    x1_ref, hw_ref, idx_ref, tw_ref, rank_ref, cnt_ref, carry_ref):
    step = pl.program_id(0)

    @pl.when(step == 0)
    def _():
        carry_ref[...] = jnp.zeros_like(carry_ref)

    half = yn_ref.shape[1]
    mix = (jnp.dot(yn_ref[...], wo_ref[0:half, :], preferred_element_type=F32) +
           jnp.dot(yg_ref[...], wo_ref[half:2 * half, :], preferred_element_type=F32))
    g1 = mod_ref[0, 0:1, :]
    sc2 = mod_ref[0, 1:2, :]
    sh2 = mod_ref[0, 2:3, :]
    x1 = _layer_norm(DN_ALPHA * x_ref[...] + g1 * mix, lng_ref[...], lnb_ref[...])
    x1_ref[...] = x1
    h2 = x1 * (1.0 + sc2) + sh2
    _store_packed(hw_ref, h2)

    tm = h2.shape[0]
    logits = jnp.dot(h2, wr_ref[...], preferred_element_type=F32, precision=HIGHEST) + br_ref[...]
    ne = logits.shape[1]
    lane = lax.broadcasted_iota(I32, (1, ne), 1).astype(F32)
    out_lane = lax.broadcasted_iota(I32, (1, LANES), 1)
    work = logits
    idx_out = jnp.zeros((tm, LANES), F32)
    w_out = jnp.zeros((tm, LANES), F32)
    onehot = jnp.zeros((tm, ne), F32)
    picks = []
    top0 = None
    den = jnp.zeros((tm, 1), F32)
    for k in range(TOP_K):
        m = jnp.max(work, axis=-1, keepdims=True)
        first = jnp.min(jnp.where(work == m, lane, float(ne)), axis=-1, keepdims=True)
        pick = lane == first
        if k == 0:
            top0 = m
        e = jnp.exp(m - top0)
        den = den + e
        idx_out = jnp.where(out_lane == k, first, idx_out)
        w_out = jnp.where(out_lane == k, e, w_out)
        onehot = jnp.where(pick, 1.0, onehot)
        picks.append(pick)
        work = jnp.where(pick, -jnp.inf, work)
    idx_ref[...] = idx_out.astype(I32)
    tw_ref[...] = w_out * (1.0 / den)

    before = jnp.dot(ltri_ref[...], onehot.astype(BF16), preferred_element_type=F32) + carry_ref[0:1, :]
    rank_out = jnp.zeros((tm, LANES), F32)
    for k in range(TOP_K):
        rk = jnp.sum(jnp.where(picks[k], before, 0.0), axis=-1, keepdims=True)
        rank_out = jnp.where(out_lane == k, rk, rank_out)
    rank_ref[...] = rank_out.astype(I32)
    carry_ref[...] = carry_ref[...] + jnp.sum(onehot, axis=0, keepdims=True)
    cnt_ref[...] = carry_ref[...].astype(I32)


def out_proj_router(ynsa, ygla, wo_bf16, x2, mod2, ln_g, ln_b, w_router, b_router, ltri, seq, tm=512):
    n, d = x2.shape
    tm = min(tm, seq)
    per_b = seq // tm
    ne = w_router.shape[1]
    row = lambda i: (i, 0)
    const = lambda i: (0, 0)
    return pl.pallas_call(
        _outproj_kernel,
        grid=(n // tm,),
        in_specs=[pl.BlockSpec((tm, NSA_WIDTH), row),
                  pl.BlockSpec((tm, GLA_WIDTH), row),
                  pl.BlockSpec((d, d), const),
                  pl.BlockSpec((tm, d), row),
                  pl.BlockSpec((1, 3, d), lambda i: (i // per_b, 0, 0)),
                  pl.BlockSpec((1, d), const),
                  pl.BlockSpec((1, d), const),
                  pl.BlockSpec((d, ne), const),
                  pl.BlockSpec((1, ne), const),
                  pl.BlockSpec((tm, tm), const)],
        out_specs=[pl.BlockSpec((tm, d), row),
                   pl.BlockSpec((tm * WORD_TILES, LANES), row),
                   pl.BlockSpec((tm, LANES), row),
                   pl.BlockSpec((tm, LANES), row),
                   pl.BlockSpec((tm, LANES), row),
                   pl.BlockSpec((8, ne), const)],
        out_shape=[jax.ShapeDtypeStruct((n, d), F32),
                   jax.ShapeDtypeStruct((n * WORD_TILES, LANES), U32),
                   jax.ShapeDtypeStruct((n, LANES), I32),
                   jax.ShapeDtypeStruct((n, LANES), F32),
                   jax.ShapeDtypeStruct((n, LANES), I32),
                   jax.ShapeDtypeStruct((8, ne), I32)],
        scratch_shapes=[pltpu.VMEM((8, ne), F32)],
        compiler_params=_cparams(("arbitrary",)),
        name="out_proj_router",
    )(ynsa, ygla, wo_bf16, x2, mod2, ln_g, ln_b, w_router, b_router, ltri)


def _issue_token_copies(idx_at, src_ref, dst_ref, sem, count):
    wt = WORD_TILES

    def issue(r8, carry):
        for u in range(GATHER_UNROLL):
            r = r8 * GATHER_UNROLL + u
            tok = idx_at(r)
            pltpu.make_async_copy(src_ref.at[pl.ds(pl.multiple_of(tok * wt, wt), wt)],
                                  dst_ref.at[pl.ds(pl.multiple_of(r * wt, wt), wt)], sem).start(priority=u % 2)
        return carry

    lax.fori_loop(0, count // GATHER_UNROLL, issue, 0)


def _gather_rows_kernel(nused_ref, idx_ref, src_ref, o_ref, sem, *, rows):
    blk = pl.program_id(0)

    @pl.when(blk < nused_ref[0])
    def _():
        _issue_token_copies(lambda r: idx_ref[0, 0, r], src_ref, o_ref, sem, rows)
        pltpu.make_async_copy(src_ref.at[pl.ds(0, rows * WORD_TILES)], o_ref, sem).wait()

    @pl.when(blk >= nused_ref[0])
    def _():
        o_ref[...] = jnp.zeros_like(o_ref)


def gather_rows(nused, idx3, src, rows):
    nblk = idx3.shape[0]
    wt = WORD_TILES
    return pl.pallas_call(
        functools.partial(_gather_rows_kernel, rows=rows),
        grid_spec=pltpu.PrefetchScalarGridSpec(
            num_scalar_prefetch=1,
            grid=(nblk,),
            in_specs=[pl.BlockSpec((1, 1, rows), lambda i, nu: (i, 0, 0), memory_space=pltpu.SMEM),
                      pl.BlockSpec(memory_space=pl.ANY)],
            out_specs=pl.BlockSpec((rows * wt, LANES), lambda i, nu: (i, 0)),
            scratch_shapes=[pltpu.SemaphoreType.DMA(())]),
        out_shape=jax.ShapeDtypeStruct((nblk * rows * wt, LANES), src.dtype),
        compiler_params=_cparams(("arbitrary",)),
        name="dispatch_gather",
    )(nused, idx3, src)


def _moe_kernel(be_ref, nused_ref, x_ref, wg_ref, wu_ref, bg_ref, bu_ref, wd_ref, bd_ref, o_ref,
                xb_ref, act_ref, *, tm, nj1, nj2):
    i = pl.program_id(0)
    j = pl.program_id(1)
    used = i < nused_ref[0]
    tf = act_ref.shape[2]
    td = wd_ref.shape[2]

    @pl.when(jnp.logical_and(used, j == 0))
    def _():
        for s in range(WORD_TILES):
            lo, hi = _unpack_words(_load_word_tile(x_ref, s, tm))
            xb_ref[:, 2 * s * LANES:(2 * s + 1) * LANES] = lo.astype(BF16)
            xb_ref[:, (2 * s + 1) * LANES:(2 * s + 2) * LANES] = hi.astype(BF16)

    @pl.when(jnp.logical_and(used, j < nj1))
    def _():
        xb = xb_ref[...]
        gt = jnp.dot(xb, wg_ref[0].astype(BF16), preferred_element_type=F32) + bg_ref[0]
        up = jnp.dot(xb, wu_ref[0].astype(BF16), preferred_element_type=F32) + bu_ref[0]
        gt = jnp.minimum(gt, SWIGLU_LIMIT)
        up = jnp.clip(up, -SWIGLU_LIMIT, SWIGLU_LIMIT)
        act = (up + 1.0) * gt * _sigmoid(SWIGLU_ALPHA * gt)
        act_ref[jnp.minimum(j, nj1 - 1)] = act.astype(BF16)

    @pl.when(jnp.logical_and(used, j >= nj1))
    def _():
        y = jnp.dot(act_ref[0], wd_ref[0, 0:tf, :].astype(BF16), preferred_element_type=F32)
        for jj in range(1, nj1):
            y = y + jnp.dot(act_ref[jj], wd_ref[0, jj * tf:(jj + 1) * tf, :].astype(BF16),
                            preferred_element_type=F32)
        y = y + bd_ref[0]
        _store_packed(o_ref, y, first_tile=(j - nj1) * (td // (2 * LANES)))

    @pl.when(jnp.logical_and(jnp.logical_not(used), j == nj1 + nj2 - 1))
    def _():
        o_ref[...] = jnp.zeros_like(o_ref)


def moe_experts(block_expert, nused, xbuf, w_gate_up, b_gate_up, w_down, b_down, tm, tf=256, td=512):
    wt = WORD_TILES
    nslots = xbuf.shape[0] // wt
    ne, d, f2 = w_gate_up.shape
    f = f2 // 2
    tf = min(tf, f)
    nj1 = f // tf
    nj2 = d // td
    nblk = nslots // tm
    b_gu3 = b_gate_up.reshape(ne, 1, f2)
    b_d3 = b_down.reshape(ne, 1, d)

    def j1(i, j, nu):
        return jnp.where(i < nu[0], jnp.minimum(j, nj1 - 1), nj1 - 1)

    def j2(i, j, nu):
        return jnp.where(i < nu[0], jnp.maximum(j - nj1, 0), nj2 - 1)

    return pl.pallas_call(
        functools.partial(_moe_kernel, tm=tm, nj1=nj1, nj2=nj2),
        grid_spec=pltpu.PrefetchScalarGridSpec(
            num_scalar_prefetch=2,
            grid=(nblk, nj1 + nj2),
            in_specs=[pl.BlockSpec((tm * wt, LANES), lambda i, j, be, nu: (i, 0)),
                      pl.BlockSpec((1, d, tf), lambda i, j, be, nu: (be[i], 0, j1(i, j, nu))),
                      pl.BlockSpec((1, d, tf), lambda i, j, be, nu: (be[i], 0, nj1 + j1(i, j, nu))),
                      pl.BlockSpec((1, 1, tf), lambda i, j, be, nu: (be[i], 0, j1(i, j, nu))),
                      pl.BlockSpec((1, 1, tf), lambda i, j, be, nu: (be[i], 0, nj1 + j1(i, j, nu))),
                      pl.BlockSpec((1, f, td), lambda i, j, be, nu: (be[i], 0, j2(i, j, nu))),
                      pl.BlockSpec((1, 1, td), lambda i, j, be, nu: (be[i], 0, j2(i, j, nu)))],
            out_specs=pl.BlockSpec((tm * wt, LANES), lambda i, j, be, nu: (i, 0)),
            scratch_shapes=[pltpu.VMEM((tm, d), BF16), pltpu.VMEM((nj1, tm, tf), BF16)]),
        out_shape=jax.ShapeDtypeStruct((nslots * wt, LANES), U32),
        compiler_params=_cparams(("arbitrary", "arbitrary")),
        name="moe_experts",
    )(block_expert, nused, xbuf, w_gate_up, w_gate_up, b_gu3, b_gu3, w_down, b_d3)


def _combine_kernel(slot_ref, y_ref, tw_ref, x1_ref, mod_ref, lng_ref, lnb_ref, o_ref, yb_ref, sem, *, tc):
    _issue_token_copies(lambda a: slot_ref[0, 0, a], y_ref, yb_ref, sem, TOP_K * tc)
    pltpu.make_async_copy(y_ref.at[pl.ds(0, TOP_K * tc * WORD_TILES)], yb_ref, sem).wait()

    tw = tw_ref[...]
    wks = [tw[:, k:k + 1] for k in range(TOP_K)]
    cols = []
    for s in range(WORD_TILES):
        f_lo = jnp.zeros((tc, LANES), F32)
        f_hi = jnp.zeros((tc, LANES), F32)
        for k in range(TOP_K):
            lo, hi = _unpack_words(_load_word_tile(yb_ref, s, tc, base=k * tc))
            f_lo = f_lo + wks[k] * lo
            f_hi = f_hi + wks[k] * hi
        cols += [f_lo, f_hi]
    ffn = jnp.concatenate(cols, axis=-1)
    g2 = mod_ref[0, 0:1, :]
    o_ref[...] = _layer_norm(DN_ALPHA * x1_ref[...] + g2 * ffn, lng_ref[...], lnb_ref[...])


def combine(slot3, ybuf, top_w, x1, modg2, ln_g, ln_b, seq, tc=256):
    n, d = x1.shape
    tc = min(tc, seq)
    per_b = seq // tc
    return pl.pallas_call(
        functools.partial(_combine_kernel, tc=tc),
        grid=(n // tc,),
        in_specs=[pl.BlockSpec((1, 1, TOP_K * tc), lambda i: (i, 0, 0), memory_space=pltpu.SMEM),
                  pl.BlockSpec(memory_space=pl.ANY),
                  pl.BlockSpec((tc, LANES), lambda i: (i, 0)),
                  pl.BlockSpec((tc, d), lambda i: (i, 0)),
                  pl.BlockSpec((1, 1, d), lambda i: (i // per_b, 0, 0)),
                  pl.BlockSpec((1, d), lambda i: (0, 0)),
                  pl.BlockSpec((1, d), lambda i: (0, 0))],
        out_specs=pl.BlockSpec((tc, d), lambda i: (i, 0)),
        out_shape=jax.ShapeDtypeStruct((n, d), F32),
        scratch_shapes=[pltpu.VMEM((TOP_K * tc * WORD_TILES, LANES), U32), pltpu.SemaphoreType.DMA(())],
        compiler_params=_cparams(("arbitrary",)),
        name="combine",
    )(slot3, ybuf, top_w, x1, modg2, ln_g, ln_b)


def _split_w_in(w_in):
    offs = [0]
    for w in PROJ_WIDTHS:
        offs.append(offs[-1] + w)
    seg = [w_in[:, offs[k]:offs[k + 1]] for k in range(len(PROJ_WIDTHS))]
    (q_nsa, k_cmp, v_cmp, k_sel, v_sel, k_win, v_win, g_nsa, q_gla, k_gla, v_gla, a_gla, r_gla) = seg
    big = jnp.concatenate([q_nsa, v_gla, r_gla, k_cmp, v_cmp, k_sel, v_sel, k_win, v_win, q_gla, k_gla],
                          axis=1).astype(BF16)
    pad = lambda w: jnp.pad(w, ((0, 0), (0, LANES - w.shape[1])))
    small = jnp.concatenate([pad(g_nsa), pad(a_gla)], axis=1)
    return big, small


def _overlap_matrix(ncp, nsel):
    cs = jnp.arange(ncp)[:, None] * CMP_STRIDE
    ss = jnp.arange(nsel)[None, :] * SEL_BLOCK
    return jnp.logical_and(cs < ss + SEL_BLOCK, cs + CMP_BLOCK > ss).astype(F32)


MOE_TM = 1024


def kernel(x, c, w_ada, b_ada, w_in, cmp_pe_k, cmp_w1_k, cmp_b1_k, cmp_w2_k, cmp_pe_v, cmp_w1_v, cmp_b1_v,
           cmp_w2_v, gla_w_gate, gla_b_gate, gla_norm_w, w_out, ln1_g, ln1_b, w_router, b_router, w_gate_up,
           b_gate_up, w_down, b_down, ln2_g, ln2_b):
    return _forward(x, c, w_ada, b_ada, w_in, cmp_pe_k, cmp_w1_k, cmp_b1_k, cmp_w2_k, cmp_pe_v, cmp_w1_v,
                    cmp_b1_v, cmp_w2_v, gla_w_gate, gla_b_gate, gla_norm_w, w_out, ln1_g, ln1_b, w_router,
                    b_router, w_gate_up, b_gate_up, w_down, b_down, ln2_g, ln2_b, moe_tm=MOE_TM)


def _forward(x, c, w_ada, b_ada, w_in, cmp_pe_k, cmp_w1_k, cmp_b1_k, cmp_w2_k, cmp_pe_v, cmp_w1_v, cmp_b1_v,
             cmp_w2_v, gla_w_gate, gla_b_gate, gla_norm_w, w_out, ln1_g, ln1_b, w_router, b_router, w_gate_up,
             b_gate_up, w_down, b_down, ln2_g, ln2_b, *, moe_tm):
    bsz, seq, d = x.shape
    n = bsz * seq
    l = 0
    x2 = x.reshape(n, d)

    mod = ada_mod(c, w_ada[l], b_ada[l])
    sh1, sc1, g1, sh2, sc2, g2 = jnp.split(mod, 6, axis=-1)
    mod1 = jnp.stack([sc1, sh1], axis=1)
    mod2 = jnp.stack([g1, sc2, sh2], axis=1)
    modg2 = g2[:, None, :]

    w_big, w_small = _split_w_in(w_in[l])
    p_buf = in_proj(x2, mod1, w_big, seq)
    s_buf = in_proj_small(x2, mod1, w_small, seq)

    ng = seq // CMP_STRIDE
    kvg = p_buf[:, P_KVCMP:P_KVCMP + 2 * KV_WIDTH].reshape(bsz, ng, CMP_STRIDE, 4, HEAD_DIM)
    kvg = kvg.transpose(0, 3, 1, 2, 4).reshape(bsz, 4, ng, CMP_STRIDE * HEAD_DIM)
    half = CMP_STRIDE * HEAD_DIM
    pe2 = jnp.stack([cmp_pe_k[l].reshape(2, half), cmp_pe_v[l].reshape(2, half)])
    w1 = jnp.stack([cmp_w1_k[l], cmp_w1_v[l]]).astype(BF16)
    b1 = jnp.stack([cmp_b1_k[l], cmp_b1_v[l]])[:, None, :]
    w2 = jnp.stack([cmp_w2_k[l], cmp_w2_v[l]]).astype(BF16)
    kvc = compress(kvg, pe2, w1, b1, w2)

    nsel = seq // SEL_BLOCK
    ovl = _overlap_matrix(ng, nsel)
    ocmp, imp = cmp_attention(p_buf, kvc, s_buf, ovl, bsz, seq)
    selb = topk_select(imp.reshape(bsz * NSA_KV_HEADS * seq, nsel), seq)
    y_nsa = sel_win_attention(p_buf, selb, ocmp, s_buf, bsz, seq)

    wg_pad = jnp.pad(gla_w_gate[l], ((0, LANES - GLA_GATE_RANK), (0, 0)))
    tril = jnp.tril(jnp.ones((GLA_CHUNK, GLA_CHUNK), F32))
    y_gla = gla_attention(p_buf, s_buf, wg_pad, gla_b_gate[l][None, :], gla_norm_w[l][None, :], tril, bsz, seq)

    tm_r = min(512, seq)
    ltri = jnp.tril(jnp.ones((tm_r, tm_r), F32), k=-1).astype(BF16)
    x1, hw, top_idx, top_w, rank, counts = out_proj_router(
        y_nsa, y_gla, w_out[l].astype(BF16), x2, mod2, ln1_g[l][None, :], ln1_b[l][None, :],
        w_router[l], b_router[l][None, :], ltri, seq, tm=tm_r)

    tm = moe_tm
    n_asg = n * TOP_K
    cnt = counts[0]
    padded = (cnt + tm - 1) // tm * tm
    ends_p = jnp.cumsum(padded)
    start_p = ends_p - padded
    idx4 = top_idx[:, :TOP_K]
    slot_of = (start_p[idx4] + rank[:, :TOP_K]).astype(I32)
    n_blocks = -(-n_asg // tm) + N_EXPERTS
    n_slots = n_blocks * tm
    tok_of_asg = jnp.arange(n_asg, dtype=I32) // TOP_K
    slot_token = jnp.zeros((n_slots,), I32).at[slot_of.reshape(-1)].set(tok_of_asg)
    nused = (ends_p[-1] // tm).astype(I32).reshape(1)
    blk_start = jnp.minimum(jnp.arange(n_blocks, dtype=I32), nused - 1) * tm
    block_expert = jnp.minimum(jnp.searchsorted(ends_p, blk_start, side='right'), N_EXPERTS - 1).astype(I32)

    xbuf = gather_rows(nused, slot_token.reshape(n_blocks, 1, tm), hw, tm)
    ybuf = moe_experts(block_expert, nused, xbuf, w_gate_up[l], b_gate_up[l], w_down[l], b_down[l], tm)

    tc = min(256, seq)
    slot3 = slot_of.reshape(n // tc, tc, TOP_K).transpose(0, 2, 1).reshape(n // tc, 1, TOP_K * tc)
    out = combine(slot3, ybuf, top_w, x1, modg2, ln2_g[l][None, :], ln2_b[l][None, :], seq, tc=tc)
    return out.reshape(bsz, seq, d)
```

```python
import functools

import jax
import jax.numpy as jnp
from jax import lax
from jax.experimental import pallas as pl
from jax.experimental.pallas import tpu as pltpu

F32 = jnp.float32
BF16 = jnp.bfloat16
U32 = jnp.uint32
I32 = jnp.int32
HIGHEST = lax.Precision.HIGHEST

D_MODEL = 2048
HEAD_DIM = 128
NSA_HEADS = 8
NSA_KV_HEADS = 2
NSA_GROUP = 4
NSA_WIDTH = 1024
KV_WIDTH = 256
CMP_BLOCK = 32
CMP_STRIDE = 16
CMP_HIDDEN = 256
SEL_BLOCK = 64
SEL_TOPK = 16
SEL_LOCAL = 2
WINDOW = 512
GLA_HEADS = 4
GLA_DV = 256
GLA_DK = 128
GLA_WIDTH = 1024
GLA_GATE_RANK = 16
GLA_GATE_TAU = 16.0
GLA_CHUNK = 64
GLA_SUB = 16
N_EXPERTS = 32
TOP_K = 4
D_FF = 2048
SWIGLU_LIMIT = 7.0
SWIGLU_ALPHA = 1.702
LN_EPS = 1e-5
RMS_EPS = 1e-6
NEG_INF = -1e30
FORCE_SCORE = 1e6
DN_ALPHA = 2.0 ** 0.25
PROJ_WIDTHS = (1024, 256, 256, 256, 256, 256, 256, 24, 512, 512, 1024, 16, 1024)

LANES = 128
VMEM_LIMIT = 56 * 1024 * 1024
WORD_TILES = D_MODEL // (2 * LANES)
GATHER_UNROLL = 8
NSA_TQ = 128
P_QNSA = 0
P_VGLA = 1024
P_RGLA = 2048
P_KVCMP = 3072
P_KSEL = 3584
P_VSEL = 3840
P_KWIN = 4096
P_VWIN = 4352
P_QGLA = 4608
P_KGLA = 5120
P_WIDTH = 5632
S_WIDTH = 256


def _cparams(sem, vmem=VMEM_LIMIT):
    return pltpu.CompilerParams(dimension_semantics=sem, vmem_limit_bytes=vmem)


def _sigmoid(x):
    return 1.0 / (1.0 + jnp.exp(-x))


def _ada_kernel(c_ref, w_ref, b_ref, o_ref):
    c = c_ref[...]
    sc = c * _sigmoid(c)
    o_ref[...] = jnp.dot(sc, w_ref[...], preferred_element_type=F32, precision=HIGHEST) + b_ref[...]


def ada_mod(c, w_ada, b_ada, tn=1024):
    bsz, d = c.shape
    n = w_ada.shape[1]
    return pl.pallas_call(
        _ada_kernel,
        grid=(n // tn,),
        in_specs=[pl.BlockSpec((bsz, d), lambda j: (0, 0)),
                  pl.BlockSpec((d, tn), lambda j: (0, j)),
                  pl.BlockSpec((1, tn), lambda j: (0, j))],
        out_specs=pl.BlockSpec((bsz, tn), lambda j: (0, j)),
        out_shape=jax.ShapeDtypeStruct((bsz, n), F32),
        compiler_params=_cparams(("arbitrary",)),
        name="ada_mod",
    )(c, w_ada, b_ada.reshape(1, n))


def _inproj_kernel(x_ref, mod_ref, w_ref, o_ref, hb_ref):
    @pl.when(pl.program_id(1) == 0)
    def _():
        sc = mod_ref[0, 0:1, :]
        sh = mod_ref[0, 1:2, :]
        hb_ref[...] = (x_ref[...] * (1.0 + sc) + sh).astype(BF16)

    o_ref[...] = jnp.dot(hb_ref[...], w_ref[...], preferred_element_type=F32).astype(o_ref.dtype)


def in_proj(x2, mod1, w_bf16, seq, tm=1024, tn=512):
    n, d = x2.shape
    wn = w_bf16.shape[1]
    tm = min(tm, seq)
    per_b = seq // tm
    return pl.pallas_call(
        _inproj_kernel,
        grid=(n // tm, wn // tn),
        in_specs=[pl.BlockSpec((tm, d), lambda i, j: (i, 0)),
                  pl.BlockSpec((1, 2, d), lambda i, j: (i // per_b, 0, 0)),
                  pl.BlockSpec((d, tn), lambda i, j: (0, j))],
        out_specs=pl.BlockSpec((tm, tn), lambda i, j: (i, j)),
        out_shape=jax.ShapeDtypeStruct((n, wn), BF16),
        scratch_shapes=[pltpu.VMEM((tm, d), BF16)],
        compiler_params=_cparams(("arbitrary", "arbitrary")),
        name="in_proj",
    )(x2, mod1, w_bf16)


def _inproj_small_kernel(x_ref, mod_ref, w_ref, o_ref):
    sc = mod_ref[0, 0:1, :]
    sh = mod_ref[0, 1:2, :]
    h = x_ref[...] * (1.0 + sc) + sh
    o_ref[...] = jnp.dot(h, w_ref[...], preferred_element_type=F32, precision=HIGHEST)


def in_proj_small(x2, mod1, w_f32, seq, tm=512):
    n, d = x2.shape
    wn = w_f32.shape[1]
    tm = min(tm, seq)
    per_b = seq // tm
    return pl.pallas_call(
        _inproj_small_kernel,
        grid=(n // tm,),
        in_specs=[pl.BlockSpec((tm, d), lambda i: (i, 0)),
                  pl.BlockSpec((1, 2, d), lambda i: (i // per_b, 0, 0)),
                  pl.BlockSpec((d, wn), lambda i: (0, 0))],
        out_specs=pl.BlockSpec((tm, wn), lambda i: (i, 0)),
        out_shape=jax.ShapeDtypeStruct((n, wn), F32),
        compiler_params=_cparams(("arbitrary",)),
        name="in_proj_small",
    )(x2, mod1, w_f32)


def _gelu_tanh(x):
    return 0.5 * x * (1.0 + jnp.tanh(0.7978845608028654 * (x + 0.044715 * (x * x * x))))


def _compress_kernel(g_ref, pe_ref, w1_ref, b1_ref, w2_ref, o_ref):
    half = CMP_STRIDE * HEAD_DIM
    g = g_ref[0, 0].astype(F32)
    ga = (g + pe_ref[0, 0:1, :]).astype(BF16)
    gb = (g + pe_ref[0, 1:2, :]).astype(BF16)
    u = jnp.dot(ga, w1_ref[0, 0:half, :], preferred_element_type=F32)
    v = jnp.dot(gb, w1_ref[0, half:2 * half, :], preferred_element_type=F32)
    ng = u.shape[0]
    v_next = pltpu.roll(v, ng - 1, 0)
    h = _gelu_tanh(u + v_next + b1_ref[0])
    o_ref[0, 0] = jnp.dot(h.astype(BF16), w2_ref[0], preferred_element_type=F32).astype(o_ref.dtype)


def compress(kvg, pe2, w1, b1, w2):
    bsz, four, ng, gd = kvg.shape
    return pl.pallas_call(
        _compress_kernel,
        grid=(bsz, four),
        in_specs=[pl.BlockSpec((1, 1, ng, gd), lambda b, j: (b, j, 0, 0)),
                  pl.BlockSpec((1, 2, gd), lambda b, j: (j // 2, 0, 0)),
                  pl.BlockSpec((1, 2 * gd, CMP_HIDDEN), lambda b, j: (j // 2, 0, 0)),
                  pl.BlockSpec((1, 1, CMP_HIDDEN), lambda b, j: (j // 2, 0, 0)),
                  pl.BlockSpec((1, CMP_HIDDEN, HEAD_DIM), lambda b, j: (j // 2, 0, 0))],
        out_specs=pl.BlockSpec((1, 1, ng, HEAD_DIM), lambda b, j: (b, j, 0, 0)),
        out_shape=jax.ShapeDtypeStruct((bsz, four, ng, HEAD_DIM), BF16),
        compiler_params=_cparams(("arbitrary", "arbitrary")),
        name="compress",
    )(kvg, pe2, w1, b1, w2)


def _head_slope(h):
    return 2.0 ** (-(h + 1.0))


def _cmp_attn_kernel(q_ref, kv_ref, gate_ref, ovl_ref, o_ref, imp_ref, *, tq):
    i = pl.program_id(1)
    ncp = kv_ref.shape[2]
    scale = HEAD_DIM ** -0.5
    qrel = lax.broadcasted_iota(I32, (tq, 1), 0)
    pos = i * tq + qrel
    cidx = lax.broadcasted_iota(I32, (1, ncp), 1)
    ok = (cidx * CMP_STRIDE + (CMP_BLOCK - 1)) <= pos
    any_ok = (pos >= CMP_BLOCK - 1).astype(F32)
    mid_rel = (cidx * CMP_STRIDE - i * tq).astype(F32) + (CMP_BLOCK - 1) / 2.0
    gates = _sigmoid(gate_ref[...])
    for g in range(NSA_KV_HEADS):
        kc = kv_ref[0, g]
        vc = kv_ref[0, NSA_KV_HEADS + g]
        psum = jnp.zeros((tq, ncp), F32)
        for r in range(NSA_GROUP):
            h = g * NSA_GROUP + r
            qh = q_ref[:, h * HEAD_DIM:(h + 1) * HEAD_DIM]
            s = lax.dot_general(qh, kc, (((1,), (1,)), ((), ())), preferred_element_type=F32) * scale
            s = s + _head_slope(h) * mid_rel
            s = jnp.where(ok, s, NEG_INF)
            m = jnp.max(s, axis=-1, keepdims=True)
            e = jnp.exp(s - m)
            p = e * (any_ok / jnp.sum(e, axis=-1, keepdims=True))
            psum = psum + p
            o = jnp.dot(p.astype(BF16), vc, preferred_element_type=F32)
            o_ref[:, h * HEAD_DIM:(h + 1) * HEAD_DIM] = o * gates[:, 3 * h:3 * h + 1]
        imp_ref[0, g] = jnp.dot(psum, ovl_ref[...], preferred_element_type=F32, precision=HIGHEST)


def cmp_attention(p_buf, kvc, s_buf, ovl, bsz, seq, tq=128):
    n = p_buf.shape[0]
    ncp = kvc.shape[2]
    nsel = ovl.shape[1]
    per_b = seq // tq
    return pl.pallas_call(
        functools.partial(_cmp_attn_kernel, tq=tq),
        grid=(bsz, per_b),
        in_specs=[pl.BlockSpec((tq, NSA_WIDTH), lambda b, i: (b * per_b + i, P_QNSA // NSA_WIDTH)),
                  pl.BlockSpec((1, 4, ncp, HEAD_DIM), lambda b, i: (b, 0, 0, 0)),
                  pl.BlockSpec((tq, LANES), lambda b, i: (b * per_b + i, 0)),
                  pl.BlockSpec((ncp, nsel), lambda b, i: (0, 0))],
        out_specs=[pl.BlockSpec((tq, NSA_WIDTH), lambda b, i: (b * per_b + i, 0)),
                   pl.BlockSpec((1, NSA_KV_HEADS, tq, nsel), lambda b, i: (b, 0, i, 0))],
        out_shape=[jax.ShapeDtypeStruct((n, NSA_WIDTH), F32),
                   jax.ShapeDtypeStruct((bsz, NSA_KV_HEADS, seq, nsel), F32)],
        compiler_params=_cparams(("arbitrary", "arbitrary")),
        name="cmp_attention",
    )(p_buf, kvc, s_buf, ovl)


def _topk_kernel(imp_ref, o_ref, any_ref, *, seq, n_topk, qtile):
    rows, nsel = imp_ref.shape
    row0 = pl.program_id(0) * rows
    pos = (row0 + lax.broadcasted_iota(I32, (rows, 1), 0)) % seq
    cur = pos // SEL_BLOCK
    blk = lax.broadcasted_iota(I32, (1, nsel), 1)
    blkf = blk.astype(F32)
    forced = jnp.logical_or(blk == 0, jnp.logical_and(blk <= cur, blk > cur - SEL_LOCAL))
    work = jnp.where(forced, FORCE_SCORE, jnp.where(blk > cur, -1.0, imp_ref[...]))
    sel = jnp.zeros((rows, nsel), F32)
    for _ in range(n_topk):
        m = jnp.max(work, axis=-1, keepdims=True)
        first = jnp.min(jnp.where(work == m, blkf, float(nsel)), axis=-1, keepdims=True)
        pick = blkf == first
        sel = jnp.where(pick, 1.0, sel)
        work = jnp.where(pick, -2.0, work)
    keep = jnp.logical_and(sel > 0.5, blk <= cur)
    res = jnp.where(keep, 0.0, NEG_INF)
    if nsel < LANES:
        res = jnp.concatenate([res, jnp.full((rows, LANES - nsel), NEG_INF, F32)], axis=1)
    o_ref[...] = res.astype(o_ref.dtype)
    hit = jnp.where(res == 0.0, 1.0, 0.0)
    for qt in range(rows // qtile):
        any_ref[qt:qt + 1, :] = jnp.max(hit[qt * qtile:(qt + 1) * qtile], axis=0, keepdims=True)


def topk_select(imp2, seq, qtile, rows=1024):
    nrow, nsel = imp2.shape
    rows = min(rows, nrow)
    return pl.pallas_call(
        functools.partial(_topk_kernel, seq=seq, n_topk=min(SEL_TOPK, nsel), qtile=qtile),
        grid=(nrow // rows,),
        in_specs=[pl.BlockSpec((rows, nsel), lambda i: (i, 0))],
        out_specs=[pl.BlockSpec((rows, LANES), lambda i: (i, 0)),
                   pl.BlockSpec((rows // qtile, LANES), lambda i: (i, 0))],
        out_shape=[jax.ShapeDtypeStruct((nrow, LANES), BF16),
                   jax.ShapeDtypeStruct((nrow // qtile, LANES), F32)],
        compiler_params=_cparams(("arbitrary",)),
        name="topk_select",
    )(imp2)


def _sel_win_kernel(ids_ref, cnt_ref, q_ref, ks_ref, vst_ref, kw_ref, vwt_ref, sb_ref, kau_ref, kad_ref, oc_ref,
                    gate_ref, o_ref, qau_ref, qad_ref, qaw_ref, acc_ref, s_ref, mt_ref, *, tq, tk, seq):
    g = pl.program_id(1)
    i = pl.program_id(2)
    q0 = i * tq
    scale = HEAD_DIM ** -0.5
    R = NSA_GROUP
    C = R * tq
    D = HEAD_DIM
    dn = (((1,), (1,)), ((), ()))
    gslope = jnp.where(g == 0, 1.0, 2.0 ** (-NSA_GROUP))
    slopes = [gslope * 2.0 ** (-(r + 1)) for r in range(R)]

    qrel_col = jnp.concatenate([lax.broadcasted_iota(I32, (1, tq), 1)] * R, axis=1)
    slope_col = jnp.concatenate([jnp.full((1, tq), 1.0, F32) * slopes[r] for r in range(R)], axis=1)

    q = q_ref[...]
    selb = sb_ref[...].astype(F32)
    lane = lax.broadcasted_iota(I32, (1, LANES), 1)
    nrel = ((lane - q0 // SEL_BLOCK) * SEL_BLOCK).astype(F32)
    sub = lax.broadcasted_iota(I32, (LANES, 1), 0)
    for r in range(R):
        cols = slice(r * tq, (r + 1) * tq)
        qs_t = (q[:, r * D:(r + 1) * D].astype(F32) * scale).T.astype(BF16)
        sb_t = (selb + slopes[r] * nrel).T
        qau_ref[0:D, cols] = qs_t
        qau_ref[D:2 * D, cols] = jnp.where(sub == LANES - 1, slopes[r], sb_t).astype(BF16)
        qad_ref[0:D, cols] = qs_t
        qad_ref[D:2 * D, cols] = sb_t.astype(BF16)
        qaw_ref[0:D, cols] = qs_t
        aw = jnp.where(sub == 0, slopes[r] * SEL_BLOCK, jnp.where(sub == 1, slopes[r], 0.0))
        qaw_ref[D:2 * D, cols] = jnp.broadcast_to(aw, (D, tq)).astype(BF16)

    acc_ref[...] = jnp.zeros_like(acc_ref)

    def online_update(st, t, m, l):
        m_new = jnp.maximum(m, jnp.max(st, axis=0, keepdims=True))
        alpha = jnp.exp(m - m_new)
        p = jnp.exp(st - m_new)
        l_new = alpha * l + jnp.sum(p, axis=0, keepdims=True)
        pv = jnp.dot(vst_ref[t], p.astype(BF16), preferred_element_type=F32)
        acc_ref[...] = acc_ref[...] * alpha + pv
        return m_new, l_new

    S_SPARE, S_NEGINF = 2, 3

    @pl.when(jnp.logical_and(pl.program_id(0) == 0, jnp.logical_and(g == 0, i == 0)))
    def _():
        s_ref[S_NEGINF] = jnp.full((tk, C), -jnp.inf, F32)

    t_last = q0 // tk
    step = (pl.program_id(0) * pl.num_programs(1) + g) * pl.num_programs(2) + i
    cnt = cnt_ref[step]
    id0 = step * (seq // tk)

    def pipelined(u, carry):
        m, l = carry
        tb = ids_ref[id0 + jnp.maximum(u - 1, 0)]
        st = s_ref[jnp.where(u >= 1, (u + 1) % 2, S_NEGINF)]
        m_new = jnp.maximum(m, mt_ref[0:1, :])
        alpha = jnp.exp(m - m_new)
        p = jnp.exp(st - m_new)
        l_new = alpha * l + jnp.sum(p, axis=0, keepdims=True)
        pv = jnp.dot(vst_ref[tb], p.astype(BF16), preferred_element_type=F32)
        acc_ref[...] = acc_ref[...] * alpha + pv
        ta = ids_ref[id0 + jnp.minimum(u, cnt - 1)]
        k0 = pl.multiple_of(ta * tk, tk)
        ka = jnp.concatenate([ks_ref[pl.ds(k0, tk), :], kau_ref[pl.ds(k0, tk), :]], axis=1)
        sa = jnp.dot(ka, qau_ref[...], preferred_element_type=F32)
        s_ref[jnp.where(u < cnt, u % 2, S_SPARE)] = sa
        mt_ref[...] = jnp.broadcast_to(jnp.max(sa, axis=0, keepdims=True), mt_ref.shape)
        return m_new, l_new

    m0 = jnp.full((1, C), NEG_INF, F32)
    l0 = jnp.zeros((1, C), F32)
    mt_ref[...] = jnp.full(mt_ref.shape, -jnp.inf, F32)
    m, l = lax.fori_loop(0, jnp.where(cnt > 0, cnt + 1, 0), pipelined, (m0, l0))

    k0 = pl.multiple_of(t_last * tk, tk)
    ka = jnp.concatenate([ks_ref[pl.ds(k0, tk), :], kad_ref[pl.ds(k0, tk), :]], axis=1)
    st = jnp.dot(ka, qad_ref[...], preferred_element_type=F32)
    krow = lax.broadcasted_iota(I32, (tk, 1), 0)
    koff = (krow % SEL_BLOCK).astype(F32)
    st = st + koff * slope_col
    st = jnp.where(krow + (k0 - q0) <= qrel_col, st, NEG_INF)
    m, l = online_update(st, t_last, m, l)
    inv_l = 1.0 / l

    wk = WINDOW + tq
    w0 = pl.multiple_of(jnp.maximum(q0 - WINDOW, 0), tq)
    wrow = lax.broadcasted_iota(I32, (wk, 1), 0) + (w0 - q0)
    w_hi = jnp.floor(wrow.astype(F32) * (1.0 / SEL_BLOCK))
    w_lo = wrow.astype(F32) - w_hi * SEL_BLOCK
    kaw = jnp.where(lane == 0, w_hi, jnp.where(lane == 1, w_lo, 0.0)).astype(BF16)
    kwa = jnp.concatenate([kw_ref[pl.ds(w0, wk), :], kaw], axis=1)
    sw = jnp.dot(kwa, qaw_ref[...], preferred_element_type=F32)
    dist = qrel_col - wrow
    sw = jnp.where(jnp.logical_and(dist >= 0, dist < WINDOW), sw, NEG_INF)
    mw = jnp.max(sw, axis=0, keepdims=True)
    pw = jnp.exp(sw - mw)
    inv_lw = 1.0 / jnp.sum(pw, axis=0, keepdims=True)
    c0 = w0 // tq
    vwt = jnp.concatenate([vwt_ref[c0 + c] for c in range(wk // tq)], axis=1)
    ow = jnp.dot(vwt, pw.astype(BF16), preferred_element_type=F32)

    gates_t = _sigmoid(gate_ref[...]).T
    osel = acc_ref[...] * inv_l
    ow = ow * inv_lw
    for r in range(R):
        cols = slice(r * tq, (r + 1) * tq)
        g_sel = jnp.where(g == 0, gates_t[3 * r + 1:3 * r + 2, :], gates_t[3 * (R + r) + 1:3 * (R + r) + 2, :])
        g_win = jnp.where(g == 0, gates_t[3 * r + 2:3 * r + 3, :], gates_t[3 * (R + r) + 2:3 * (R + r) + 3, :])
        y_t = g_sel * osel[:, cols] + g_win * ow[:, cols]
        y = oc_ref[:, r * D:(r + 1) * D] + y_t.T
        o_ref[:, r * D:(r + 1) * D] = y.astype(o_ref.dtype)


def _key_side_constants(seq):
    nsel = seq // SEL_BLOCK
    key = jnp.arange(seq)
    blk = key // SEL_BLOCK
    lane = jnp.arange(LANES)[None, :]
    onehot = (lane == blk[:, None])
    kad = onehot.astype(BF16)
    kau = jnp.where(lane == LANES - 1, (key % SEL_BLOCK)[:, None].astype(F32),
                    jnp.logical_and(onehot, lane < nsel - SEL_LOCAL).astype(F32)).astype(BF16)
    return kau, kad


def _active_key_tiles(blk_any, seq, tq, tk):
    nt = seq // tk
    per_b = seq // tq
    bpt = tk // SEL_BLOCK
    tile_any = blk_any[:, :nt * bpt].reshape(-1, nt, bpt).max(axis=-1) > 0.5
    t_last = (jnp.arange(blk_any.shape[0], dtype=I32) % per_b) * tq // tk
    tile_idx = jnp.arange(nt, dtype=I32)[None, :]
    active = jnp.logical_and(tile_any, tile_idx < t_last[:, None])
    ids = jnp.argsort(jnp.where(active, tile_idx, nt + tile_idx), axis=-1).astype(I32)
    return ids.reshape(-1), active.sum(axis=-1).astype(I32)


def sel_win_attention(p_buf, selb, blk_any, ocmp, s_buf, bsz, seq, tq=128, tk=512):
    n = p_buf.shape[0]
    tk = min(tk, seq)
    per_b = seq // tq
    gw = NSA_GROUP * HEAD_DIM
    G = NSA_KV_HEADS
    D = HEAD_DIM
    kau, kad = _key_side_constants(seq)
    if blk_any.shape[1] < (seq // tk) * (tk // SEL_BLOCK):
        blk_any = jnp.pad(blk_any, ((0, 0), (0, (seq // tk) * (tk // SEL_BLOCK) - blk_any.shape[1])))
    tile_ids, tile_cnt = _active_key_tiles(blk_any, seq, tq, tk)
    vsel = p_buf[:, P_VSEL:P_VSEL + KV_WIDTH].reshape(bsz, seq // tk, tk, G, D).transpose(0, 3, 1, 4, 2)
    vwin = p_buf[:, P_VWIN:P_VWIN + KV_WIDTH].reshape(bsz, seq // tq, tq, G, D).transpose(0, 3, 1, 4, 2)

    def colblk(off):
        return lambda b, g, i, ids, cnt: (b, off // D + g)

    return pl.pallas_call(
        functools.partial(_sel_win_kernel, tq=tq, tk=tk, seq=seq),
        grid_spec=pltpu.PrefetchScalarGridSpec(
            num_scalar_prefetch=2,
            grid=(bsz, G, per_b),
            in_specs=[pl.BlockSpec((tq, gw), lambda b, g, i, ids, cnt: (b * per_b + i, P_QNSA // gw + g)),
                      pl.BlockSpec((seq, D), colblk(P_KSEL)),
                      pl.BlockSpec((None, None, seq // tk, D, tk), lambda b, g, i, ids, cnt: (b, g, 0, 0, 0)),
                      pl.BlockSpec((seq, D), colblk(P_KWIN)),
                      pl.BlockSpec((None, None, seq // tq, D, tq), lambda b, g, i, ids, cnt: (b, g, 0, 0, 0)),
                      pl.BlockSpec((tq, LANES), lambda b, g, i, ids, cnt: ((b * G + g) * per_b + i, 0)),
                      pl.BlockSpec((seq, LANES), lambda b, g, i, ids, cnt: (0, 0)),
                      pl.BlockSpec((seq, LANES), lambda b, g, i, ids, cnt: (0, 0)),
                      pl.BlockSpec((tq, gw), lambda b, g, i, ids, cnt: (b * per_b + i, g)),
                      pl.BlockSpec((tq, LANES), lambda b, g, i, ids, cnt: (b * per_b + i, 0))],
            out_specs=pl.BlockSpec((tq, gw), lambda b, g, i, ids, cnt: (b * per_b + i, g)),
            scratch_shapes=[pltpu.VMEM((2 * D, NSA_GROUP * tq), BF16),
                            pltpu.VMEM((2 * D, NSA_GROUP * tq), BF16),
                            pltpu.VMEM((2 * D, NSA_GROUP * tq), BF16),
                            pltpu.VMEM((D, NSA_GROUP * tq), F32),
                            pltpu.VMEM((4, tk, NSA_GROUP * tq), F32),
                            pltpu.VMEM((8, NSA_GROUP * tq), F32)]),
        out_shape=jax.ShapeDtypeStruct((n, NSA_WIDTH), BF16),
        compiler_params=_cparams(("arbitrary", "arbitrary", "arbitrary")),
        name="sel_win_attention",
    )(tile_ids, tile_cnt, p_buf, p_buf, vsel, p_buf, vwin, selb, kau, kad, ocmp, s_buf)


def _gla_kernel(q_ref, k_ref, v_ref, r_ref, a_ref, wg_ref, bg_ref, nw_ref, tril_ref, o_ref,
                st_ref, b_ref, *, tc):
    C, SUB, H = GLA_CHUNK, GLA_SUB, GLA_HEADS

    @pl.when(pl.program_id(1) == 0)
    def _():
        st_ref[...] = jnp.zeros_like(st_ref)

    x = jnp.dot(a_ref[...], wg_ref[...], preferred_element_type=F32, precision=HIGHEST) + bg_ref[...]
    la = (jnp.minimum(x, 0.0) - jnp.log(1.0 + jnp.exp(-jnp.abs(x)))) * (1.0 / GLA_GATE_TAU)
    tril = tril_ref[...]
    for c in range(tc // C):
        b_ref[c * C:(c + 1) * C, :] = jnp.dot(tril, la[c * C:(c + 1) * C, :],
                                              preferred_element_type=F32, precision=HIGHEST)

    row16 = lax.broadcasted_iota(I32, (SUB, C), 0)
    lane64 = lax.broadcasted_iota(I32, (SUB, C), 1)
    nw = nw_ref[...]

    def chunk(c, carry):
        r0 = pl.multiple_of(c * C, C)
        for h in range(H):
            bh = b_ref[pl.ds(r0, C), h * GLA_DK:(h + 1) * GLA_DK]
            qc = q_ref[pl.ds(r0, C), h * GLA_DK:(h + 1) * GLA_DK].astype(F32) * (GLA_DK ** -0.5)
            kc = k_ref[pl.ds(r0, C), h * GLA_DK:(h + 1) * GLA_DK].astype(F32)
            vc = v_ref[pl.ds(r0, C), h * GLA_DV:(h + 1) * GLA_DV]
            st = st_ref[h]
            b_last = bh[C - 1:C, :]
            qe = (qc * jnp.exp(bh)).astype(BF16)
            o = lax.dot_general(qe, st.astype(BF16), (((1,), (1,)), ((), ())),
                                preferred_element_type=F32)
            a_rows = []
            for sb in range(C // SUB):
                s0 = sb * SUB
                bi = bh[s0:s0 + SUB, :]
                qi = qc[s0:s0 + SUB, :]
                ki = kc[s0:s0 + SUB, :]
                beta = bh[s0:s0 + 1, :]
                if sb > 0:
                    qd = (qi * jnp.exp(bi - beta)).astype(BF16)
                    kd = (kc * jnp.exp(jnp.minimum(beta - bh, 0.0))).astype(BF16)
                    a_i = lax.dot_general(qd, kd, (((1,), (1,)), ((), ())), preferred_element_type=F32)
                    a_i = jnp.where(lane64 < s0, a_i, 0.0)
                else:
                    a_i = jnp.zeros((SUB, C), F32)
                for s in range(SUB):
                    e = jnp.exp(jnp.minimum(bi - bi[s:s + 1, :], 0.0))
                    col = jnp.sum(qi * ki[s:s + 1, :] * e, axis=-1, keepdims=True)
                    a_i = jnp.where(jnp.logical_and(lane64 == s0 + s, row16 >= s), col, a_i)
                a_rows.append(a_i)
            a_mat = jnp.concatenate(a_rows, axis=0)
            o = o + jnp.dot(a_mat.astype(BF16), vc, preferred_element_type=F32)
            kdec = (kc * jnp.exp(b_last - bh)).astype(BF16)
            upd = lax.dot_general(vc, kdec, (((0,), (0,)), ((), ())), preferred_element_type=F32)
            st_ref[h] = st * jnp.exp(b_last) + upd
            rms = lax.rsqrt(jnp.mean(o * o, axis=-1, keepdims=True) + RMS_EPS)
            rr = r_ref[pl.ds(r0, C), h * GLA_DV:(h + 1) * GLA_DV].astype(F32)
            y = o * rms * nw * (rr * _sigmoid(rr))
            o_ref[pl.ds(r0, C), h * GLA_DV:(h + 1) * GLA_DV] = y.astype(o_ref.dtype)
        return carry

    lax.fori_loop(0, tc // C, chunk, 0)


def gla_attention(p_buf, s_buf, wg_pad, bg, nw, tril, bsz, seq, tc=512):
    n = p_buf.shape[0]
    tc = min(tc, seq)
    per_b = seq // tc
    kw = GLA_HEADS * GLA_DK
    return pl.pallas_call(
        functools.partial(_gla_kernel, tc=tc),
        grid=(bsz, per_b),
        in_specs=[pl.BlockSpec((tc, kw), lambda b, j: (b * per_b + j, P_QGLA // kw)),
                  pl.BlockSpec((tc, kw), lambda b, j: (b * per_b + j, P_KGLA // kw)),
                  pl.BlockSpec((tc, GLA_WIDTH), lambda b, j: (b * per_b + j, P_VGLA // GLA_WIDTH)),
                  pl.BlockSpec((tc, GLA_WIDTH), lambda b, j: (b * per_b + j, P_RGLA // GLA_WIDTH)),
                  pl.BlockSpec((tc, LANES), lambda b, j: (b * per_b + j, 1)),
                  pl.BlockSpec((LANES, kw), lambda b, j: (0, 0)),
                  pl.BlockSpec((1, kw), lambda b, j: (0, 0)),
                  pl.BlockSpec((1, GLA_DV), lambda b, j: (0, 0)),
                  pl.BlockSpec((GLA_CHUNK, GLA_CHUNK), lambda b, j: (0, 0))],
        out_specs=pl.BlockSpec((tc, GLA_WIDTH), lambda b, j: (b * per_b + j, 0)),
        out_shape=jax.ShapeDtypeStruct((n, GLA_WIDTH), BF16),
        scratch_shapes=[pltpu.VMEM((GLA_HEADS, GLA_DV, GLA_DK), F32),
                        pltpu.VMEM((tc, kw), F32)],
        compiler_params=_cparams(("arbitrary", "arbitrary")),
        name="gla_attention",
    )(p_buf, p_buf, p_buf, p_buf, s_buf, wg_pad, bg, nw, tril)


def _pack_words(lo, hi):
    lo_b = pltpu.bitcast(lo.astype(BF16).astype(F32), U32)
    hi_b = pltpu.bitcast(hi.astype(BF16).astype(F32), U32)
    return jnp.bitwise_or(hi_b, jnp.right_shift(lo_b, jnp.uint32(16)))


def _unpack_words(w):
    lo = pltpu.bitcast(jnp.left_shift(w, jnp.uint32(16)), F32)
    hi = pltpu.bitcast(jnp.bitwise_and(w, jnp.uint32(0xFFFF0000)), F32)
    return lo, hi


def _store_packed(ref, v, first_tile=0):
    rows = v.shape[0]
    n_tiles = ref.shape[0] // rows
    for u in range(v.shape[1] // (2 * LANES)):
        w = _pack_words(v[:, 2 * u * LANES:(2 * u + 1) * LANES], v[:, (2 * u + 1) * LANES:(2 * u + 2) * LANES])
        ref[pl.ds(first_tile + u, rows, stride=n_tiles), :] = w


def _load_word_tile(ref, s, rows, base=0):
    n_tiles = WORD_TILES
    return ref[pl.ds(base * n_tiles + s, rows, stride=n_tiles), :]


def _layer_norm(z, g, b):
    mu = jnp.mean(z, axis=-1, keepdims=True)
    zc = z - mu
    var = jnp.mean(zc * zc, axis=-1, keepdims=True)
    return zc * lax.rsqrt(var + LN_EPS) * g + b


def _outproj_kernel(yn_ref, yg_ref, wo_ref, x_ref, mod_ref, lng_ref, lnb_ref, wr_ref, br_ref, ltri_ref,
                    x1_ref, hw_ref, idx_ref, tw_ref, rank_ref, cnt_ref, carry_ref):
    step = pl.program_id(0)

    @pl.when(step == 0)
    def _():
        carry_ref[...] = jnp.zeros_like(carry_ref)

    half = yn_ref.shape[1]
    mix = (jnp.dot(yn_ref[...], wo_ref[0:half, :], preferred_element_type=F32) +
           jnp.dot(yg_ref[...], wo_ref[half:2 * half, :], preferred_element_type=F32))
    g1 = mod_ref[0, 0:1, :]
    sc2 = mod_ref[0, 1:2, :]
    sh2 = mod_ref[0, 2:3, :]
    x1 = _layer_norm(DN_ALPHA * x_ref[...] + g1 * mix, lng_ref[...], lnb_ref[...])
    x1_ref[...] = x1
    h2 = x1 * (1.0 + sc2) + sh2
    _store_packed(hw_ref, h2)

    tm = h2.shape[0]
    logits = jnp.dot(h2, wr_ref[...], preferred_element_type=F32, precision=HIGHEST) + br_ref[...]
    ne = logits.shape[1]
    lane = lax.broadcasted_iota(I32, (1, ne), 1).astype(F32)
    out_lane = lax.broadcasted_iota(I32, (1, LANES), 1)
    work = logits
    idx_out = jnp.zeros((tm, LANES), F32)
    w_out = jnp.zeros((tm, LANES), F32)
    onehot = jnp.zeros((tm, ne), F32)
    picks = []
    top0 = None
    den = jnp.zeros((tm, 1), F32)
    for k in range(TOP_K):
        m = jnp.max(work, axis=-1, keepdims=True)
        first = jnp.min(jnp.where(work == m, lane, float(ne)), axis=-1, keepdims=True)
        pick = lane == first
        if k == 0:
            top0 = m
        e = jnp.exp(m - top0)
        den = den + e
        idx_out = jnp.where(out_lane == k, first, idx_out)
        w_out = jnp.where(out_lane == k, e, w_out)
        onehot = jnp.where(pick, 1.0, onehot)
        picks.append(pick)
        work = jnp.where(pick, -jnp.inf, work)
    idx_ref[...] = idx_out.astype(I32)
    tw_ref[...] = w_out * (1.0 / den)

    before = jnp.dot(ltri_ref[...], onehot.astype(BF16), preferred_element_type=F32) + carry_ref[0:1, :]
    rank_out = jnp.zeros((tm, LANES), F32)
    for k in range(TOP_K):
        rk = jnp.sum(jnp.where(picks[k], before, 0.0), axis=-1, keepdims=True)
        rank_out = jnp.where(out_lane == k, rk, rank_out)
    rank_ref[...] = rank_out.astype(I32)
    carry_ref[...] = carry_ref[...] + jnp.sum(onehot, axis=0, keepdims=True)
    cnt_ref[...] = carry_ref[...].astype(I32)


def out_proj_router(ynsa, ygla, wo_bf16, x2, mod2, ln_g, ln_b, w_router, b_router, ltri, seq, tm=512):
    n, d = x2.shape
    tm = min(tm, seq)
    per_b = seq // tm
    ne = w_router.shape[1]
    row = lambda i: (i, 0)
    const = lambda i: (0, 0)
    return pl.pallas_call(
        _outproj_kernel,
        grid=(n // tm,),
        in_specs=[pl.BlockSpec((tm, NSA_WIDTH), row),
                  pl.BlockSpec((tm, GLA_WIDTH), row),
                  pl.BlockSpec((d, d), const),
                  pl.BlockSpec((tm, d), row),
                  pl.BlockSpec((1, 3, d), lambda i: (i // per_b, 0, 0)),
                  pl.BlockSpec((1, d), const),
                  pl.BlockSpec((1, d), const),
                  pl.BlockSpec((d, ne), const),
                  pl.BlockSpec((1, ne), const),
                  pl.BlockSpec((tm, tm), const)],
        out_specs=[pl.BlockSpec((tm, d), row),
                   pl.BlockSpec((tm * WORD_TILES, LANES), row),
                   pl.BlockSpec((tm, LANES), row),
                   pl.BlockSpec((tm, LANES), row),
                   pl.BlockSpec((tm, LANES), row),
                   pl.BlockSpec((8, ne), const)],
        out_shape=[jax.ShapeDtypeStruct((n, d), F32),
                   jax.ShapeDtypeStruct((n * WORD_TILES, LANES), U32),
                   jax.ShapeDtypeStruct((n, LANES), I32),
                   jax.ShapeDtypeStruct((n, LANES), F32),
                   jax.ShapeDtypeStruct((n, LANES), I32),
                   jax.ShapeDtypeStruct((8, ne), I32)],
        scratch_shapes=[pltpu.VMEM((8, ne), F32)],
        compiler_params=_cparams(("arbitrary",)),
        name="out_proj_router",
    )(ynsa, ygla, wo_bf16, x2, mod2, ln_g, ln_b, w_router, b_router, ltri)


def _issue_token_copies(idx_at, src_ref, dst_ref, sem, count):
    wt = WORD_TILES

    def issue(r8, carry):
        for u in range(GATHER_UNROLL):
            r = r8 * GATHER_UNROLL + u
            tok = idx_at(r)
            pltpu.make_async_copy(src_ref.at[pl.ds(pl.multiple_of(tok * wt, wt), wt)],
                                  dst_ref.at[pl.ds(pl.multiple_of(r * wt, wt), wt)], sem).start(priority=u % 2)
        return carry

    lax.fori_loop(0, count // GATHER_UNROLL, issue, 0)


def _gather_rows_kernel(nused_ref, idx_ref, src_ref, o_ref, sem, *, rows):
    blk = pl.program_id(0)

    @pl.when(blk < nused_ref[0])
    def _():
        _issue_token_copies(lambda r: idx_ref[0, 0, r], src_ref, o_ref, sem, rows)
        pltpu.make_async_copy(src_ref.at[pl.ds(0, rows * WORD_TILES)], o_ref, sem).wait()

    @pl.when(blk >= nused_ref[0])
    def _():
        o_ref[...] = jnp.zeros_like(o_ref)


def gather_rows(nused, idx3, src, rows):
    nblk = idx3.shape[0]
    wt = WORD_TILES
    return pl.pallas_call(
        functools.partial(_gather_rows_kernel, rows=rows),
        grid_spec=pltpu.PrefetchScalarGridSpec(
            num_scalar_prefetch=1,
            grid=(nblk,),
            in_specs=[pl.BlockSpec((1, 1, rows), lambda i, nu: (i, 0, 0), memory_space=pltpu.SMEM),
                      pl.BlockSpec(memory_space=pl.ANY)],
            out_specs=pl.BlockSpec((rows * wt, LANES), lambda i, nu: (i, 0)),
            scratch_shapes=[pltpu.SemaphoreType.DMA(())]),
        out_shape=jax.ShapeDtypeStruct((nblk * rows * wt, LANES), src.dtype),
        compiler_params=_cparams(("arbitrary",)),
        name="dispatch_gather",
    )(nused, idx3, src)


def _moe_kernel(be_ref, nused_ref, x_ref, wg_ref, wu_ref, bg_ref, bu_ref, wd_ref, bd_ref, o_ref,
                xb_ref, acc_ref, *, tm):
    i = pl.program_id(0)
    j = pl.program_id(1)
    nj = pl.num_programs(1)
    used = i < nused_ref[0]

    @pl.when(jnp.logical_and(used, j == 0))
    def _():
        for s in range(WORD_TILES):
            lo, hi = _unpack_words(_load_word_tile(x_ref, s, tm))
            xb_ref[:, 2 * s * LANES:(2 * s + 1) * LANES] = lo.astype(BF16)
            xb_ref[:, (2 * s + 1) * LANES:(2 * s + 2) * LANES] = hi.astype(BF16)
        acc_ref[...] = jnp.zeros_like(acc_ref)

    @pl.when(used)
    def _():
        xb = xb_ref[...]
        gt = jnp.dot(xb, wg_ref[0].astype(BF16), preferred_element_type=F32) + bg_ref[0]
        up = jnp.dot(xb, wu_ref[0].astype(BF16), preferred_element_type=F32) + bu_ref[0]
        gt = jnp.minimum(gt, SWIGLU_LIMIT)
        up = jnp.clip(up, -SWIGLU_LIMIT, SWIGLU_LIMIT)
        act = (up + 1.0) * gt * _sigmoid(SWIGLU_ALPHA * gt)
        acc_ref[...] += jnp.dot(act.astype(BF16), wd_ref[0].astype(BF16), preferred_element_type=F32)

    @pl.when(jnp.logical_and(used, j == nj - 1))
    def _():
        _store_packed(o_ref, acc_ref[...] + bd_ref[0])

    @pl.when(jnp.logical_and(jnp.logical_not(used), j == nj - 1))
    def _():
        o_ref[...] = jnp.zeros_like(o_ref)


def moe_experts(block_expert, nused, xbuf, w_gate_up, b_gate_up, w_down, b_down, tm, tf=256):
    wt = WORD_TILES
    nslots = xbuf.shape[0] // wt
    ne, d, f2 = w_gate_up.shape
    f = f2 // 2
    tf = min(tf, f)
    nfj = f // tf
    nblk = nslots // tm
    b_gu3 = b_gate_up.reshape(ne, 1, f2)
    b_d3 = b_down.reshape(ne, 1, d)

    def jj(i, j, nu):
        return jnp.where(i < nu[0], j, nfj - 1)

    return pl.pallas_call(
        functools.partial(_moe_kernel, tm=tm),
        grid_spec=pltpu.PrefetchScalarGridSpec(
            num_scalar_prefetch=2,
            grid=(nblk, nfj),
            in_specs=[pl.BlockSpec((tm * wt, LANES), lambda i, j, be, nu: (i, 0)),
                      pl.BlockSpec((1, d, tf), lambda i, j, be, nu: (be[i], 0, jj(i, j, nu))),
                      pl.BlockSpec((1, d, tf), lambda i, j, be, nu: (be[i], 0, nfj + jj(i, j, nu))),
                      pl.BlockSpec((1, 1, tf), lambda i, j, be, nu: (be[i], 0, jj(i, j, nu))),
                      pl.BlockSpec((1, 1, tf), lambda i, j, be, nu: (be[i], 0, nfj + jj(i, j, nu))),
                      pl.BlockSpec((1, tf, d), lambda i, j, be, nu: (be[i], jj(i, j, nu), 0)),
                      pl.BlockSpec((1, 1, d), lambda i, j, be, nu: (be[i], 0, 0))],
            out_specs=pl.BlockSpec((tm * wt, LANES), lambda i, j, be, nu: (i, 0)),
            scratch_shapes=[pltpu.VMEM((tm, d), BF16), pltpu.VMEM((tm, d), F32)]),
        out_shape=jax.ShapeDtypeStruct((nslots * wt, LANES), U32),
        compiler_params=_cparams(("arbitrary", "arbitrary")),
        name="moe_experts",
    )(block_expert, nused, xbuf, w_gate_up, w_gate_up, b_gu3, b_gu3, w_down, b_d3)


def _combine_kernel(slot_ref, y_ref, tw_ref, x1_ref, mod_ref, lng_ref, lnb_ref, o_ref, yb_ref, sem, *, tc):
    _issue_token_copies(lambda a: slot_ref[0, 0, a], y_ref, yb_ref, sem, TOP_K * tc)
    pltpu.make_async_copy(y_ref.at[pl.ds(0, TOP_K * tc * WORD_TILES)], yb_ref, sem).wait()

    tw = tw_ref[...]
    wks = [tw[:, k:k + 1] for k in range(TOP_K)]
    cols = []
    for s in range(WORD_TILES):
        f_lo = jnp.zeros((tc, LANES), F32)
        f_hi = jnp.zeros((tc, LANES), F32)
        for k in range(TOP_K):
            lo, hi = _unpack_words(_load_word_tile(yb_ref, s, tc, base=k * tc))
            f_lo = f_lo + wks[k] * lo
            f_hi = f_hi + wks[k] * hi
        cols += [f_lo, f_hi]
    ffn = jnp.concatenate(cols, axis=-1)
    g2 = mod_ref[0, 0:1, :]
    o_ref[...] = _layer_norm(DN_ALPHA * x1_ref[...] + g2 * ffn, lng_ref[...], lnb_ref[...])


def combine(slot3, ybuf, top_w, x1, modg2, ln_g, ln_b, seq, tc=256):
    n, d = x1.shape
    tc = min(tc, seq)
    per_b = seq // tc
    return pl.pallas_call(
        functools.partial(_combine_kernel, tc=tc),
        grid=(n // tc,),
        in_specs=[pl.BlockSpec((1, 1, TOP_K * tc), lambda i: (i, 0, 0), memory_space=pltpu.SMEM),
                  pl.BlockSpec(memory_space=pl.ANY),
                  pl.BlockSpec((tc, LANES), lambda i: (i, 0)),
                  pl.BlockSpec((tc, d), lambda i: (i, 0)),
                  pl.BlockSpec((1, 1, d), lambda i: (i // per_b, 0, 0)),
                  pl.BlockSpec((1, d), lambda i: (0, 0)),
                  pl.BlockSpec((1, d), lambda i: (0, 0))],
        out_specs=pl.BlockSpec((tc, d), lambda i: (i, 0)),
        out_shape=jax.ShapeDtypeStruct((n, d), F32),
        scratch_shapes=[pltpu.VMEM((TOP_K * tc * WORD_TILES, LANES), U32), pltpu.SemaphoreType.DMA(())],
        compiler_params=_cparams(("arbitrary",)),
        name="combine",
    )(slot3, ybuf, top_w, x1, modg2, ln_g, ln_b)


def _split_w_in(w_in):
    offs = [0]
    for w in PROJ_WIDTHS:
        offs.append(offs[-1] + w)
    seg = [w_in[:, offs[k]:offs[k + 1]] for k in range(len(PROJ_WIDTHS))]
    (q_nsa, k_cmp, v_cmp, k_sel, v_sel, k_win, v_win, g_nsa, q_gla, k_gla, v_gla, a_gla, r_gla) = seg
    big = jnp.concatenate([q_nsa, v_gla, r_gla, k_cmp, v_cmp, k_sel, v_sel, k_win, v_win, q_gla, k_gla],
                          axis=1).astype(BF16)
    pad = lambda w: jnp.pad(w, ((0, 0), (0, LANES - w.shape[1])))
    small = jnp.concatenate([pad(g_nsa), pad(a_gla)], axis=1)
    return big, small


def _overlap_matrix(ncp, nsel):
    cs = jnp.arange(ncp)[:, None] * CMP_STRIDE
    ss = jnp.arange(nsel)[None, :] * SEL_BLOCK
    return jnp.logical_and(cs < ss + SEL_BLOCK, cs + CMP_BLOCK > ss).astype(F32)


MOE_TM = 1024


def kernel(x, c, w_ada, b_ada, w_in, cmp_pe_k, cmp_w1_k, cmp_b1_k, cmp_w2_k, cmp_pe_v, cmp_w1_v, cmp_b1_v,
           cmp_w2_v, gla_w_gate, gla_b_gate, gla_norm_w, w_out, ln1_g, ln1_b, w_router, b_router, w_gate_up,
           b_gate_up, w_down, b_down, ln2_g, ln2_b):
    return _forward(x, c, w_ada, b_ada, w_in, cmp_pe_k, cmp_w1_k, cmp_b1_k, cmp_w2_k, cmp_pe_v, cmp_w1_v,
                    cmp_b1_v, cmp_w2_v, gla_w_gate, gla_b_gate, gla_norm_w, w_out, ln1_g, ln1_b, w_router,
                    b_router, w_gate_up, b_gate_up, w_down, b_down, ln2_g, ln2_b, moe_tm=MOE_TM)


def _forward(x, c, w_ada, b_ada, w_in, cmp_pe_k, cmp_w1_k, cmp_b1_k, cmp_w2_k, cmp_pe_v, cmp_w1_v, cmp_b1_v,
             cmp_w2_v, gla_w_gate, gla_b_gate, gla_norm_w, w_out, ln1_g, ln1_b, w_router, b_router, w_gate_up,
             b_gate_up, w_down, b_down, ln2_g, ln2_b, *, moe_tm):
    bsz, seq, d = x.shape
    n = bsz * seq
    l = 0
    x2 = x.reshape(n, d)

    mod = ada_mod(c, w_ada[l], b_ada[l])
    sh1, sc1, g1, sh2, sc2, g2 = jnp.split(mod, 6, axis=-1)
    mod1 = jnp.stack([sc1, sh1], axis=1)
    mod2 = jnp.stack([g1, sc2, sh2], axis=1)
    modg2 = g2[:, None, :]

    w_big, w_small = _split_w_in(w_in[l])
    p_buf = in_proj(x2, mod1, w_big, seq)
    s_buf = in_proj_small(x2, mod1, w_small, seq)

    ng = seq // CMP_STRIDE
    kvg = p_buf[:, P_KVCMP:P_KVCMP + 2 * KV_WIDTH].reshape(bsz, ng, CMP_STRIDE, 4, HEAD_DIM)
    kvg = kvg.transpose(0, 3, 1, 2, 4).reshape(bsz, 4, ng, CMP_STRIDE * HEAD_DIM)
    half = CMP_STRIDE * HEAD_DIM
    pe2 = jnp.stack([cmp_pe_k[l].reshape(2, half), cmp_pe_v[l].reshape(2, half)])
    w1 = jnp.stack([cmp_w1_k[l], cmp_w1_v[l]]).astype(BF16)
    b1 = jnp.stack([cmp_b1_k[l], cmp_b1_v[l]])[:, None, :]
    w2 = jnp.stack([cmp_w2_k[l], cmp_w2_v[l]]).astype(BF16)
    kvc = compress(kvg, pe2, w1, b1, w2)

    nsel = seq // SEL_BLOCK
    ovl = _overlap_matrix(ng, nsel)
    ocmp, imp = cmp_attention(p_buf, kvc, s_buf, ovl, bsz, seq)
    selb, blk_any = topk_select(imp.reshape(bsz * NSA_KV_HEADS * seq, nsel), seq, qtile=NSA_TQ)
    y_nsa = sel_win_attention(p_buf, selb, blk_any, ocmp, s_buf, bsz, seq, tq=NSA_TQ)

    wg_pad = jnp.pad(gla_w_gate[l], ((0, LANES - GLA_GATE_RANK), (0, 0)))
    tril = jnp.tril(jnp.ones((GLA_CHUNK, GLA_CHUNK), F32))
    y_gla = gla_attention(p_buf, s_buf, wg_pad, gla_b_gate[l][None, :], gla_norm_w[l][None, :], tril, bsz, seq)

    tm_r = min(512, seq)
    ltri = jnp.tril(jnp.ones((tm_r, tm_r), F32), k=-1).astype(BF16)
    x1, hw, top_idx, top_w, rank, counts = out_proj_router(
        y_nsa, y_gla, w_out[l].astype(BF16), x2, mod2, ln1_g[l][None, :], ln1_b[l][None, :],
        w_router[l], b_router[l][None, :], ltri, seq, tm=tm_r)

    tm = moe_tm
    n_asg = n * TOP_K
    cnt = counts[0]
    padded = (cnt + tm - 1) // tm * tm
    ends_p = jnp.cumsum(padded)
    start_p = ends_p - padded
    idx4 = top_idx[:, :TOP_K]
    slot_of = (start_p[idx4] + rank[:, :TOP_K]).astype(I32)
    n_blocks = -(-n_asg // tm) + N_EXPERTS
    n_slots = n_blocks * tm
    tok_of_asg = jnp.arange(n_asg, dtype=I32) // TOP_K
    slot_token = jnp.zeros((n_slots,), I32).at[slot_of.reshape(-1)].set(tok_of_asg, unique_indices=True)
    nused = (ends_p[-1] // tm).astype(I32).reshape(1)
    blk_start = jnp.minimum(jnp.arange(n_blocks, dtype=I32), nused - 1) * tm
    block_expert = jnp.minimum(jnp.sum(ends_p[None, :] <= blk_start[:, None], axis=1), N_EXPERTS - 1).astype(I32)

    xbuf = gather_rows(nused, slot_token.reshape(n_blocks, 1, tm), hw, tm)
    ybuf = moe_experts(block_expert, nused, xbuf, w_gate_up[l], b_gate_up[l], w_down[l], b_down[l], tm)

    tc = min(256, seq)
    slot3 = slot_of.reshape(n // tc, tc, TOP_K).transpose(0, 2, 1).reshape(n // tc, 1, TOP_K * tc)
    out = combine(slot3, ybuf, top_w, x1, modg2, ln2_g[l][None, :], ln2_b[l][None, :], seq, tc=tc)
    return out.reshape(bsz, seq, d)
```

```python
import functools

import jax
import jax.numpy as jnp
from jax import lax
from jax.experimental import pallas as pl
from jax.experimental.pallas import tpu as pltpu

F32 = jnp.float32
BF16 = jnp.bfloat16
U32 = jnp.uint32
I32 = jnp.int32
HIGHEST = lax.Precision.HIGHEST

D_MODEL = 2048
HEAD_DIM = 128
NSA_HEADS = 8
NSA_KV_HEADS = 2
NSA_GROUP = 4
NSA_WIDTH = 1024
KV_WIDTH = 256
CMP_BLOCK = 32
CMP_STRIDE = 16
CMP_HIDDEN = 256
SEL_BLOCK = 64
SEL_TOPK = 16
SEL_LOCAL = 2
WINDOW = 512
GLA_HEADS = 4
GLA_DV = 256
GLA_DK = 128
GLA_WIDTH = 1024
GLA_GATE_RANK = 16
GLA_GATE_TAU = 16.0
GLA_CHUNK = 64
GLA_SUB = 16
N_EXPERTS = 32
TOP_K = 4
D_FF = 2048
SWIGLU_LIMIT = 7.0
SWIGLU_ALPHA = 1.702
LN_EPS = 1e-5
RMS_EPS = 1e-6
NEG_INF = -1e30
FORCE_SCORE = 1e6
DN_ALPHA = 2.0 ** 0.25
PROJ_WIDTHS = (1024, 256, 256, 256, 256, 256, 256, 24, 512, 512, 1024, 16, 1024)

LANES = 128
VMEM_LIMIT = 56 * 1024 * 1024
WORD_TILES = D_MODEL // (2 * LANES)
GATHER_UNROLL = 8
NSA_TQ = 128
MOE_CHUNKS = 4
P_QNSA = 0
P_VGLA = 1024
P_RGLA = 2048
P_KVCMP = 3072
P_KSEL = 3584
P_VSEL = 3840
P_KWIN = 4096
P_VWIN = 4352
P_QGLA = 4608
P_KGLA = 5120
P_WIDTH = 5632
S_WIDTH = 256


def _cparams(sem, vmem=VMEM_LIMIT):
    return pltpu.CompilerParams(dimension_semantics=sem, vmem_limit_bytes=vmem)


def _sigmoid(x):
    return 1.0 / (1.0 + jnp.exp(-x))


def _ada_kernel(c_ref, w_ref, b_ref, o_ref):
    c = c_ref[...]
    sc = c * _sigmoid(c)
    o_ref[...] = jnp.dot(sc, w_ref[...], preferred_element_type=F32, precision=HIGHEST) + b_ref[...]


def ada_mod(c, w_ada, b_ada, tn=1024):
    bsz, d = c.shape
    n = w_ada.shape[1]
    return pl.pallas_call(
        _ada_kernel,
        grid=(n // tn,),
        in_specs=[pl.BlockSpec((bsz, d), lambda j: (0, 0)),
                  pl.BlockSpec((d, tn), lambda j: (0, j)),
                  pl.BlockSpec((1, tn), lambda j: (0, j))],
        out_specs=pl.BlockSpec((bsz, tn), lambda j: (0, j)),
        out_shape=jax.ShapeDtypeStruct((bsz, n), F32),
        compiler_params=_cparams(("arbitrary",)),
        name="ada_mod",
    )(c, w_ada, b_ada.reshape(1, n))


def _inproj_kernel(x_ref, mod_ref, w_ref, o_ref, hb_ref):
    @pl.when(pl.program_id(1) == 0)
    def _():
        sc = mod_ref[0, 0:1, :]
        sh = mod_ref[0, 1:2, :]
        hb_ref[...] = (x_ref[...] * (1.0 + sc) + sh).astype(BF16)

    o_ref[...] = jnp.dot(hb_ref[...], w_ref[...], preferred_element_type=F32).astype(o_ref.dtype)


def in_proj(x2, mod1, w_bf16, seq, tm=1024, tn=512):
    n, d = x2.shape
    wn = w_bf16.shape[1]
    tm = min(tm, seq)
    per_b = seq // tm
    return pl.pallas_call(
        _inproj_kernel,
        grid=(n // tm, wn // tn),
        in_specs=[pl.BlockSpec((tm, d), lambda i, j: (i, 0)),
                  pl.BlockSpec((1, 2, d), lambda i, j: (i // per_b, 0, 0)),
                  pl.BlockSpec((d, tn), lambda i, j: (0, j))],
        out_specs=pl.BlockSpec((tm, tn), lambda i, j: (i, j)),
        out_shape=jax.ShapeDtypeStruct((n, wn), BF16),
        scratch_shapes=[pltpu.VMEM((tm, d), BF16)],
        compiler_params=_cparams(("arbitrary", "arbitrary")),
        name="in_proj",
    )(x2, mod1, w_bf16)


def _split_weight(w):
    hi = w.astype(BF16)
    lo = (w - hi.astype(F32)).astype(BF16)
    return jnp.concatenate([hi, lo], axis=1)


def _dot_split(a, w2_ref):
    n = w2_ref.shape[1] // 2
    a_hi = a.astype(BF16)
    a_lo = (a - a_hi.astype(F32)).astype(BF16)
    o2 = jnp.dot(a_hi, w2_ref[...], preferred_element_type=F32)
    return o2[:, 0:n] + o2[:, n:2 * n] + jnp.dot(a_lo, w2_ref[:, 0:n], preferred_element_type=F32)


def _inproj_small_kernel(x_ref, mod_ref, w_ref, o_ref):
    sc = mod_ref[0, 0:1, :]
    sh = mod_ref[0, 1:2, :]
    h = x_ref[...] * (1.0 + sc) + sh
    o_ref[...] = _dot_split(h, w_ref)


def in_proj_small(x2, mod1, w_f32, seq, tm=512):
    n, d = x2.shape
    wn = w_f32.shape[1]
    w2 = _split_weight(w_f32)
    tm = min(tm, seq)
    per_b = seq // tm
    return pl.pallas_call(
        _inproj_small_kernel,
        grid=(n // tm,),
        in_specs=[pl.BlockSpec((tm, d), lambda i: (i, 0)),
                  pl.BlockSpec((1, 2, d), lambda i: (i // per_b, 0, 0)),
                  pl.BlockSpec((d, 2 * wn), lambda i: (0, 0))],
        out_specs=pl.BlockSpec((tm, wn), lambda i: (i, 0)),
        out_shape=jax.ShapeDtypeStruct((n, wn), F32),
        compiler_params=_cparams(("arbitrary",)),
        name="in_proj_small",
    )(x2, mod1, w2)


def _gelu_tanh(x):
    return 0.5 * x * (1.0 + jnp.tanh(0.7978845608028654 * (x + 0.044715 * (x * x * x))))


def _compress_kernel(g_ref, pe_ref, w1_ref, b1_ref, w2_ref, o_ref):
    half = CMP_STRIDE * HEAD_DIM
    g = g_ref[0, 0].astype(F32)
    ga = (g + pe_ref[0, 0:1, :]).astype(BF16)
    gb = (g + pe_ref[0, 1:2, :]).astype(BF16)
    u = jnp.dot(ga, w1_ref[0, 0:half, :], preferred_element_type=F32)
    v = jnp.dot(gb, w1_ref[0, half:2 * half, :], preferred_element_type=F32)
    ng = u.shape[0]
    v_next = pltpu.roll(v, ng - 1, 0)
    h = _gelu_tanh(u + v_next + b1_ref[0])
    o_ref[0, 0] = jnp.dot(h.astype(BF16), w2_ref[0], preferred_element_type=F32).astype(o_ref.dtype)


def compress(kvg, pe2, w1, b1, w2):
    bsz, four, ng, gd = kvg.shape
    return pl.pallas_call(
        _compress_kernel,
        grid=(bsz, four),
        in_specs=[pl.BlockSpec((1, 1, ng, gd), lambda b, j: (b, j, 0, 0)),
                  pl.BlockSpec((1, 2, gd), lambda b, j: (j // 2, 0, 0)),
                  pl.BlockSpec((1, 2 * gd, CMP_HIDDEN), lambda b, j: (j // 2, 0, 0)),
                  pl.BlockSpec((1, 1, CMP_HIDDEN), lambda b, j: (j // 2, 0, 0)),
                  pl.BlockSpec((1, CMP_HIDDEN, HEAD_DIM), lambda b, j: (j // 2, 0, 0))],
        out_specs=pl.BlockSpec((1, 1, ng, HEAD_DIM), lambda b, j: (b, j, 0, 0)),
        out_shape=jax.ShapeDtypeStruct((bsz, four, ng, HEAD_DIM), BF16),
        compiler_params=_cparams(("arbitrary", "arbitrary")),
        name="compress",
    )(kvg, pe2, w1, b1, w2)


def _head_slope(h):
    return 2.0 ** (-(h + 1.0))


def _cmp_attn_kernel(q_ref, kv_ref, gate_ref, ovl_ref, o_ref, imp_ref, *, tq):
    i = pl.program_id(1)
    ncp = kv_ref.shape[2]
    scale = HEAD_DIM ** -0.5
    qrel = lax.broadcasted_iota(I32, (tq, 1), 0)
    pos = i * tq + qrel
    cidx = lax.broadcasted_iota(I32, (1, ncp), 1)
    ok = (cidx * CMP_STRIDE + (CMP_BLOCK - 1)) <= pos
    any_ok = (pos >= CMP_BLOCK - 1).astype(F32)
    mid_rel = (cidx * CMP_STRIDE - i * tq).astype(F32) + (CMP_BLOCK - 1) / 2.0
    gates = _sigmoid(gate_ref[...])
    for g in range(NSA_KV_HEADS):
        kc = kv_ref[0, g]
        vc = kv_ref[0, NSA_KV_HEADS + g]
        psum = jnp.zeros((tq, ncp), F32)
        for r in range(NSA_GROUP):
            h = g * NSA_GROUP + r
            qh = q_ref[:, h * HEAD_DIM:(h + 1) * HEAD_DIM]
            s = lax.dot_general(qh, kc, (((1,), (1,)), ((), ())), preferred_element_type=F32) * scale
            s = s + _head_slope(h) * mid_rel
            s = jnp.where(ok, s, NEG_INF)
            m = jnp.max(s, axis=-1, keepdims=True)
            e = jnp.exp(s - m)
            p = e * (any_ok / jnp.sum(e, axis=-1, keepdims=True))
            psum = psum + p
            o = jnp.dot(p.astype(BF16), vc, preferred_element_type=F32)
            o_ref[:, h * HEAD_DIM:(h + 1) * HEAD_DIM] = o * gates[:, 3 * h:3 * h + 1]
        imp_ref[0, g] = jnp.dot(psum, ovl_ref[...], preferred_element_type=F32, precision=HIGHEST)


def cmp_attention(p_buf, kvc, s_buf, ovl, bsz, seq, tq=128):
    n = p_buf.shape[0]
    ncp = kvc.shape[2]
    nsel = ovl.shape[1]
    per_b = seq // tq
    return pl.pallas_call(
        functools.partial(_cmp_attn_kernel, tq=tq),
        grid=(bsz, per_b),
        in_specs=[pl.BlockSpec((tq, NSA_WIDTH), lambda b, i: (b * per_b + i, P_QNSA // NSA_WIDTH)),
                  pl.BlockSpec((1, 4, ncp, HEAD_DIM), lambda b, i: (b, 0, 0, 0)),
                  pl.BlockSpec((tq, LANES), lambda b, i: (b * per_b + i, 0)),
                  pl.BlockSpec((ncp, nsel), lambda b, i: (0, 0))],
        out_specs=[pl.BlockSpec((tq, NSA_WIDTH), lambda b, i: (b * per_b + i, 0)),
                   pl.BlockSpec((1, NSA_KV_HEADS, tq, nsel), lambda b, i: (b, 0, i, 0))],
        out_shape=[jax.ShapeDtypeStruct((n, NSA_WIDTH), F32),
                   jax.ShapeDtypeStruct((bsz, NSA_KV_HEADS, seq, nsel), F32)],
        compiler_params=_cparams(("arbitrary", "arbitrary")),
        name="cmp_attention",
    )(p_buf, kvc, s_buf, ovl)


def _topk_kernel(imp_ref, o_ref, any_ref, *, seq, n_topk, qtile):
    rows, nsel = imp_ref.shape
    row0 = pl.program_id(0) * rows
    pos = (row0 + lax.broadcasted_iota(I32, (rows, 1), 0)) % seq
    cur = pos // SEL_BLOCK
    blk = lax.broadcasted_iota(I32, (1, nsel), 1)
    blkf = blk.astype(F32)
    forced = jnp.logical_or(blk == 0, jnp.logical_and(blk <= cur, blk > cur - SEL_LOCAL))
    work = jnp.where(forced, FORCE_SCORE, jnp.where(blk > cur, -1.0, imp_ref[...]))
    sel = jnp.zeros((rows, nsel), F32)
    for _ in range(n_topk):
        m = jnp.max(work, axis=-1, keepdims=True)
        first = jnp.min(jnp.where(work == m, blkf, float(nsel)), axis=-1, keepdims=True)
        pick = blkf == first
        sel = jnp.where(pick, 1.0, sel)
        work = jnp.where(pick, -2.0, work)
    keep = jnp.logical_and(sel > 0.5, blk <= cur)
    res = jnp.where(keep, 0.0, NEG_INF)
    if nsel < LANES:
        res = jnp.concatenate([res, jnp.full((rows, LANES - nsel), NEG_INF, F32)], axis=1)
    o_ref[...] = res.astype(o_ref.dtype)
    hit = jnp.where(res == 0.0, 1.0, 0.0)
    for qt in range(rows // qtile):
        any_ref[qt:qt + 1, :] = jnp.max(hit[qt * qtile:(qt + 1) * qtile], axis=0, keepdims=True)


def topk_select(imp2, seq, qtile, rows=1024):
    nrow, nsel = imp2.shape
    rows = min(rows, nrow)
    return pl.pallas_call(
        functools.partial(_topk_kernel, seq=seq, n_topk=min(SEL_TOPK, nsel), qtile=qtile),
        grid=(nrow // rows,),
        in_specs=[pl.BlockSpec((rows, nsel), lambda i: (i, 0))],
        out_specs=[pl.BlockSpec((rows, LANES), lambda i: (i, 0)),
                   pl.BlockSpec((rows // qtile, LANES), lambda i: (i, 0))],
        out_shape=[jax.ShapeDtypeStruct((nrow, LANES), BF16),
                   jax.ShapeDtypeStruct((nrow // qtile, LANES), F32)],
        compiler_params=_cparams(("arbitrary",)),
        name="topk_select",
    )(imp2)


def _sel_win_kernel(ids_ref, cnt_ref, q_ref, ks_ref, vst_ref, kw_ref, vwt_ref, sb_ref, kau_ref, kad_ref, oc_ref,
                    gate_ref, o_ref, qau_ref, qad_ref, qaw_ref, acc_ref, s_ref, mt_ref, *, tq, tk, seq):
    g = pl.program_id(1)
    i = pl.program_id(2)
    q0 = i * tq
    scale = HEAD_DIM ** -0.5
    R = NSA_GROUP
    C = R * tq
    D = HEAD_DIM
    dn = (((1,), (1,)), ((), ()))
    gslope = jnp.where(g == 0, 1.0, 2.0 ** (-NSA_GROUP))
    slopes = [gslope * 2.0 ** (-(r + 1)) for r in range(R)]

    qrel_col = jnp.concatenate([lax.broadcasted_iota(I32, (1, tq), 1)] * R, axis=1)
    slope_col = jnp.concatenate([jnp.full((1, tq), 1.0, F32) * slopes[r] for r in range(R)], axis=1)

    q = q_ref[...]
    selb = sb_ref[...].astype(F32)
    lane = lax.broadcasted_iota(I32, (1, LANES), 1)
    nrel = ((lane - q0 // SEL_BLOCK) * SEL_BLOCK).astype(F32)
    sub = lax.broadcasted_iota(I32, (LANES, 1), 0)
    for r in range(R):
        cols = slice(r * tq, (r + 1) * tq)
        qs_t = (q[:, r * D:(r + 1) * D].astype(F32) * scale).T.astype(BF16)
        sb_t = (selb + slopes[r] * nrel).T
        qau_ref[0:D, cols] = qs_t
        qau_ref[D:2 * D, cols] = jnp.where(sub == LANES - 1, slopes[r], sb_t).astype(BF16)
        qad_ref[0:D, cols] = qs_t
        qad_ref[D:2 * D, cols] = sb_t.astype(BF16)
        qaw_ref[0:D, cols] = qs_t
        aw = jnp.where(sub == 0, slopes[r] * SEL_BLOCK, jnp.where(sub == 1, slopes[r], 0.0))
        qaw_ref[D:2 * D, cols] = jnp.broadcast_to(aw, (D, tq)).astype(BF16)

    acc_ref[...] = jnp.zeros_like(acc_ref)

    def online_update(st, t, m, l):
        m_new = jnp.maximum(m, jnp.max(st, axis=0, keepdims=True))
        alpha = jnp.exp(m - m_new)
        p = jnp.exp(st - m_new)
        l_new = alpha * l + jnp.sum(p, axis=0, keepdims=True)
        pv = jnp.dot(vst_ref[t], p.astype(BF16), preferred_element_type=F32)
        acc_ref[...] = acc_ref[...] * alpha + pv
        return m_new, l_new

    S_SPARE, S_NEGINF = 2, 3

    @pl.when(jnp.logical_and(pl.program_id(0) == 0, jnp.logical_and(g == 0, i == 0)))
    def _():
        s_ref[S_NEGINF] = jnp.full((tk, C), -jnp.inf, F32)

    t_last = q0 // tk
    step = (pl.program_id(0) * pl.num_programs(1) + g) * pl.num_programs(2) + i
    cnt = cnt_ref[step]
    id0 = step * (seq // tk)

    def pipelined(u, carry):
        m, l = carry
        tb = ids_ref[id0 + jnp.maximum(u - 1, 0)]
        st = s_ref[jnp.where(u >= 1, (u + 1) % 2, S_NEGINF)]
        m_new = jnp.maximum(m, mt_ref[0:1, :])
        alpha = jnp.exp(m - m_new)
        p = jnp.exp(st - m_new)
        l_new = alpha * l + jnp.sum(p, axis=0, keepdims=True)
        pv = jnp.dot(vst_ref[tb], p.astype(BF16), preferred_element_type=F32)
        acc_ref[...] = acc_ref[...] * alpha + pv
        ta = ids_ref[id0 + jnp.minimum(u, cnt - 1)]
        k0 = pl.multiple_of(ta * tk, tk)
        ka = jnp.concatenate([ks_ref[pl.ds(k0, tk), :], kau_ref[pl.ds(k0, tk), :]], axis=1)
        sa = jnp.dot(ka, qau_ref[...], preferred_element_type=F32)
        s_ref[jnp.where(u < cnt, u % 2, S_SPARE)] = sa
        mt_ref[...] = jnp.broadcast_to(jnp.max(sa, axis=0, keepdims=True), mt_ref.shape)
        return m_new, l_new

    m0 = jnp.full((1, C), NEG_INF, F32)
    l0 = jnp.zeros((1, C), F32)
    mt_ref[...] = jnp.full(mt_ref.shape, -jnp.inf, F32)
    m, l = lax.fori_loop(0, jnp.where(cnt > 0, cnt + 1, 0), pipelined, (m0, l0))

    k0 = pl.multiple_of(t_last * tk, tk)
    ka = jnp.concatenate([ks_ref[pl.ds(k0, tk), :], kad_ref[pl.ds(k0, tk), :]], axis=1)
    st = jnp.dot(ka, qad_ref[...], preferred_element_type=F32)
    krow = lax.broadcasted_iota(I32, (tk, 1), 0)
    koff = (krow % SEL_BLOCK).astype(F32)
    st = st + koff * slope_col
    st = jnp.where(krow + (k0 - q0) <= qrel_col, st, NEG_INF)
    m, l = online_update(st, t_last, m, l)
    inv_l = 1.0 / l

    wk = WINDOW + tq
    w0 = pl.multiple_of(jnp.maximum(q0 - WINDOW, 0), tq)
    wrow = lax.broadcasted_iota(I32, (wk, 1), 0) + (w0 - q0)
    w_hi = jnp.floor(wrow.astype(F32) * (1.0 / SEL_BLOCK))
    w_lo = wrow.astype(F32) - w_hi * SEL_BLOCK
    kaw = jnp.where(lane == 0, w_hi, jnp.where(lane == 1, w_lo, 0.0)).astype(BF16)
    kwa = jnp.concatenate([kw_ref[pl.ds(w0, wk), :], kaw], axis=1)
    sw = jnp.dot(kwa, qaw_ref[...], preferred_element_type=F32)
    dist = qrel_col - wrow
    sw = jnp.where(jnp.logical_and(dist >= 0, dist < WINDOW), sw, NEG_INF)
    mw = jnp.max(sw, axis=0, keepdims=True)
    pw = jnp.exp(sw - mw)
    inv_lw = 1.0 / jnp.sum(pw, axis=0, keepdims=True)
    c0 = w0 // tq
    vwt = jnp.concatenate([vwt_ref[c0 + c] for c in range(wk // tq)], axis=1)
    ow = jnp.dot(vwt, pw.astype(BF16), preferred_element_type=F32)

    gates_t = _sigmoid(gate_ref[...]).T
    osel = acc_ref[...] * inv_l
    ow = ow * inv_lw
    for r in range(R):
        cols = slice(r * tq, (r + 1) * tq)
        g_sel = jnp.where(g == 0, gates_t[3 * r + 1:3 * r + 2, :], gates_t[3 * (R + r) + 1:3 * (R + r) + 2, :])
        g_win = jnp.where(g == 0, gates_t[3 * r + 2:3 * r + 3, :], gates_t[3 * (R + r) + 2:3 * (R + r) + 3, :])
        y_t = g_sel * osel[:, cols] + g_win * ow[:, cols]
        y = oc_ref[:, r * D:(r + 1) * D] + y_t.T
        o_ref[:, r * D:(r + 1) * D] = y.astype(o_ref.dtype)


def _key_side_constants(seq):
    nsel = seq // SEL_BLOCK
    key = jnp.arange(seq)
    blk = key // SEL_BLOCK
    lane = jnp.arange(LANES)[None, :]
    onehot = (lane == blk[:, None])
    kad = onehot.astype(BF16)
    kau = jnp.where(lane == LANES - 1, (key % SEL_BLOCK)[:, None].astype(F32),
                    jnp.logical_and(onehot, lane < nsel - SEL_LOCAL).astype(F32)).astype(BF16)
    return kau, kad


def _active_key_tiles(blk_any, seq, tq, tk):
    nt = seq // tk
    per_b = seq // tq
    bpt = tk // SEL_BLOCK
    tile_any = blk_any[:, :nt * bpt].reshape(-1, nt, bpt).max(axis=-1) > 0.5
    t_last = (jnp.arange(blk_any.shape[0], dtype=I32) % per_b) * tq // tk
    tile_idx = jnp.arange(nt, dtype=I32)[None, :]
    active = jnp.logical_and(tile_any, tile_idx < t_last[:, None])
    ids = jnp.argsort(jnp.where(active, tile_idx, nt + tile_idx), axis=-1).astype(I32)
    return ids.reshape(-1), active.sum(axis=-1).astype(I32)


def sel_win_attention(p_buf, selb, blk_any, ocmp, s_buf, bsz, seq, tq=128, tk=512):
    n = p_buf.shape[0]
    tk = min(tk, seq)
    per_b = seq // tq
    gw = NSA_GROUP * HEAD_DIM
    G = NSA_KV_HEADS
    D = HEAD_DIM
    kau, kad = _key_side_constants(seq)
    if blk_any.shape[1] < (seq // tk) * (tk // SEL_BLOCK):
        blk_any = jnp.pad(blk_any, ((0, 0), (0, (seq // tk) * (tk // SEL_BLOCK) - blk_any.shape[1])))
    tile_ids, tile_cnt = _active_key_tiles(blk_any, seq, tq, tk)
    vsel = p_buf[:, P_VSEL:P_VSEL + KV_WIDTH].reshape(bsz, seq // tk, tk, G, D).transpose(0, 3, 1, 4, 2)
    vwin = p_buf[:, P_VWIN:P_VWIN + KV_WIDTH].reshape(bsz, seq // tq, tq, G, D).transpose(0, 3, 1, 4, 2)

    def colblk(off):
        return lambda b, g, i, ids, cnt: (b, off // D + g)

    return pl.pallas_call(
        functools.partial(_sel_win_kernel, tq=tq, tk=tk, seq=seq),
        grid_spec=pltpu.PrefetchScalarGridSpec(
            num_scalar_prefetch=2,
            grid=(bsz, G, per_b),
            in_specs=[pl.BlockSpec((tq, gw), lambda b, g, i, ids, cnt: (b * per_b + i, P_QNSA // gw + g)),
                      pl.BlockSpec((seq, D), colblk(P_KSEL)),
                      pl.BlockSpec((None, None, seq // tk, D, tk), lambda b, g, i, ids, cnt: (b, g, 0, 0, 0)),
                      pl.BlockSpec((seq, D), colblk(P_KWIN)),
                      pl.BlockSpec((None, None, seq // tq, D, tq), lambda b, g, i, ids, cnt: (b, g, 0, 0, 0)),
                      pl.BlockSpec((tq, LANES), lambda b, g, i, ids, cnt: ((b * G + g) * per_b + i, 0)),
                      pl.BlockSpec((seq, LANES), lambda b, g, i, ids, cnt: (0, 0)),
                      pl.BlockSpec((seq, LANES), lambda b, g, i, ids, cnt: (0, 0)),
                      pl.BlockSpec((tq, gw), lambda b, g, i, ids, cnt: (b * per_b + i, g)),
                      pl.BlockSpec((tq, LANES), lambda b, g, i, ids, cnt: (b * per_b + i, 0))],
            out_specs=pl.BlockSpec((tq, gw), lambda b, g, i, ids, cnt: (b * per_b + i, g)),
            scratch_shapes=[pltpu.VMEM((2 * D, NSA_GROUP * tq), BF16),
                            pltpu.VMEM((2 * D, NSA_GROUP * tq), BF16),
                            pltpu.VMEM((2 * D, NSA_GROUP * tq), BF16),
                            pltpu.VMEM((D, NSA_GROUP * tq), F32),
                            pltpu.VMEM((4, tk, NSA_GROUP * tq), F32),
                            pltpu.VMEM((8, NSA_GROUP * tq), F32)]),
        out_shape=jax.ShapeDtypeStruct((n, NSA_WIDTH), BF16),
        compiler_params=_cparams(("arbitrary", "arbitrary", "arbitrary")),
        name="sel_win_attention",
    )(tile_ids, tile_cnt, p_buf, p_buf, vsel, p_buf, vwin, selb, kau, kad, ocmp, s_buf)


def _gla_kernel(q_ref, k_ref, v_ref, r_ref, a_ref, wg_ref, bg_ref, nw_ref, tril_ref, o_ref,
                st_ref, b_ref, *, tc):
    C, SUB, H = GLA_CHUNK, GLA_SUB, GLA_HEADS

    @pl.when(pl.program_id(1) == 0)
    def _():
        st_ref[...] = jnp.zeros_like(st_ref)

    x = jnp.dot(a_ref[...], wg_ref[...], preferred_element_type=F32, precision=HIGHEST) + bg_ref[...]
    la = (jnp.minimum(x, 0.0) - jnp.log(1.0 + jnp.exp(-jnp.abs(x)))) * (1.0 / GLA_GATE_TAU)
    tril = tril_ref[...]
    for c in range(tc // C):
        b_ref[c * C:(c + 1) * C, :] = jnp.dot(tril, la[c * C:(c + 1) * C, :],
                                              preferred_element_type=F32, precision=HIGHEST)

    row16 = lax.broadcasted_iota(I32, (SUB, C), 0)
    lane64 = lax.broadcasted_iota(I32, (SUB, C), 1)
    nw = nw_ref[...]

    def chunk(c, carry):
        r0 = pl.multiple_of(c * C, C)
        for h in range(H):
            bh = b_ref[pl.ds(r0, C), h * GLA_DK:(h + 1) * GLA_DK]
            qc = q_ref[pl.ds(r0, C), h * GLA_DK:(h + 1) * GLA_DK].astype(F32) * (GLA_DK ** -0.5)
            kc = k_ref[pl.ds(r0, C), h * GLA_DK:(h + 1) * GLA_DK].astype(F32)
            vc = v_ref[pl.ds(r0, C), h * GLA_DV:(h + 1) * GLA_DV]
            st = st_ref[h]
            b_last = bh[C - 1:C, :]
            qe = (qc * jnp.exp(bh)).astype(BF16)
            o = lax.dot_general(qe, st.astype(BF16), (((1,), (1,)), ((), ())),
                                preferred_element_type=F32)
            a_rows = []
            for sb in range(C // SUB):
                s0 = sb * SUB
                bi = bh[s0:s0 + SUB, :]
                qi = qc[s0:s0 + SUB, :]
                ki = kc[s0:s0 + SUB, :]
                beta = bh[s0:s0 + 1, :]
                if sb > 0:
                    qd = (qi * jnp.exp(bi - beta)).astype(BF16)
                    kd = (kc * jnp.exp(jnp.minimum(beta - bh, 0.0))).astype(BF16)
                    a_i = lax.dot_general(qd, kd, (((1,), (1,)), ((), ())), preferred_element_type=F32)
                    a_i = jnp.where(lane64 < s0, a_i, 0.0)
                else:
                    a_i = jnp.zeros((SUB, C), F32)
                for s in range(SUB):
                    e = jnp.exp(jnp.minimum(bi - bi[s:s + 1, :], 0.0))
                    col = jnp.sum(qi * ki[s:s + 1, :] * e, axis=-1, keepdims=True)
                    a_i = jnp.where(jnp.logical_and(lane64 == s0 + s, row16 >= s), col, a_i)
                a_rows.append(a_i)
            a_mat = jnp.concatenate(a_rows, axis=0)
            o = o + jnp.dot(a_mat.astype(BF16), vc, preferred_element_type=F32)
            kdec = (kc * jnp.exp(b_last - bh)).astype(BF16)
            upd = lax.dot_general(vc, kdec, (((0,), (0,)), ((), ())), preferred_element_type=F32)
            st_ref[h] = st * jnp.exp(b_last) + upd
            rms = lax.rsqrt(jnp.mean(o * o, axis=-1, keepdims=True) + RMS_EPS)
            rr = r_ref[pl.ds(r0, C), h * GLA_DV:(h + 1) * GLA_DV].astype(F32)
            y = o * rms * nw * (rr * _sigmoid(rr))
            o_ref[pl.ds(r0, C), h * GLA_DV:(h + 1) * GLA_DV] = y.astype(o_ref.dtype)
        return carry

    lax.fori_loop(0, tc // C, chunk, 0)


def gla_attention(p_buf, s_buf, wg_pad, bg, nw, tril, bsz, seq, tc=512):
    n = p_buf.shape[0]
    tc = min(tc, seq)
    per_b = seq // tc
    kw = GLA_HEADS * GLA_DK
    return pl.pallas_call(
        functools.partial(_gla_kernel, tc=tc),
        grid=(bsz, per_b),
        in_specs=[pl.BlockSpec((tc, kw), lambda b, j: (b * per_b + j, P_QGLA // kw)),
                  pl.BlockSpec((tc, kw), lambda b, j: (b * per_b + j, P_KGLA // kw)),
                  pl.BlockSpec((tc, GLA_WIDTH), lambda b, j: (b * per_b + j, P_VGLA // GLA_WIDTH)),
                  pl.BlockSpec((tc, GLA_WIDTH), lambda b, j: (b * per_b + j, P_RGLA // GLA_WIDTH)),
                  pl.BlockSpec((tc, LANES), lambda b, j: (b * per_b + j, 1)),
                  pl.BlockSpec((LANES, kw), lambda b, j: (0, 0)),
                  pl.BlockSpec((1, kw), lambda b, j: (0, 0)),
                  pl.BlockSpec((1, GLA_DV), lambda b, j: (0, 0)),
                  pl.BlockSpec((GLA_CHUNK, GLA_CHUNK), lambda b, j: (0, 0))],
        out_specs=pl.BlockSpec((tc, GLA_WIDTH), lambda b, j: (b * per_b + j, 0)),
        out_shape=jax.ShapeDtypeStruct((n, GLA_WIDTH), BF16),
        scratch_shapes=[pltpu.VMEM((GLA_HEADS, GLA_DV, GLA_DK), F32),
                        pltpu.VMEM((tc, kw), F32)],
        compiler_params=_cparams(("arbitrary", "arbitrary")),
        name="gla_attention",
    )(p_buf, p_buf, p_buf, p_buf, s_buf, wg_pad, bg, nw, tril)


def _pack_words(lo, hi):
    lo_b = pltpu.bitcast(lo.astype(BF16).astype(F32), U32)
    hi_b = pltpu.bitcast(hi.astype(BF16).astype(F32), U32)
    return jnp.bitwise_or(hi_b, jnp.right_shift(lo_b, jnp.uint32(16)))


def _unpack_words(w):
    lo = pltpu.bitcast(jnp.left_shift(w, jnp.uint32(16)), F32)
    hi = pltpu.bitcast(jnp.bitwise_and(w, jnp.uint32(0xFFFF0000)), F32)
    return lo, hi


def _store_packed(ref, v, first_tile=0):
    rows = v.shape[0]
    n_tiles = ref.shape[0] // rows
    for u in range(v.shape[1] // (2 * LANES)):
        w = _pack_words(v[:, 2 * u * LANES:(2 * u + 1) * LANES], v[:, (2 * u + 1) * LANES:(2 * u + 2) * LANES])
        ref[pl.ds(first_tile + u, rows, stride=n_tiles), :] = w


def _load_word_tile(ref, s, rows, base=0):
    n_tiles = WORD_TILES
    return ref[pl.ds(base * n_tiles + s, rows, stride=n_tiles), :]


def _layer_norm(z, g, b):
    mu = jnp.mean(z, axis=-1, keepdims=True)
    zc = z - mu
    var = jnp.mean(zc * zc, axis=-1, keepdims=True)
    return zc * lax.rsqrt(var + LN_EPS) * g + b


def _outproj_kernel(yn_ref, yg_ref, wo_ref, x_ref, mod_ref, lng_ref, lnb_ref, wr_ref, br_ref, ltri_ref,
                    x1_ref, hw_ref, idx_ref, tw_ref, rank_ref, cnt_ref, carry_ref):
    step = pl.program_id(0)

    @pl.when(step == 0)
    def _():
        carry_ref[...] = jnp.zeros_like(carry_ref)

    half = yn_ref.shape[1]
    mix = (jnp.dot(yn_ref[...], wo_ref[0:half, :], preferred_element_type=F32) +
           jnp.dot(yg_ref[...], wo_ref[half:2 * half, :], preferred_element_type=F32))
    g1 = mod_ref[0, 0:1, :]
    sc2 = mod_ref[0, 1:2, :]
    sh2 = mod_ref[0, 2:3, :]
    x1 = _layer_norm(DN_ALPHA * x_ref[...] + g1 * mix, lng_ref[...], lnb_ref[...])
    x1_ref[...] = x1
    h2 = x1 * (1.0 + sc2) + sh2
    _store_packed(hw_ref, h2)

    tm = h2.shape[0]
    ne = br_ref.shape[1]
    logits = _dot_split(h2, wr_ref)[:, 0:ne] + br_ref[...]
    lane = lax.broadcasted_iota(I32, (1, ne), 1).astype(F32)
    out_lane = lax.broadcasted_iota(I32, (1, LANES), 1)
    work = logits
    idx_out = jnp.zeros((tm, LANES), F32)
    w_out = jnp.zeros((tm, LANES), F32)
    onehot = jnp.zeros((tm, ne), F32)
    picks = []
    top0 = None
    den = jnp.zeros((tm, 1), F32)
    for k in range(TOP_K):
        m = jnp.max(work, axis=-1, keepdims=True)
        first = jnp.min(jnp.where(work == m, lane, float(ne)), axis=-1, keepdims=True)
        pick = lane == first
        if k == 0:
            top0 = m
        e = jnp.exp(m - top0)
        den = den + e
        idx_out = jnp.where(out_lane == k, first, idx_out)
        w_out = jnp.where(out_lane == k, e, w_out)
        onehot = jnp.where(pick, 1.0, onehot)
        picks.append(pick)
        work = jnp.where(pick, -jnp.inf, work)
    idx_ref[...] = idx_out.astype(I32)
    tw_ref[...] = w_out * (1.0 / den)

    before = jnp.dot(ltri_ref[...], onehot.astype(BF16), preferred_element_type=F32) + carry_ref[0:1, :]
    rank_out = jnp.zeros((tm, LANES), F32)
    for k in range(TOP_K):
        rk = jnp.sum(jnp.where(picks[k], before, 0.0), axis=-1, keepdims=True)
        rank_out = jnp.where(out_lane == k, rk, rank_out)
    rank_ref[...] = rank_out.astype(I32)
    carry_ref[...] = carry_ref[...] + jnp.sum(onehot, axis=0, keepdims=True)
    cnt_ref[...] = carry_ref[...].astype(I32)


def out_proj_router(ynsa, ygla, wo_bf16, x2, mod2, ln_g, ln_b, w_router, b_router, ltri, seq, tm=512):
    n, d = x2.shape
    tm = min(tm, seq)
    per_b = seq // tm
    ne = w_router.shape[1]
    wr2 = _split_weight(jnp.pad(w_router, ((0, 0), (0, LANES - ne))))
    row = lambda i: (i, 0)
    const = lambda i: (0, 0)
    return pl.pallas_call(
        _outproj_kernel,
        grid=(n // tm,),
        in_specs=[pl.BlockSpec((tm, NSA_WIDTH), row),
                  pl.BlockSpec((tm, GLA_WIDTH), row),
                  pl.BlockSpec((d, d), const),
                  pl.BlockSpec((tm, d), row),
                  pl.BlockSpec((1, 3, d), lambda i: (i // per_b, 0, 0)),
                  pl.BlockSpec((1, d), const),
                  pl.BlockSpec((1, d), const),
                  pl.BlockSpec((d, 2 * LANES), const),
                  pl.BlockSpec((1, ne), const),
                  pl.BlockSpec((tm, tm), const)],
        out_specs=[pl.BlockSpec((tm, d), row),
                   pl.BlockSpec((tm * WORD_TILES, LANES), row),
                   pl.BlockSpec((tm, LANES), row),
                   pl.BlockSpec((tm, LANES), row),
                   pl.BlockSpec((tm, LANES), row),
                   pl.BlockSpec((8, ne), const)],
        out_shape=[jax.ShapeDtypeStruct((n, d), F32),
                   jax.ShapeDtypeStruct((n * WORD_TILES, LANES), U32),
                   jax.ShapeDtypeStruct((n, LANES), I32),
                   jax.ShapeDtypeStruct((n, LANES), F32),
                   jax.ShapeDtypeStruct((n, LANES), I32),
                   jax.ShapeDtypeStruct((8, ne), I32)],
        scratch_shapes=[pltpu.VMEM((8, ne), F32)],
        compiler_params=_cparams(("arbitrary",)),
        name="out_proj_router",
    )(ynsa, ygla, wo_bf16, x2, mod2, ln_g, ln_b, wr2, b_router, ltri)


def _issue_token_copies(idx_at, src_ref, dst_ref, sem, count):
    wt = WORD_TILES

    def issue(r8, carry):
        for u in range(GATHER_UNROLL):
            r = r8 * GATHER_UNROLL + u
            tok = idx_at(r)
            pltpu.make_async_copy(src_ref.at[pl.ds(pl.multiple_of(tok * wt, wt), wt)],
                                  dst_ref.at[pl.ds(pl.multiple_of(r * wt, wt), wt)], sem).start(priority=u % 2)
        return carry

    lax.fori_loop(0, count // GATHER_UNROLL, issue, 0)


def _dispatch_kernel(slot_ref, src_ref, zero_ref, o_ref, sem, *, tc):
    del zero_ref
    wt = WORD_TILES

    def issue(t8, carry):
        for u in range(GATHER_UNROLL // TOP_K):
            t = t8 * (GATHER_UNROLL // TOP_K) + u
            for k in range(TOP_K):
                slot = slot_ref[0, 0, t * TOP_K + k]
                pltpu.make_async_copy(src_ref.at[pl.ds(pl.multiple_of(t * wt, wt), wt)],
                                      o_ref.at[pl.ds(pl.multiple_of(slot * wt, wt), wt)], sem).start(priority=k % 2)
        return carry

    lax.fori_loop(0, tc * TOP_K // GATHER_UNROLL, issue, 0)
    for k in range(TOP_K):
        pltpu.make_async_copy(src_ref, o_ref.at[pl.ds(0, tc * wt)], sem).wait()


def dispatch_rows(slot3, src, n_slots, tc):
    wt = WORD_TILES
    n = src.shape[0] // wt
    zeros = jnp.zeros((n_slots * wt, LANES), src.dtype)
    return pl.pallas_call(
        functools.partial(_dispatch_kernel, tc=tc),
        grid=(n // tc,),
        in_specs=[pl.BlockSpec((1, 1, tc * TOP_K), lambda i: (i, 0, 0), memory_space=pltpu.SMEM),
                  pl.BlockSpec((tc * wt, LANES), lambda i: (i, 0)),
                  pl.BlockSpec(memory_space=pl.ANY)],
        out_specs=pl.BlockSpec(memory_space=pl.ANY),
        out_shape=jax.ShapeDtypeStruct((n_slots * wt, LANES), src.dtype),
        scratch_shapes=[pltpu.SemaphoreType.DMA(())],
        input_output_aliases={2: 0},
        compiler_params=_cparams(("arbitrary",)),
        name="dispatch_scatter",
    )(slot3, src, zeros)


def _moe_kernel(be_ref, nused_ref, nchunk_ref, x_ref, wg_ref, wu_ref, bg_ref, bu_ref, wd_ref, bd_ref, o_ref,
                xb_ref, acc_ref, *, tm):
    i = pl.program_id(0)
    j = pl.program_id(1)
    nj = pl.num_programs(1)
    used = i < nused_ref[0]
    sub = tm // MOE_CHUNKS

    @pl.when(jnp.logical_and(used, j == 0))
    def _():
        for s in range(WORD_TILES):
            lo, hi = _unpack_words(_load_word_tile(x_ref, s, tm))
            xb_ref[:, 2 * s * LANES:(2 * s + 1) * LANES] = lo.astype(BF16)
            xb_ref[:, (2 * s + 1) * LANES:(2 * s + 2) * LANES] = hi.astype(BF16)
        acc_ref[...] = jnp.zeros_like(acc_ref)

    for c in range(1, MOE_CHUNKS + 1):
        @pl.when(jnp.logical_and(used, nchunk_ref[i] == c))
        def _(c=c):
            xb = xb_ref[0:c * sub, :]
            gt = jnp.dot(xb, wg_ref[0].astype(BF16), preferred_element_type=F32) + bg_ref[0]
            up = jnp.dot(xb, wu_ref[0].astype(BF16), preferred_element_type=F32) + bu_ref[0]
            gt = jnp.minimum(gt, SWIGLU_LIMIT)
            up = jnp.clip(up, -SWIGLU_LIMIT, SWIGLU_LIMIT)
            act = (up + 1.0) * gt * _sigmoid(SWIGLU_ALPHA * gt)
            acc_ref[0:c * sub, :] += jnp.dot(act.astype(BF16), wd_ref[0].astype(BF16),
                                             preferred_element_type=F32)

    @pl.when(jnp.logical_and(used, j == nj - 1))
    def _():
        _store_packed(o_ref, acc_ref[...] + bd_ref[0])

    @pl.when(jnp.logical_and(jnp.logical_not(used), j == nj - 1))
    def _():
        o_ref[...] = jnp.zeros_like(o_ref)


def moe_experts(block_expert, nused, block_chunks, xbuf, w_gate_up, b_gate_up, w_down, b_down, tm, tf=256):
    wt = WORD_TILES
    nslots = xbuf.shape[0] // wt
    ne, d, f2 = w_gate_up.shape
    f = f2 // 2
    tf = min(tf, f)
    nfj = f // tf
    nblk = nslots // tm
    b_gu3 = b_gate_up.reshape(ne, 1, f2)
    b_d3 = b_down.reshape(ne, 1, d)

    def jj(i, j, nu):
        return jnp.where(i < nu[0], j, nfj - 1)

    return pl.pallas_call(
        functools.partial(_moe_kernel, tm=tm),
        grid_spec=pltpu.PrefetchScalarGridSpec(
            num_scalar_prefetch=3,
            grid=(nblk, nfj),
            in_specs=[pl.BlockSpec((tm * wt, LANES), lambda i, j, be, nu, nc: (i, 0)),
                      pl.BlockSpec((1, d, tf), lambda i, j, be, nu, nc: (be[i], 0, jj(i, j, nu))),
                      pl.BlockSpec((1, d, tf), lambda i, j, be, nu, nc: (be[i], 0, nfj + jj(i, j, nu))),
                      pl.BlockSpec((1, 1, tf), lambda i, j, be, nu, nc: (be[i], 0, jj(i, j, nu))),
                      pl.BlockSpec((1, 1, tf), lambda i, j, be, nu, nc: (be[i], 0, nfj + jj(i, j, nu))),
                      pl.BlockSpec((1, tf, d), lambda i, j, be, nu, nc: (be[i], jj(i, j, nu), 0)),
                      pl.BlockSpec((1, 1, d), lambda i, j, be, nu, nc: (be[i], 0, 0))],
            out_specs=pl.BlockSpec((tm * wt, LANES), lambda i, j, be, nu, nc: (i, 0)),
            scratch_shapes=[pltpu.VMEM((tm, d), BF16), pltpu.VMEM((tm, d), F32)]),
        out_shape=jax.ShapeDtypeStruct((nslots * wt, LANES), U32),
        compiler_params=_cparams(("arbitrary", "arbitrary")),
        name="moe_experts",
    )(block_expert, nused, block_chunks, xbuf, w_gate_up, w_gate_up, b_gu3, b_gu3, w_down, b_d3)


def _combine_kernel(slot_ref, y_ref, tw_ref, x1_ref, mod_ref, lng_ref, lnb_ref, o_ref, yb_ref, sem, *, tc):
    _issue_token_copies(lambda a: slot_ref[0, 0, a], y_ref, yb_ref, sem, TOP_K * tc)
    pltpu.make_async_copy(y_ref.at[pl.ds(0, TOP_K * tc * WORD_TILES)], yb_ref, sem).wait()

    tw = tw_ref[...]
    wks = [tw[:, k:k + 1] for k in range(TOP_K)]
    cols = []
    for s in range(WORD_TILES):
        f_lo = jnp.zeros((tc, LANES), F32)
        f_hi = jnp.zeros((tc, LANES), F32)
        for k in range(TOP_K):
            lo, hi = _unpack_words(_load_word_tile(yb_ref, s, tc, base=k * tc))
            f_lo = f_lo + wks[k] * lo
            f_hi = f_hi + wks[k] * hi
        cols += [f_lo, f_hi]
    ffn = jnp.concatenate(cols, axis=-1)
    g2 = mod_ref[0, 0:1, :]
    o_ref[...] = _layer_norm(DN_ALPHA * x1_ref[...] + g2 * ffn, lng_ref[...], lnb_ref[...])


def combine(slot3, ybuf, top_w, x1, modg2, ln_g, ln_b, seq, tc=256):
    n, d = x1.shape
    tc = min(tc, seq)
    per_b = seq // tc
    return pl.pallas_call(
        functools.partial(_combine_kernel, tc=tc),
        grid=(n // tc,),
        in_specs=[pl.BlockSpec((1, 1, TOP_K * tc), lambda i: (i, 0, 0), memory_space=pltpu.SMEM),
                  pl.BlockSpec(memory_space=pl.ANY),
                  pl.BlockSpec((tc, LANES), lambda i: (i, 0)),
                  pl.BlockSpec((tc, d), lambda i: (i, 0)),
                  pl.BlockSpec((1, 1, d), lambda i: (i // per_b, 0, 0)),
                  pl.BlockSpec((1, d), lambda i: (0, 0)),
                  pl.BlockSpec((1, d), lambda i: (0, 0))],
        out_specs=pl.BlockSpec((tc, d), lambda i: (i, 0)),
        out_shape=jax.ShapeDtypeStruct((n, d), F32),
        scratch_shapes=[pltpu.VMEM((TOP_K * tc * WORD_TILES, LANES), U32), pltpu.SemaphoreType.DMA(())],
        compiler_params=_cparams(("arbitrary",)),
        name="combine",
    )(slot3, ybuf, top_w, x1, modg2, ln_g, ln_b)


def _split_w_in(w_in):
    offs = [0]
    for w in PROJ_WIDTHS:
        offs.append(offs[-1] + w)
    seg = [w_in[:, offs[k]:offs[k + 1]] for k in range(len(PROJ_WIDTHS))]
    (q_nsa, k_cmp, v_cmp, k_sel, v_sel, k_win, v_win, g_nsa, q_gla, k_gla, v_gla, a_gla, r_gla) = seg
    big = jnp.concatenate([q_nsa, v_gla, r_gla, k_cmp, v_cmp, k_sel, v_sel, k_win, v_win, q_gla, k_gla],
                          axis=1).astype(BF16)
    pad = lambda w: jnp.pad(w, ((0, 0), (0, LANES - w.shape[1])))
    small = jnp.concatenate([pad(g_nsa), pad(a_gla)], axis=1)
    return big, small


def _overlap_matrix(ncp, nsel):
    cs = jnp.arange(ncp)[:, None] * CMP_STRIDE
    ss = jnp.arange(nsel)[None, :] * SEL_BLOCK
    return jnp.logical_and(cs < ss + SEL_BLOCK, cs + CMP_BLOCK > ss).astype(F32)


MOE_TM = 1024


def kernel(x, c, w_ada, b_ada, w_in, cmp_pe_k, cmp_w1_k, cmp_b1_k, cmp_w2_k, cmp_pe_v, cmp_w1_v, cmp_b1_v,
           cmp_w2_v, gla_w_gate, gla_b_gate, gla_norm_w, w_out, ln1_g, ln1_b, w_router, b_router, w_gate_up,
           b_gate_up, w_down, b_down, ln2_g, ln2_b):
    return _forward(x, c, w_ada, b_ada, w_in, cmp_pe_k, cmp_w1_k, cmp_b1_k, cmp_w2_k, cmp_pe_v, cmp_w1_v,
                    cmp_b1_v, cmp_w2_v, gla_w_gate, gla_b_gate, gla_norm_w, w_out, ln1_g, ln1_b, w_router,
                    b_router, w_gate_up, b_gate_up, w_down, b_down, ln2_g, ln2_b, moe_tm=MOE_TM)


def _forward(x, c, w_ada, b_ada, w_in, cmp_pe_k, cmp_w1_k, cmp_b1_k, cmp_w2_k, cmp_pe_v, cmp_w1_v, cmp_b1_v,
             cmp_w2_v, gla_w_gate, gla_b_gate, gla_norm_w, w_out, ln1_g, ln1_b, w_router, b_router, w_gate_up,
             b_gate_up, w_down, b_down, ln2_g, ln2_b, *, moe_tm):
    bsz, seq, d = x.shape
    n = bsz * seq
    l = 0
    x2 = x.reshape(n, d)

    mod = ada_mod(c, w_ada[l], b_ada[l])
    sh1, sc1, g1, sh2, sc2, g2 = jnp.split(mod, 6, axis=-1)
    mod1 = jnp.stack([sc1, sh1], axis=1)
    mod2 = jnp.stack([g1, sc2, sh2], axis=1)
    modg2 = g2[:, None, :]

    w_big, w_small = _split_w_in(w_in[l])
    p_buf = in_proj(x2, mod1, w_big, seq)
    s_buf = in_proj_small(x2, mod1, w_small, seq)

    ng = seq // CMP_STRIDE
    kvg = p_buf[:, P_KVCMP:P_KVCMP + 2 * KV_WIDTH].reshape(bsz, ng, CMP_STRIDE, 4, HEAD_DIM)
    kvg = kvg.transpose(0, 3, 1, 2, 4).reshape(bsz, 4, ng, CMP_STRIDE * HEAD_DIM)
    half = CMP_STRIDE * HEAD_DIM
    pe2 = jnp.stack([cmp_pe_k[l].reshape(2, half), cmp_pe_v[l].reshape(2, half)])
    w1 = jnp.stack([cmp_w1_k[l], cmp_w1_v[l]]).astype(BF16)
    b1 = jnp.stack([cmp_b1_k[l], cmp_b1_v[l]])[:, None, :]
    w2 = jnp.stack([cmp_w2_k[l], cmp_w2_v[l]]).astype(BF16)
    kvc = compress(kvg, pe2, w1, b1, w2)

    nsel = seq // SEL_BLOCK
    ovl = _overlap_matrix(ng, nsel)
    ocmp, imp = cmp_attention(p_buf, kvc, s_buf, ovl, bsz, seq)
    selb, blk_any = topk_select(imp.reshape(bsz * NSA_KV_HEADS * seq, nsel), seq, qtile=NSA_TQ)
    y_nsa = sel_win_attention(p_buf, selb, blk_any, ocmp, s_buf, bsz, seq, tq=NSA_TQ)

    wg_pad = jnp.pad(gla_w_gate[l], ((0, LANES - GLA_GATE_RANK), (0, 0)))
    tril = jnp.tril(jnp.ones((GLA_CHUNK, GLA_CHUNK), F32))
    y_gla = gla_attention(p_buf, s_buf, wg_pad, gla_b_gate[l][None, :], gla_norm_w[l][None, :], tril, bsz, seq)

    tm_r = min(512, seq)
    ltri = jnp.tril(jnp.ones((tm_r, tm_r), F32), k=-1).astype(BF16)
    x1, hw, top_idx, top_w, rank, counts = out_proj_router(
        y_nsa, y_gla, w_out[l].astype(BF16), x2, mod2, ln1_g[l][None, :], ln1_b[l][None, :],
        w_router[l], b_router[l][None, :], ltri, seq, tm=tm_r)

    tm = moe_tm
    n_asg = n * TOP_K
    cnt = counts[0]
    padded = (cnt + tm - 1) // tm * tm
    ends_p = jnp.cumsum(padded)
    start_p = ends_p - padded
    idx4 = top_idx[:, :TOP_K]
    slot_of = (start_p[idx4] + rank[:, :TOP_K]).astype(I32)
    n_blocks = -(-n_asg // tm) + N_EXPERTS
    n_slots = n_blocks * tm
    nused = (ends_p[-1] // tm).astype(I32).reshape(1)
    blk_start = jnp.minimum(jnp.arange(n_blocks, dtype=I32), nused - 1) * tm
    block_expert = jnp.minimum(jnp.sum(ends_p[None, :] <= blk_start[:, None], axis=1), N_EXPERTS - 1).astype(I32)
    sub = tm // MOE_CHUNKS
    rows_left = cnt[block_expert] - (blk_start - start_p[block_expert])
    block_chunks = jnp.clip((rows_left + sub - 1) // sub, 1, MOE_CHUNKS).astype(I32)

    tc = min(256, seq)
    xbuf = dispatch_rows(slot_of.reshape(n // tc, 1, tc * TOP_K), hw, n_slots, tc)
    ybuf = moe_experts(block_expert, nused, block_chunks, xbuf, w_gate_up[l], b_gate_up[l], w_down[l], b_down[l], tm)

    slot3 = slot_of.reshape(n // tc, tc, TOP_K).transpose(0, 2, 1).reshape(n // tc, 1, TOP_K * tc)
    out = combine(slot3, ybuf, top_w, x1, modg2, ln2_g[l][None, :], ln2_b[l][None, :], seq, tc=tc)
    return out.reshape(bsz, seq, d)
```

```python
import functools

import jax
import jax.numpy as jnp
from jax import lax
from jax.experimental import pallas as pl
from jax.experimental.pallas import tpu as pltpu

F32 = jnp.float32
BF16 = jnp.bfloat16
U32 = jnp.uint32
I32 = jnp.int32
HIGHEST = lax.Precision.HIGHEST

D_MODEL = 2048
HEAD_DIM = 128
NSA_HEADS = 8
NSA_KV_HEADS = 2
NSA_GROUP = 4
NSA_WIDTH = 1024
KV_WIDTH = 256
CMP_BLOCK = 32
CMP_STRIDE = 16
CMP_HIDDEN = 256
SEL_BLOCK = 64
SEL_TOPK = 16
SEL_LOCAL = 2
WINDOW = 512
GLA_HEADS = 4
GLA_DV = 256
GLA_DK = 128
GLA_WIDTH = 1024
GLA_GATE_RANK = 16
GLA_GATE_TAU = 16.0
GLA_CHUNK = 64
GLA_SUB = 16
N_EXPERTS = 32
TOP_K = 4
D_FF = 2048
SWIGLU_LIMIT = 7.0
SWIGLU_ALPHA = 1.702
LN_EPS = 1e-5
RMS_EPS = 1e-6
NEG_INF = -1e30
FORCE_SCORE = 1e6
DN_ALPHA = 2.0 ** 0.25
PROJ_WIDTHS = (1024, 256, 256, 256, 256, 256, 256, 24, 512, 512, 1024, 16, 1024)

LANES = 128
VMEM_LIMIT = 56 * 1024 * 1024
WORD_TILES = D_MODEL // (2 * LANES)
GATHER_UNROLL = 8
NSA_TQ = 128
MOE_CHUNKS = 4
P_QNSA = 0
P_VGLA = 1024
P_RGLA = 2048
P_KVCMP = 3072
P_KSEL = 3584
P_VSEL = 3840
P_KWIN = 4096
P_VWIN = 4352
P_QGLA = 4608
P_KGLA = 5120
P_WIDTH = 5632
S_WIDTH = 256


def _cparams(sem, vmem=VMEM_LIMIT):
    return pltpu.CompilerParams(dimension_semantics=sem, vmem_limit_bytes=vmem)


def _sigmoid(x):
    return 1.0 / (1.0 + jnp.exp(-x))


def _ada_kernel(c_ref, w_ref, b_ref, o_ref):
    c = c_ref[...]
    sc = c * _sigmoid(c)
    o_ref[...] = jnp.dot(sc, w_ref[...], preferred_element_type=F32, precision=HIGHEST) + b_ref[...]


def ada_mod(c, w_ada, b_ada, tn=1024):
    bsz, d = c.shape
    n = w_ada.shape[1]
    return pl.pallas_call(
        _ada_kernel,
        grid=(n // tn,),
        in_specs=[pl.BlockSpec((bsz, d), lambda j: (0, 0)),
                  pl.BlockSpec((d, tn), lambda j: (0, j)),
                  pl.BlockSpec((1, tn), lambda j: (0, j))],
        out_specs=pl.BlockSpec((bsz, tn), lambda j: (0, j)),
        out_shape=jax.ShapeDtypeStruct((bsz, n), F32),
        compiler_params=_cparams(("arbitrary",)),
        name="ada_mod",
    )(c, w_ada, b_ada.reshape(1, n))


def _inproj_kernel(x_ref, mod_ref, w_ref, o_ref, hb_ref):
    @pl.when(pl.program_id(1) == 0)
    def _():
        sc = mod_ref[0, 0:1, :]
        sh = mod_ref[0, 1:2, :]
        hb_ref[...] = (x_ref[...] * (1.0 + sc) + sh).astype(BF16)

    o_ref[...] = jnp.dot(hb_ref[...], w_ref[...], preferred_element_type=F32).astype(o_ref.dtype)


def in_proj(x2, mod1, w_bf16, seq, tm=1024, tn=512):
    n, d = x2.shape
    wn = w_bf16.shape[1]
    tm = min(tm, seq)
    per_b = seq // tm
    return pl.pallas_call(
        _inproj_kernel,
        grid=(n // tm, wn // tn),
        in_specs=[pl.BlockSpec((tm, d), lambda i, j: (i, 0)),
                  pl.BlockSpec((1, 2, d), lambda i, j: (i // per_b, 0, 0)),
                  pl.BlockSpec((d, tn), lambda i, j: (0, j))],
        out_specs=pl.BlockSpec((tm, tn), lambda i, j: (i, j)),
        out_shape=jax.ShapeDtypeStruct((n, wn), BF16),
        scratch_shapes=[pltpu.VMEM((tm, d), BF16)],
        compiler_params=_cparams(("arbitrary", "arbitrary")),
        name="in_proj",
    )(x2, mod1, w_bf16)


def _split_weight(w):
    hi = w.astype(BF16)
    lo = (w - hi.astype(F32)).astype(BF16)
    return jnp.concatenate([hi, lo], axis=1)


def _dot_split(a, w2_ref):
    n = w2_ref.shape[1] // 2
    a_hi = a.astype(BF16)
    a_lo = (a - a_hi.astype(F32)).astype(BF16)
    o2 = jnp.dot(a_hi, w2_ref[...], preferred_element_type=F32)
    return o2[:, 0:n] + o2[:, n:2 * n] + jnp.dot(a_lo, w2_ref[:, 0:n], preferred_element_type=F32)


def _inproj_small_kernel(x_ref, mod_ref, w_ref, o_ref):
    sc = mod_ref[0, 0:1, :]
    sh = mod_ref[0, 1:2, :]
    h = x_ref[...] * (1.0 + sc) + sh
    o_ref[...] = _dot_split(h, w_ref)


def in_proj_small(x2, mod1, w_f32, seq, tm=512):
    n, d = x2.shape
    wn = w_f32.shape[1]
    w2 = _split_weight(w_f32)
    tm = min(tm, seq)
    per_b = seq // tm
    return pl.pallas_call(
        _inproj_small_kernel,
        grid=(n // tm,),
        in_specs=[pl.BlockSpec((tm, d), lambda i: (i, 0)),
                  pl.BlockSpec((1, 2, d), lambda i: (i // per_b, 0, 0)),
                  pl.BlockSpec((d, 2 * wn), lambda i: (0, 0))],
        out_specs=pl.BlockSpec((tm, wn), lambda i: (i, 0)),
        out_shape=jax.ShapeDtypeStruct((n, wn), F32),
        compiler_params=_cparams(("arbitrary",)),
        name="in_proj_small",
    )(x2, mod1, w2)


def _gelu_tanh(x):
    return 0.5 * x * (1.0 + jnp.tanh(0.7978845608028654 * (x + 0.044715 * (x * x * x))))


def _compress_kernel(g_ref, pe_ref, w1_ref, b1_ref, w2_ref, o_ref):
    half = CMP_STRIDE * HEAD_DIM
    g = g_ref[0, 0].astype(F32)
    ga = (g + pe_ref[0, 0:1, :]).astype(BF16)
    gb = (g + pe_ref[0, 1:2, :]).astype(BF16)
    u = jnp.dot(ga, w1_ref[0, 0:half, :], preferred_element_type=F32)
    v = jnp.dot(gb, w1_ref[0, half:2 * half, :], preferred_element_type=F32)
    ng = u.shape[0]
    v_next = pltpu.roll(v, ng - 1, 0)
    h = _gelu_tanh(u + v_next + b1_ref[0])
    o_ref[0, 0] = jnp.dot(h.astype(BF16), w2_ref[0], preferred_element_type=F32).astype(o_ref.dtype)


def compress(kvg, pe2, w1, b1, w2):
    bsz, four, ng, gd = kvg.shape
    return pl.pallas_call(
        _compress_kernel,
        grid=(bsz, four),
        in_specs=[pl.BlockSpec((1, 1, ng, gd), lambda b, j: (b, j, 0, 0)),
                  pl.BlockSpec((1, 2, gd), lambda b, j: (j // 2, 0, 0)),
                  pl.BlockSpec((1, 2 * gd, CMP_HIDDEN), lambda b, j: (j // 2, 0, 0)),
                  pl.BlockSpec((1, 1, CMP_HIDDEN), lambda b, j: (j // 2, 0, 0)),
                  pl.BlockSpec((1, CMP_HIDDEN, HEAD_DIM), lambda b, j: (j // 2, 0, 0))],
        out_specs=pl.BlockSpec((1, 1, ng, HEAD_DIM), lambda b, j: (b, j, 0, 0)),
        out_shape=jax.ShapeDtypeStruct((bsz, four, ng, HEAD_DIM), BF16),
        compiler_params=_cparams(("arbitrary", "arbitrary")),
        name="compress",
    )(kvg, pe2, w1, b1, w2)


def _cmp_attn_kernel(q_ref, kc_ref, vct_ref, gate_ref, ovlt_ref, o_ref, imp_ref, *, tq):
    i = pl.program_id(1)
    ncp = kc_ref.shape[2]
    R = NSA_GROUP
    D = HEAD_DIM
    scale = D ** -0.5
    q0 = i * tq
    crow = lax.broadcasted_iota(I32, (ncp, 1), 0)
    pos_col = q0 + jnp.concatenate([lax.broadcasted_iota(I32, (1, tq), 1)] * R, axis=1)
    ok = (crow * CMP_STRIDE + (CMP_BLOCK - 1)) <= pos_col
    any_ok = (pos_col >= CMP_BLOCK - 1).astype(F32)
    a = (crow - q0 // CMP_STRIDE).astype(F32)
    a_hi = jnp.floor(a * (1.0 / 32.0))
    a_lo = a - 32.0 * a_hi
    lane = lax.broadcasted_iota(I32, (1, LANES), 1)
    sub = lax.broadcasted_iota(I32, (LANES, 1), 0)
    kaug = jnp.where(lane == 0, a_hi, jnp.where(lane == 1, a_lo, 0.0)).astype(BF16)
    gates_t = _sigmoid(gate_ref[...]).T
    q = q_ref[...]
    for g in range(NSA_KV_HEADS):
        qa_cols = []
        for r in range(R):
            h = g * R + r
            slope = 2.0 ** (-(h + 1))
            qs_t = (q[:, h * D:(h + 1) * D].astype(F32) * scale).T.astype(BF16)
            aug = jnp.where(sub == 0, 32.0 * CMP_STRIDE * slope, jnp.where(sub == 1, CMP_STRIDE * slope, 0.0))
            qa_cols.append(jnp.concatenate([qs_t, jnp.broadcast_to(aug, (D, tq)).astype(BF16)], axis=0))
        qa = jnp.concatenate(qa_cols, axis=1)
        ka = jnp.concatenate([kc_ref[0, g], kaug], axis=1)
        st = jnp.dot(ka, qa, preferred_element_type=F32)
        st = jnp.where(ok, st, NEG_INF)
        m = jnp.max(st, axis=0, keepdims=True)
        e = jnp.exp(st - m)
        p = e * (any_ok / jnp.sum(e, axis=0, keepdims=True))
        o_t = jnp.dot(vct_ref[0, g], p.astype(BF16), preferred_element_type=F32)
        psum_t = p[:, 0:tq]
        for r in range(1, R):
            psum_t = psum_t + p[:, r * tq:(r + 1) * tq]
        for r in range(R):
            h = g * R + r
            o_ref[g, :, r * tq:(r + 1) * tq] = o_t[:, r * tq:(r + 1) * tq] * gates_t[3 * h:3 * h + 1, :]
        imp_ref[g] = jnp.dot(ovlt_ref[...], psum_t, preferred_element_type=F32, precision=HIGHEST)


def cmp_attention(p_buf, kvc, s_buf, ovl, bsz, seq, tq=128):
    ncp = kvc.shape[2]
    nsel = ovl.shape[1]
    per_b = seq // tq
    G, D, C = NSA_KV_HEADS, HEAD_DIM, NSA_GROUP * tq
    kc = kvc[:, 0:G]
    vct = kvc[:, G:2 * G].transpose(0, 1, 3, 2)
    return pl.pallas_call(
        functools.partial(_cmp_attn_kernel, tq=tq),
        grid=(bsz, per_b),
        in_specs=[pl.BlockSpec((tq, NSA_WIDTH), lambda b, i: (b * per_b + i, P_QNSA // NSA_WIDTH)),
                  pl.BlockSpec((1, G, ncp, D), lambda b, i: (b, 0, 0, 0)),
                  pl.BlockSpec((1, G, D, ncp), lambda b, i: (b, 0, 0, 0)),
                  pl.BlockSpec((tq, LANES), lambda b, i: (b * per_b + i, 0)),
                  pl.BlockSpec((nsel, ncp), lambda b, i: (0, 0))],
        out_specs=[pl.BlockSpec((None, G, None, D, C), lambda b, i: (b, 0, i, 0, 0)),
                   pl.BlockSpec((None, G, nsel, tq), lambda b, i: (b, 0, 0, i))],
        out_shape=[jax.ShapeDtypeStruct((bsz, G, per_b, D, C), F32),
                   jax.ShapeDtypeStruct((bsz, G, nsel, seq), F32)],
        compiler_params=_cparams(("arbitrary", "arbitrary")),
        name="cmp_attention",
    )(p_buf, kc, vct, s_buf, ovl.T)


def _topk_kernel(imp_ref, o_ref, any_ref, *, n_topk, qtile):
    nsel, cols = imp_ref.shape
    pos = pl.program_id(1) * cols + lax.broadcasted_iota(I32, (1, cols), 1)
    cur = pos // SEL_BLOCK
    blk = lax.broadcasted_iota(I32, (nsel, 1), 0)
    blkf = blk.astype(F32)
    forced = jnp.logical_or(blk == 0, jnp.logical_and(blk <= cur, blk > cur - SEL_LOCAL))
    work = jnp.where(forced, FORCE_SCORE, jnp.where(blk > cur, -1.0, imp_ref[...]))
    sel = jnp.zeros((nsel, cols), F32)
    for _ in range(n_topk):
        m = jnp.max(work, axis=0, keepdims=True)
        first = jnp.min(jnp.where(work == m, blkf, float(nsel)), axis=0, keepdims=True)
        pick = blkf == first
        sel = jnp.where(pick, 1.0, sel)
        work = jnp.where(pick, -2.0, work)
    keep = jnp.logical_and(sel > 0.5, blk <= cur)
    res = jnp.where(keep, 0.0, NEG_INF)
    if nsel < LANES:
        res = jnp.concatenate([res, jnp.full((LANES - nsel, cols), NEG_INF, F32)], axis=0)
    o_ref[...] = res.astype(o_ref.dtype)
    hit = jnp.where(res == 0.0, 1.0, 0.0)
    lane = lax.broadcasted_iota(I32, (1, LANES), 1)
    out = jnp.zeros((LANES, LANES), F32)
    for qt in range(cols // qtile):
        col = jnp.max(hit[:, qt * qtile:(qt + 1) * qtile], axis=1, keepdims=True)
        out = jnp.where(lane == qt, col, out)
    any_ref[...] = out


def topk_select(imp_t, qtile, cols=1024):
    nbg, nsel, seq = imp_t.shape
    cols = min(cols, seq)
    nstep = seq // cols
    mask_t, hits = pl.pallas_call(
        functools.partial(_topk_kernel, n_topk=min(SEL_TOPK, nsel), qtile=qtile),
        grid=(nbg, nstep),
        in_specs=[pl.BlockSpec((None, nsel, cols), lambda b, j: (b, 0, j))],
        out_specs=[pl.BlockSpec((None, LANES, cols), lambda b, j: (b, 0, j)),
                   pl.BlockSpec((None, None, LANES, LANES), lambda b, j: (b, j, 0, 0))],
        out_shape=[jax.ShapeDtypeStruct((nbg, LANES, seq), BF16),
                   jax.ShapeDtypeStruct((nbg, nstep, LANES, LANES), F32)],
        compiler_params=_cparams(("arbitrary", "arbitrary")),
        name="topk_select",
    )(imp_t)
    per_step = cols // qtile
    blk_any = hits[:, :, :, :per_step].transpose(0, 1, 3, 2).reshape(nbg * nstep * per_step, LANES)
    return mask_t, blk_any


def _sel_win_kernel(ids_ref, cnt_ref, q_ref, ks_ref, vst_ref, kw_ref, vwt_ref, sb_ref, kau_ref, kad_ref, oc_ref,
                    gate_ref, o_ref, qau_ref, qad_ref, qaw_ref, acc_ref, s_ref, mt_ref, *, tq, tk, seq):
    g = pl.program_id(1)
    i = pl.program_id(2)
    q0 = i * tq
    scale = HEAD_DIM ** -0.5
    R = NSA_GROUP
    C = R * tq
    D = HEAD_DIM
    dn = (((1,), (1,)), ((), ()))
    gslope = jnp.where(g == 0, 1.0, 2.0 ** (-NSA_GROUP))
    slopes = [gslope * 2.0 ** (-(r + 1)) for r in range(R)]

    qrel_col = jnp.concatenate([lax.broadcasted_iota(I32, (1, tq), 1)] * R, axis=1)
    slope_col = jnp.concatenate([jnp.full((1, tq), 1.0, F32) * slopes[r] for r in range(R)], axis=1)

    q = q_ref[...]
    selb_t = sb_ref[...].astype(F32)
    lane = lax.broadcasted_iota(I32, (1, LANES), 1)
    sub = lax.broadcasted_iota(I32, (LANES, 1), 0)
    nrel = ((sub - q0 // SEL_BLOCK) * SEL_BLOCK).astype(F32)
    for r in range(R):
        cols = slice(r * tq, (r + 1) * tq)
        qs_t = (q[:, r * D:(r + 1) * D].astype(F32) * scale).T.astype(BF16)
        sb_t = selb_t + slopes[r] * nrel
        qau_ref[0:D, cols] = qs_t
        qau_ref[D:2 * D, cols] = jnp.where(sub == LANES - 1, slopes[r], sb_t).astype(BF16)
        qad_ref[0:D, cols] = qs_t
        qad_ref[D:2 * D, cols] = sb_t.astype(BF16)
        qaw_ref[0:D, cols] = qs_t
        aw = jnp.where(sub == 0, slopes[r] * SEL_BLOCK, jnp.where(sub == 1, slopes[r], 0.0))
        qaw_ref[D:2 * D, cols] = jnp.broadcast_to(aw, (D, tq)).astype(BF16)

    acc_ref[...] = jnp.zeros_like(acc_ref)

    def online_update(st, t, m, l):
        m_new = jnp.maximum(m, jnp.max(st, axis=0, keepdims=True))
        alpha = jnp.exp(m - m_new)
        p = jnp.exp(st - m_new)
        l_new = alpha * l + jnp.sum(p, axis=0, keepdims=True)
        pv = jnp.dot(vst_ref[t], p.astype(BF16), preferred_element_type=F32)
        acc_ref[...] = acc_ref[...] * alpha + pv
        return m_new, l_new

    S_SPARE, S_NEGINF = 2, 3

    @pl.when(jnp.logical_and(pl.program_id(0) == 0, jnp.logical_and(g == 0, i == 0)))
    def _():
        s_ref[S_NEGINF] = jnp.full((tk, C), -jnp.inf, F32)

    t_last = q0 // tk
    step = (pl.program_id(0) * pl.num_programs(1) + g) * pl.num_programs(2) + i
    cnt = cnt_ref[step]
    id0 = step * (seq // tk)

    def pipelined(u, carry):
        m, l = carry
        tb = ids_ref[id0 + jnp.maximum(u - 1, 0)]
        st = s_ref[jnp.where(u >= 1, (u + 1) % 2, S_NEGINF)]
        m_new = jnp.maximum(m, mt_ref[0:1, :])
        alpha = jnp.exp(m - m_new)
        p = jnp.exp(st - m_new)
        l_new = alpha * l + jnp.sum(p, axis=0, keepdims=True)
        pv = jnp.dot(vst_ref[tb], p.astype(BF16), preferred_element_type=F32)
        acc_ref[...] = acc_ref[...] * alpha + pv
        ta = ids_ref[id0 + jnp.minimum(u, cnt - 1)]
        k0 = pl.multiple_of(ta * tk, tk)
        ka = jnp.concatenate([ks_ref[pl.ds(k0, tk), :], kau_ref[pl.ds(k0, tk), :]], axis=1)
        sa = jnp.dot(ka, qau_ref[...], preferred_element_type=F32)
        s_ref[jnp.where(u < cnt, u % 2, S_SPARE)] = sa
        mt_ref[...] = jnp.broadcast_to(jnp.max(sa, axis=0, keepdims=True), mt_ref.shape)
        return m_new, l_new

    m0 = jnp.full((1, C), NEG_INF, F32)
    l0 = jnp.zeros((1, C), F32)
    mt_ref[...] = jnp.full(mt_ref.shape, -jnp.inf, F32)
    m, l = lax.fori_loop(0, jnp.where(cnt > 0, cnt + 1, 0), pipelined, (m0, l0))

    k0 = pl.multiple_of(t_last * tk, tk)
    ka = jnp.concatenate([ks_ref[pl.ds(k0, tk), :], kad_ref[pl.ds(k0, tk), :]], axis=1)
    st = jnp.dot(ka, qad_ref[...], preferred_element_type=F32)
    krow = lax.broadcasted_iota(I32, (tk, 1), 0)
    koff = (krow % SEL_BLOCK).astype(F32)
    st = st + koff * slope_col
    st = jnp.where(krow + (k0 - q0) <= qrel_col, st, NEG_INF)
    m, l = online_update(st, t_last, m, l)
    inv_l = 1.0 / l

    wk = WINDOW + tq
    w0 = pl.multiple_of(jnp.maximum(q0 - WINDOW, 0), tq)
    wrow = lax.broadcasted_iota(I32, (wk, 1), 0) + (w0 - q0)
    w_hi = jnp.floor(wrow.astype(F32) * (1.0 / SEL_BLOCK))
    w_lo = wrow.astype(F32) - w_hi * SEL_BLOCK
    kaw = jnp.where(lane == 0, w_hi, jnp.where(lane == 1, w_lo, 0.0)).astype(BF16)
    kwa = jnp.concatenate([kw_ref[pl.ds(w0, wk), :], kaw], axis=1)
    sw = jnp.dot(kwa, qaw_ref[...], preferred_element_type=F32)
    dist = qrel_col - wrow
    sw = jnp.where(jnp.logical_and(dist >= 0, dist < WINDOW), sw, NEG_INF)
    mw = jnp.max(sw, axis=0, keepdims=True)
    pw = jnp.exp(sw - mw)
    inv_lw = 1.0 / jnp.sum(pw, axis=0, keepdims=True)
    c0 = w0 // tq
    vwt = jnp.concatenate([vwt_ref[c0 + c] for c in range(wk // tq)], axis=1)
    ow = jnp.dot(vwt, pw.astype(BF16), preferred_element_type=F32)

    gates_t = _sigmoid(gate_ref[...]).T
    osel = acc_ref[...] * inv_l
    ow = ow * inv_lw
    for r in range(R):
        cols = slice(r * tq, (r + 1) * tq)
        g_sel = jnp.where(g == 0, gates_t[3 * r + 1:3 * r + 2, :], gates_t[3 * (R + r) + 1:3 * (R + r) + 2, :])
        g_win = jnp.where(g == 0, gates_t[3 * r + 2:3 * r + 3, :], gates_t[3 * (R + r) + 2:3 * (R + r) + 3, :])
        y_t = oc_ref[:, cols] + g_sel * osel[:, cols] + g_win * ow[:, cols]
        y = y_t.T
        o_ref[:, r * D:(r + 1) * D] = y.astype(o_ref.dtype)


def _key_side_constants(seq):
    nsel = seq // SEL_BLOCK
    key = jnp.arange(seq)
    blk = key // SEL_BLOCK
    lane = jnp.arange(LANES)[None, :]
    onehot = (lane == blk[:, None])
    kad = onehot.astype(BF16)
    kau = jnp.where(lane == LANES - 1, (key % SEL_BLOCK)[:, None].astype(F32),
                    jnp.logical_and(onehot, lane < nsel - SEL_LOCAL).astype(F32)).astype(BF16)
    return kau, kad


def _active_key_tiles(blk_any, seq, tq, tk):
    nt = seq // tk
    per_b = seq // tq
    bpt = tk // SEL_BLOCK
    tile_any = blk_any[:, :nt * bpt].reshape(-1, nt, bpt).max(axis=-1) > 0.5
    t_last = (jnp.arange(blk_any.shape[0], dtype=I32) % per_b) * tq // tk
    tile_idx = jnp.arange(nt, dtype=I32)[None, :]
    active = jnp.logical_and(tile_any, tile_idx < t_last[:, None])
    ids = jnp.argsort(jnp.where(active, tile_idx, nt + tile_idx), axis=-1).astype(I32)
    return ids.reshape(-1), active.sum(axis=-1).astype(I32)


def sel_win_attention(p_buf, selb, blk_any, ocmp, s_buf, bsz, seq, tq=128, tk=512):
    n = p_buf.shape[0]
    tk = min(tk, seq)
    per_b = seq // tq
    gw = NSA_GROUP * HEAD_DIM
    G = NSA_KV_HEADS
    D = HEAD_DIM
    kau, kad = _key_side_constants(seq)
    if blk_any.shape[1] < (seq // tk) * (tk // SEL_BLOCK):
        blk_any = jnp.pad(blk_any, ((0, 0), (0, (seq // tk) * (tk // SEL_BLOCK) - blk_any.shape[1])))
    tile_ids, tile_cnt = _active_key_tiles(blk_any, seq, tq, tk)
    vsel = p_buf[:, P_VSEL:P_VSEL + KV_WIDTH].reshape(bsz, seq // tk, tk, G, D).transpose(0, 3, 1, 4, 2)
    vwin = p_buf[:, P_VWIN:P_VWIN + KV_WIDTH].reshape(bsz, seq // tq, tq, G, D).transpose(0, 3, 1, 4, 2)

    def colblk(off):
        return lambda b, g, i, ids, cnt: (b, off // D + g)

    return pl.pallas_call(
        functools.partial(_sel_win_kernel, tq=tq, tk=tk, seq=seq),
        grid_spec=pltpu.PrefetchScalarGridSpec(
            num_scalar_prefetch=2,
            grid=(bsz, G, per_b),
            in_specs=[pl.BlockSpec((tq, gw), lambda b, g, i, ids, cnt: (b * per_b + i, P_QNSA // gw + g)),
                      pl.BlockSpec((seq, D), colblk(P_KSEL)),
                      pl.BlockSpec((None, None, seq // tk, D, tk), lambda b, g, i, ids, cnt: (b, g, 0, 0, 0)),
                      pl.BlockSpec((seq, D), colblk(P_KWIN)),
                      pl.BlockSpec((None, None, seq // tq, D, tq), lambda b, g, i, ids, cnt: (b, g, 0, 0, 0)),
                      pl.BlockSpec((None, LANES, tq), lambda b, g, i, ids, cnt: (b * G + g, 0, i)),
                      pl.BlockSpec((seq, LANES), lambda b, g, i, ids, cnt: (0, 0)),
                      pl.BlockSpec((seq, LANES), lambda b, g, i, ids, cnt: (0, 0)),
                      pl.BlockSpec((None, None, None, D, gw), lambda b, g, i, ids, cnt: (b, g, i, 0, 0)),
                      pl.BlockSpec((tq, LANES), lambda b, g, i, ids, cnt: (b * per_b + i, 0))],
            out_specs=pl.BlockSpec((tq, gw), lambda b, g, i, ids, cnt: (b * per_b + i, g)),
            scratch_shapes=[pltpu.VMEM((2 * D, NSA_GROUP * tq), BF16),
                            pltpu.VMEM((2 * D, NSA_GROUP * tq), BF16),
                            pltpu.VMEM((2 * D, NSA_GROUP * tq), BF16),
                            pltpu.VMEM((D, NSA_GROUP * tq), F32),
                            pltpu.VMEM((4, tk, NSA_GROUP * tq), F32),
                            pltpu.VMEM((8, NSA_GROUP * tq), F32)]),
        out_shape=jax.ShapeDtypeStruct((n, NSA_WIDTH), BF16),
        compiler_params=_cparams(("arbitrary", "arbitrary", "arbitrary")),
        name="sel_win_attention",
    )(tile_ids, tile_cnt, p_buf, p_buf, vsel, p_buf, vwin, selb, kau, kad, ocmp, s_buf)


def _gla_kernel(q_ref, k_ref, v_ref, r_ref, a_ref, wg_ref, bg_ref, nw_ref, tril_ref, o_ref,
                st_ref, b_ref, *, tc):
    C, SUB, H = GLA_CHUNK, GLA_SUB, GLA_HEADS

    @pl.when(pl.program_id(1) == 0)
    def _():
        st_ref[...] = jnp.zeros_like(st_ref)

    x = jnp.dot(a_ref[...], wg_ref[...], preferred_element_type=F32, precision=HIGHEST) + bg_ref[...]
    la = (jnp.minimum(x, 0.0) - jnp.log(1.0 + jnp.exp(-jnp.abs(x)))) * (1.0 / GLA_GATE_TAU)
    tril = tril_ref[...]
    for c in range(tc // C):
        b_ref[c * C:(c + 1) * C, :] = jnp.dot(tril, la[c * C:(c + 1) * C, :],
                                              preferred_element_type=F32, precision=HIGHEST)

    row16 = lax.broadcasted_iota(I32, (SUB, C), 0)
    lane64 = lax.broadcasted_iota(I32, (SUB, C), 1)
    nw = nw_ref[...]

    def chunk(c, carry):
        r0 = pl.multiple_of(c * C, C)
        for h in range(H):
            bh = b_ref[pl.ds(r0, C), h * GLA_DK:(h + 1) * GLA_DK]
            qc = q_ref[pl.ds(r0, C), h * GLA_DK:(h + 1) * GLA_DK].astype(F32) * (GLA_DK ** -0.5)
            kc = k_ref[pl.ds(r0, C), h * GLA_DK:(h + 1) * GLA_DK].astype(F32)
            vc = v_ref[pl.ds(r0, C), h * GLA_DV:(h + 1) * GLA_DV]
            st = st_ref[h]
            b_last = bh[C - 1:C, :]
            qe = (qc * jnp.exp(bh)).astype(BF16)
            o = lax.dot_general(qe, st.astype(BF16), (((1,), (1,)), ((), ())),
                                preferred_element_type=F32)
            a_rows = []
            for sb in range(C // SUB):
                s0 = sb * SUB
                bi = bh[s0:s0 + SUB, :]
                qi = qc[s0:s0 + SUB, :]
                ki = kc[s0:s0 + SUB, :]
                beta = bh[s0:s0 + 1, :]
                if sb > 0:
                    qd = (qi * jnp.exp(bi - beta)).astype(BF16)
                    kd = (kc * jnp.exp(jnp.minimum(beta - bh, 0.0))).astype(BF16)
                    a_i = lax.dot_general(qd, kd, (((1,), (1,)), ((), ())), preferred_element_type=F32)
                    a_i = jnp.where(lane64 < s0, a_i, 0.0)
                else:
                    a_i = jnp.zeros((SUB, C), F32)
                for s in range(SUB):
                    e = jnp.exp(jnp.minimum(bi - bi[s:s + 1, :], 0.0))
                    col = jnp.sum(qi * ki[s:s + 1, :] * e, axis=-1, keepdims=True)
                    a_i = jnp.where(jnp.logical_and(lane64 == s0 + s, row16 >= s), col, a_i)
                a_rows.append(a_i)
            a_mat = jnp.concatenate(a_rows, axis=0)
            o = o + jnp.dot(a_mat.astype(BF16), vc, preferred_element_type=F32)
            kdec = (kc * jnp.exp(b_last - bh)).astype(BF16)
            upd = lax.dot_general(vc, kdec, (((0,), (0,)), ((), ())), preferred_element_type=F32)
            st_ref[h] = st * jnp.exp(b_last) + upd
            rms = lax.rsqrt(jnp.mean(o * o, axis=-1, keepdims=True) + RMS_EPS)
            rr = r_ref[pl.ds(r0, C), h * GLA_DV:(h + 1) * GLA_DV].astype(F32)
            y = o * rms * nw * (rr * _sigmoid(rr))
            o_ref[pl.ds(r0, C), h * GLA_DV:(h + 1) * GLA_DV] = y.astype(o_ref.dtype)
        return carry

    lax.fori_loop(0, tc // C, chunk, 0)


def gla_attention(p_buf, s_buf, wg_pad, bg, nw, tril, bsz, seq, tc=512):
    n = p_buf.shape[0]
    tc = min(tc, seq)
    per_b = seq // tc
    kw = GLA_HEADS * GLA_DK
    return pl.pallas_call(
        functools.partial(_gla_kernel, tc=tc),
        grid=(bsz, per_b),
        in_specs=[pl.BlockSpec((tc, kw), lambda b, j: (b * per_b + j, P_QGLA // kw)),
                  pl.BlockSpec((tc, kw), lambda b, j: (b * per_b + j, P_KGLA // kw)),
                  pl.BlockSpec((tc, GLA_WIDTH), lambda b, j: (b * per_b + j, P_VGLA // GLA_WIDTH)),
                  pl.BlockSpec((tc, GLA_WIDTH), lambda b, j: (b * per_b + j, P_RGLA // GLA_WIDTH)),
                  pl.BlockSpec((tc, LANES), lambda b, j: (b * per_b + j, 1)),
                  pl.BlockSpec((LANES, kw), lambda b, j: (0, 0)),
                  pl.BlockSpec((1, kw), lambda b, j: (0, 0)),
                  pl.BlockSpec((1, GLA_DV), lambda b, j: (0, 0)),
                  pl.BlockSpec((GLA_CHUNK, GLA_CHUNK), lambda b, j: (0, 0))],
        out_specs=pl.BlockSpec((tc, GLA_WIDTH), lambda b, j: (b * per_b + j, 0)),
        out_shape=jax.ShapeDtypeStruct((n, GLA_WIDTH), BF16),
        scratch_shapes=[pltpu.VMEM((GLA_HEADS, GLA_DV, GLA_DK), F32),
                        pltpu.VMEM((tc, kw), F32)],
        compiler_params=_cparams(("arbitrary", "arbitrary")),
        name="gla_attention",
    )(p_buf, p_buf, p_buf, p_buf, s_buf, wg_pad, bg, nw, tril)


def _pack_words(lo, hi):
    lo_b = pltpu.bitcast(lo.astype(BF16).astype(F32), U32)
    hi_b = pltpu.bitcast(hi.astype(BF16).astype(F32), U32)
    return jnp.bitwise_or(hi_b, jnp.right_shift(lo_b, jnp.uint32(16)))


def _unpack_words(w):
    lo = pltpu.bitcast(jnp.left_shift(w, jnp.uint32(16)), F32)
    hi = pltpu.bitcast(jnp.bitwise_and(w, jnp.uint32(0xFFFF0000)), F32)
    return lo, hi


def _store_packed(ref, v, first_tile=0):
    rows = v.shape[0]
    n_tiles = ref.shape[0] // rows
    for u in range(v.shape[1] // (2 * LANES)):
        w = _pack_words(v[:, 2 * u * LANES:(2 * u + 1) * LANES], v[:, (2 * u + 1) * LANES:(2 * u + 2) * LANES])
        ref[pl.ds(first_tile + u, rows, stride=n_tiles), :] = w


def _load_word_tile(ref, s, rows, base=0):
    n_tiles = WORD_TILES
    return ref[pl.ds(base * n_tiles + s, rows, stride=n_tiles), :]


def _layer_norm(z, g, b):
    mu = jnp.mean(z, axis=-1, keepdims=True)
    zc = z - mu
    var = jnp.mean(zc * zc, axis=-1, keepdims=True)
    return zc * lax.rsqrt(var + LN_EPS) * g + b


def _outproj_kernel(yn_ref, yg_ref, wo_ref, x_ref, mod_ref, lng_ref, lnb_ref, wr_ref, br_ref, ltri_ref,
                    x1_ref, hw_ref, idx_ref, tw_ref, rank_ref, cnt_ref, carry_ref):
    step = pl.program_id(0)

    @pl.when(step == 0)
    def _():
        carry_ref[...] = jnp.zeros_like(carry_ref)

    half = yn_ref.shape[1]
    mix = (jnp.dot(yn_ref[...], wo_ref[0:half, :], preferred_element_type=F32) +
           jnp.dot(yg_ref[...], wo_ref[half:2 * half, :], preferred_element_type=F32))
    g1 = mod_ref[0, 0:1, :]
    sc2 = mod_ref[0, 1:2, :]
    sh2 = mod_ref[0, 2:3, :]
    x1 = _layer_norm(DN_ALPHA * x_ref[...] + g1 * mix, lng_ref[...], lnb_ref[...])
    x1_ref[...] = x1
    h2 = x1 * (1.0 + sc2) + sh2
    _store_packed(hw_ref, h2)

    tm = h2.shape[0]
    ne = br_ref.shape[1]
    logits = _dot_split(h2, wr_ref)[:, 0:ne] + br_ref[...]
    lane = lax.broadcasted_iota(I32, (1, ne), 1).astype(F32)
    out_lane = lax.broadcasted_iota(I32, (1, LANES), 1)
    work = logits
    idx_out = jnp.zeros((tm, LANES), F32)
    w_out = jnp.zeros((tm, LANES), F32)
    onehot = jnp.zeros((tm, ne), F32)
    picks = []
    top0 = None
    den = jnp.zeros((tm, 1), F32)
    for k in range(TOP_K):
        m = jnp.max(work, axis=-1, keepdims=True)
        first = jnp.min(jnp.where(work == m, lane, float(ne)), axis=-1, keepdims=True)
        pick = lane == first
        if k == 0:
            top0 = m
        e = jnp.exp(m - top0)
        den = den + e
        idx_out = jnp.where(out_lane == k, first, idx_out)
        w_out = jnp.where(out_lane == k, e, w_out)
        onehot = jnp.where(pick, 1.0, onehot)
        picks.append(pick)
        work = jnp.where(pick, -jnp.inf, work)
    idx_ref[...] = idx_out.astype(I32)
    tw_ref[...] = w_out * (1.0 / den)

    before = jnp.dot(ltri_ref[...], onehot.astype(BF16), preferred_element_type=F32) + carry_ref[0:1, :]
    rank_out = jnp.zeros((tm, LANES), F32)
    for k in range(TOP_K):
        rk = jnp.sum(jnp.where(picks[k], before, 0.0), axis=-1, keepdims=True)
        rank_out = jnp.where(out_lane == k, rk, rank_out)
    rank_ref[...] = rank_out.astype(I32)
    carry_ref[...] = carry_ref[...] + jnp.sum(onehot, axis=0, keepdims=True)
    cnt_ref[...] = carry_ref[...].astype(I32)


def out_proj_router(ynsa, ygla, wo_bf16, x2, mod2, ln_g, ln_b, w_router, b_router, ltri, seq, tm=512):
    n, d = x2.shape
    tm = min(tm, seq)
    per_b = seq // tm
    ne = w_router.shape[1]
    wr2 = _split_weight(jnp.pad(w_router, ((0, 0), (0, LANES - ne))))
    row = lambda i: (i, 0)
    const = lambda i: (0, 0)
    return pl.pallas_call(
        _outproj_kernel,
        grid=(n // tm,),
        in_specs=[pl.BlockSpec((tm, NSA_WIDTH), row),
                  pl.BlockSpec((tm, GLA_WIDTH), row),
                  pl.BlockSpec((d, d), const),
                  pl.BlockSpec((tm, d), row),
                  pl.BlockSpec((1, 3, d), lambda i: (i // per_b, 0, 0)),
                  pl.BlockSpec((1, d), const),
                  pl.BlockSpec((1, d), const),
                  pl.BlockSpec((d, 2 * LANES), const),
                  pl.BlockSpec((1, ne), const),
                  pl.BlockSpec((tm, tm), const)],
        out_specs=[pl.BlockSpec((tm, d), row),
                   pl.BlockSpec((tm * WORD_TILES, LANES), row),
                   pl.BlockSpec((tm, LANES), row),
                   pl.BlockSpec((tm, LANES), row),
                   pl.BlockSpec((tm, LANES), row),
                   pl.BlockSpec((8, ne), const)],
        out_shape=[jax.ShapeDtypeStruct((n, d), F32),
                   jax.ShapeDtypeStruct((n * WORD_TILES, LANES), U32),
                   jax.ShapeDtypeStruct((n, LANES), I32),
                   jax.ShapeDtypeStruct((n, LANES), F32),
                   jax.ShapeDtypeStruct((n, LANES), I32),
                   jax.ShapeDtypeStruct((8, ne), I32)],
        scratch_shapes=[pltpu.VMEM((8, ne), F32)],
        compiler_params=_cparams(("arbitrary",)),
        name="out_proj_router",
    )(ynsa, ygla, wo_bf16, x2, mod2, ln_g, ln_b, wr2, b_router, ltri)


def _issue_token_copies(idx_at, src_ref, dst_ref, sem, count):
    wt = WORD_TILES

    def issue(r8, carry):
        for u in range(GATHER_UNROLL):
            r = r8 * GATHER_UNROLL + u
            tok = idx_at(r)
            pltpu.make_async_copy(src_ref.at[pl.ds(pl.multiple_of(tok * wt, wt), wt)],
                                  dst_ref.at[pl.ds(pl.multiple_of(r * wt, wt), wt)], sem).start(priority=u % 2)
        return carry

    lax.fori_loop(0, count // GATHER_UNROLL, issue, 0)


def _dispatch_kernel(slot_ref, src_ref, zero_ref, o_ref, sem, *, tc):
    del zero_ref
    wt = WORD_TILES

    def issue(t8, carry):
        for u in range(GATHER_UNROLL // TOP_K):
            t = t8 * (GATHER_UNROLL // TOP_K) + u
            for k in range(TOP_K):
                slot = slot_ref[0, 0, t * TOP_K + k]
                pltpu.make_async_copy(src_ref.at[pl.ds(pl.multiple_of(t * wt, wt), wt)],
                                      o_ref.at[pl.ds(pl.multiple_of(slot * wt, wt), wt)], sem).start(priority=k % 2)
        return carry

    lax.fori_loop(0, tc * TOP_K // GATHER_UNROLL, issue, 0)
    for k in range(TOP_K):
        pltpu.make_async_copy(src_ref, o_ref.at[pl.ds(0, tc * wt)], sem).wait()


def dispatch_rows(slot3, src, n_slots, tc):
    wt = WORD_TILES
    n = src.shape[0] // wt
    zeros = jnp.zeros((n_slots * wt, LANES), src.dtype)
    return pl.pallas_call(
        functools.partial(_dispatch_kernel, tc=tc),
        grid=(n // tc,),
        in_specs=[pl.BlockSpec((1, 1, tc * TOP_K), lambda i: (i, 0, 0), memory_space=pltpu.SMEM),
                  pl.BlockSpec((tc * wt, LANES), lambda i: (i, 0)),
                  pl.BlockSpec(memory_space=pl.ANY)],
        out_specs=pl.BlockSpec(memory_space=pl.ANY),
        out_shape=jax.ShapeDtypeStruct((n_slots * wt, LANES), src.dtype),
        scratch_shapes=[pltpu.SemaphoreType.DMA(())],
        input_output_aliases={2: 0},
        compiler_params=_cparams(("arbitrary",)),
        name="dispatch_scatter",
    )(slot3, src, zeros)


def _moe_kernel(be_ref, nused_ref, nchunk_ref, x_ref, wg_ref, wu_ref, bg_ref, bu_ref, wd_ref, bd_ref, o_ref,
                xb_ref, acc_ref, *, tm):
    i = pl.program_id(0)
    j = pl.program_id(1)
    nj = pl.num_programs(1)
    used = i < nused_ref[0]
    sub = tm // MOE_CHUNKS

    @pl.when(jnp.logical_and(used, j == 0))
    def _():
        for s in range(WORD_TILES):
            lo, hi = _unpack_words(_load_word_tile(x_ref, s, tm))
            xb_ref[:, 2 * s * LANES:(2 * s + 1) * LANES] = lo.astype(BF16)
            xb_ref[:, (2 * s + 1) * LANES:(2 * s + 2) * LANES] = hi.astype(BF16)
        acc_ref[...] = jnp.zeros_like(acc_ref)

    for c in range(1, MOE_CHUNKS + 1):
        @pl.when(jnp.logical_and(used, nchunk_ref[i] == c))
        def _(c=c):
            xb = xb_ref[0:c * sub, :]
            gt = jnp.dot(xb, wg_ref[0].astype(BF16), preferred_element_type=F32) + bg_ref[0]
            up = jnp.dot(xb, wu_ref[0].astype(BF16), preferred_element_type=F32) + bu_ref[0]
            gt = jnp.minimum(gt, SWIGLU_LIMIT)
            up = jnp.clip(up, -SWIGLU_LIMIT, SWIGLU_LIMIT)
            act = (up + 1.0) * gt * _sigmoid(SWIGLU_ALPHA * gt)
            acc_ref[0:c * sub, :] += jnp.dot(act.astype(BF16), wd_ref[0].astype(BF16),
                                             preferred_element_type=F32)

    @pl.when(jnp.logical_and(used, j == nj - 1))
    def _():
        _store_packed(o_ref, acc_ref[...] + bd_ref[0])

    @pl.when(jnp.logical_and(jnp.logical_not(used), j == nj - 1))
    def _():
        o_ref[...] = jnp.zeros_like(o_ref)


def moe_experts(block_expert, nused, block_chunks, xbuf, w_gate_up, b_gate_up, w_down, b_down, tm, tf=256):
    wt = WORD_TILES
    nslots = xbuf.shape[0] // wt
    ne, d, f2 = w_gate_up.shape
    f = f2 // 2
    tf = min(tf, f)
    nfj = f // tf
    nblk = nslots // tm
    b_gu3 = b_gate_up.reshape(ne, 1, f2)
    b_d3 = b_down.reshape(ne, 1, d)

    def jj(i, j, nu):
        return jnp.where(i < nu[0], j, nfj - 1)

    return pl.pallas_call(
        functools.partial(_moe_kernel, tm=tm),
        grid_spec=pltpu.PrefetchScalarGridSpec(
            num_scalar_prefetch=3,
            grid=(nblk, nfj),
            in_specs=[pl.BlockSpec((tm * wt, LANES), lambda i, j, be, nu, nc: (i, 0)),
                      pl.BlockSpec((1, d, tf), lambda i, j, be, nu, nc: (be[i], 0, jj(i, j, nu))),
                      pl.BlockSpec((1, d, tf), lambda i, j, be, nu, nc: (be[i], 0, nfj + jj(i, j, nu))),
                      pl.BlockSpec((1, 1, tf), lambda i, j, be, nu, nc: (be[i], 0, jj(i, j, nu))),
                      pl.BlockSpec((1, 1, tf), lambda i, j, be, nu, nc: (be[i], 0, nfj + jj(i, j, nu))),
                      pl.BlockSpec((1, tf, d), lambda i, j, be, nu, nc: (be[i], jj(i, j, nu), 0)),
                      pl.BlockSpec((1, 1, d), lambda i, j, be, nu, nc: (be[i], 0, 0))],
            out_specs=pl.BlockSpec((tm * wt, LANES), lambda i, j, be, nu, nc: (i, 0)),
            scratch_shapes=[pltpu.VMEM((tm, d), BF16), pltpu.VMEM((tm, d), F32)]),
        out_shape=jax.ShapeDtypeStruct((nslots * wt, LANES), U32),
        compiler_params=_cparams(("arbitrary", "arbitrary")),
        name="moe_experts",
    )(block_expert, nused, block_chunks, xbuf, w_gate_up, w_gate_up, b_gu3, b_gu3, w_down, b_d3)


def _combine_kernel(slot_ref, y_ref, tw_ref, x1_ref, mod_ref, lng_ref, lnb_ref, o_ref, yb_ref, sem, *, tc):
    _issue_token_copies(lambda a: slot_ref[0, 0, a], y_ref, yb_ref, sem, TOP_K * tc)
    pltpu.make_async_copy(y_ref.at[pl.ds(0, TOP_K * tc * WORD_TILES)], yb_ref, sem).wait()

    tw = tw_ref[...]
    wks = [tw[:, k:k + 1] for k in range(TOP_K)]
    cols = []
    for s in range(WORD_TILES):
        f_lo = jnp.zeros((tc, LANES), F32)
        f_hi = jnp.zeros((tc, LANES), F32)
        for k in range(TOP_K):
            lo, hi = _unpack_words(_load_word_tile(yb_ref, s, tc, base=k * tc))
            f_lo = f_lo + wks[k] * lo
            f_hi = f_hi + wks[k] * hi
        cols += [f_lo, f_hi]
    ffn = jnp.concatenate(cols, axis=-1)
    g2 = mod_ref[0, 0:1, :]
    o_ref[...] = _layer_norm(DN_ALPHA * x1_ref[...] + g2 * ffn, lng_ref[...], lnb_ref[...])


def combine(slot3, ybuf, top_w, x1, modg2, ln_g, ln_b, seq, tc=256):
    n, d = x1.shape
    tc = min(tc, seq)
    per_b = seq // tc
    return pl.pallas_call(
        functools.partial(_combine_kernel, tc=tc),
        grid=(n // tc,),
        in_specs=[pl.BlockSpec((1, 1, TOP_K * tc), lambda i: (i, 0, 0), memory_space=pltpu.SMEM),
                  pl.BlockSpec(memory_space=pl.ANY),
                  pl.BlockSpec((tc, LANES), lambda i: (i, 0)),
                  pl.BlockSpec((tc, d), lambda i: (i, 0)),
                  pl.BlockSpec((1, 1, d), lambda i: (i // per_b, 0, 0)),
                  pl.BlockSpec((1, d), lambda i: (0, 0)),
                  pl.BlockSpec((1, d), lambda i: (0, 0))],
        out_specs=pl.BlockSpec((tc, d), lambda i: (i, 0)),
        out_shape=jax.ShapeDtypeStruct((n, d), F32),
        scratch_shapes=[pltpu.VMEM((TOP_K * tc * WORD_TILES, LANES), U32), pltpu.SemaphoreType.DMA(())],
        compiler_params=_cparams(("arbitrary",)),
        name="combine",
    )(slot3, ybuf, top_w, x1, modg2, ln_g, ln_b)


def _split_w_in(w_in):
    offs = [0]
    for w in PROJ_WIDTHS:
        offs.append(offs[-1] + w)
    seg = [w_in[:, offs[k]:offs[k + 1]] for k in range(len(PROJ_WIDTHS))]
    (q_nsa, k_cmp, v_cmp, k_sel, v_sel, k_win, v_win, g_nsa, q_gla, k_gla, v_gla, a_gla, r_gla) = seg
    big = jnp.concatenate([q_nsa, v_gla, r_gla, k_cmp, v_cmp, k_sel, v_sel, k_win, v_win, q_gla, k_gla],
                          axis=1).astype(BF16)
    pad = lambda w: jnp.pad(w, ((0, 0), (0, LANES - w.shape[1])))
    small = jnp.concatenate([pad(g_nsa), pad(a_gla)], axis=1)
    return big, small


def _overlap_matrix(ncp, nsel):
    cs = jnp.arange(ncp)[:, None] * CMP_STRIDE
    ss = jnp.arange(nsel)[None, :] * SEL_BLOCK
    return jnp.logical_and(cs < ss + SEL_BLOCK, cs + CMP_BLOCK > ss).astype(F32)


MOE_TM = 1024


def kernel(x, c, w_ada, b_ada, w_in, cmp_pe_k, cmp_w1_k, cmp_b1_k, cmp_w2_k, cmp_pe_v, cmp_w1_v, cmp_b1_v,
           cmp_w2_v, gla_w_gate, gla_b_gate, gla_norm_w, w_out, ln1_g, ln1_b, w_router, b_router, w_gate_up,
           b_gate_up, w_down, b_down, ln2_g, ln2_b):
    return _forward(x, c, w_ada, b_ada, w_in, cmp_pe_k, cmp_w1_k, cmp_b1_k, cmp_w2_k, cmp_pe_v, cmp_w1_v,
                    cmp_b1_v, cmp_w2_v, gla_w_gate, gla_b_gate, gla_norm_w, w_out, ln1_g, ln1_b, w_router,
                    b_router, w_gate_up, b_gate_up, w_down, b_down, ln2_g, ln2_b, moe_tm=MOE_TM)


def _forward(x, c, w_ada, b_ada, w_in, cmp_pe_k, cmp_w1_k, cmp_b1_k, cmp_w2_k, cmp_pe_v, cmp_w1_v, cmp_b1_v,
             cmp_w2_v, gla_w_gate, gla_b_gate, gla_norm_w, w_out, ln1_g, ln1_b, w_router, b_router, w_gate_up,
             b_gate_up, w_down, b_down, ln2_g, ln2_b, *, moe_tm):
    bsz, seq, d = x.shape
    n = bsz * seq
    l = 0
    x2 = x.reshape(n, d)

    mod = ada_mod(c, w_ada[l], b_ada[l])
    sh1, sc1, g1, sh2, sc2, g2 = jnp.split(mod, 6, axis=-1)
    mod1 = jnp.stack([sc1, sh1], axis=1)
    mod2 = jnp.stack([g1, sc2, sh2], axis=1)
    modg2 = g2[:, None, :]

    w_big, w_small = _split_w_in(w_in[l])
    p_buf = in_proj(x2, mod1, w_big, seq)
    s_buf = in_proj_small(x2, mod1, w_small, seq)

    ng = seq // CMP_STRIDE
    kvg = p_buf[:, P_KVCMP:P_KVCMP + 2 * KV_WIDTH].reshape(bsz, ng, CMP_STRIDE, 4, HEAD_DIM)
    kvg = kvg.transpose(0, 3, 1, 2, 4).reshape(bsz, 4, ng, CMP_STRIDE * HEAD_DIM)
    half = CMP_STRIDE * HEAD_DIM
    pe2 = jnp.stack([cmp_pe_k[l].reshape(2, half), cmp_pe_v[l].reshape(2, half)])
    w1 = jnp.stack([cmp_w1_k[l], cmp_w1_v[l]]).astype(BF16)
    b1 = jnp.stack([cmp_b1_k[l], cmp_b1_v[l]])[:, None, :]
    w2 = jnp.stack([cmp_w2_k[l], cmp_w2_v[l]]).astype(BF16)
    kvc = compress(kvg, pe2, w1, b1, w2)

    nsel = seq // SEL_BLOCK
    ovl = _overlap_matrix(ng, nsel)
    ocmp, imp = cmp_attention(p_buf, kvc, s_buf, ovl, bsz, seq)
    selb, blk_any = topk_select(imp.reshape(bsz * NSA_KV_HEADS, nsel, seq), qtile=NSA_TQ)
    y_nsa = sel_win_attention(p_buf, selb, blk_any, ocmp, s_buf, bsz, seq, tq=NSA_TQ)

    wg_pad = jnp.pad(gla_w_gate[l], ((0, LANES - GLA_GATE_RANK), (0, 0)))
    tril = jnp.tril(jnp.ones((GLA_CHUNK, GLA_CHUNK), F32))
    y_gla = gla_attention(p_buf, s_buf, wg_pad, gla_b_gate[l][None, :], gla_norm_w[l][None, :], tril, bsz, seq)

    tm_r = min(512, seq)
    ltri = jnp.tril(jnp.ones((tm_r, tm_r), F32), k=-1).astype(BF16)
    x1, hw, top_idx, top_w, rank, counts = out_proj_router(
        y_nsa, y_gla, w_out[l].astype(BF16), x2, mod2, ln1_g[l][None, :], ln1_b[l][None, :],
        w_router[l], b_router[l][None, :], ltri, seq, tm=tm_r)

    tm = moe_tm
    n_asg = n * TOP_K
    cnt = counts[0]
    padded = (cnt + tm - 1) // tm * tm
    ends_p = jnp.cumsum(padded)
    start_p = ends_p - padded
    idx4 = top_idx[:, :TOP_K]
    slot_of = (start_p[idx4] + rank[:, :TOP_K]).astype(I32)
    n_blocks = -(-n_asg // tm) + N_EXPERTS
    n_slots = n_blocks * tm
    nused = (ends_p[-1] // tm).astype(I32).reshape(1)
    blk_start = jnp.minimum(jnp.arange(n_blocks, dtype=I32), nused - 1) * tm
    block_expert = jnp.minimum(jnp.sum(ends_p[None, :] <= blk_start[:, None], axis=1), N_EXPERTS - 1).astype(I32)
    sub = tm // MOE_CHUNKS
    rows_left = cnt[block_expert] - (blk_start - start_p[block_expert])
    block_chunks = jnp.clip((rows_left + sub - 1) // sub, 1, MOE_CHUNKS).astype(I32)

    tc = min(256, seq)
    xbuf = dispatch_rows(slot_of.reshape(n // tc, 1, tc * TOP_K), hw, n_slots, tc)
    ybuf = moe_experts(block_expert, nused, block_chunks, xbuf, w_gate_up[l], b_gate_up[l], w_down[l], b_down[l], tm)

    slot3 = slot_of.reshape(n // tc, tc, TOP_K).transpose(0, 2, 1).reshape(n // tc, 1, TOP_K * tc)
    out = combine(slot3, ybuf, top_w, x1, modg2, ln2_g[l][None, :], ln2_b[l][None, :], seq, tc=tc)
    return out.reshape(bsz, seq, d)
```

```python
import functools

import jax
import jax.numpy as jnp
from jax import lax
from jax.experimental import pallas as pl
from jax.experimental.pallas import tpu as pltpu

F32 = jnp.float32
BF16 = jnp.bfloat16
U32 = jnp.uint32
I32 = jnp.int32
HIGHEST = lax.Precision.HIGHEST

D_MODEL = 2048
HEAD_DIM = 128
NSA_HEADS = 8
NSA_KV_HEADS = 2
NSA_GROUP = 4
NSA_WIDTH = 1024
KV_WIDTH = 256
CMP_BLOCK = 32
CMP_STRIDE = 16
CMP_HIDDEN = 256
SEL_BLOCK = 64
SEL_TOPK = 16
SEL_LOCAL = 2
WINDOW = 512
GLA_HEADS = 4
GLA_DV = 256
GLA_DK = 128
GLA_WIDTH = 1024
GLA_GATE_RANK = 16
GLA_GATE_TAU = 16.0
GLA_CHUNK = 64
GLA_SUB = 16
N_EXPERTS = 32
TOP_K = 4
D_FF = 2048
SWIGLU_LIMIT = 7.0
SWIGLU_ALPHA = 1.702
LN_EPS = 1e-5
RMS_EPS = 1e-6
NEG_INF = -1e30
FORCE_SCORE = 1e6
DN_ALPHA = 2.0 ** 0.25
PROJ_WIDTHS = (1024, 256, 256, 256, 256, 256, 256, 24, 512, 512, 1024, 16, 1024)

LANES = 128
VMEM_LIMIT = 56 * 1024 * 1024
WORD_TILES = D_MODEL // (2 * LANES)
GATHER_UNROLL = 8
NSA_TQ = 128
MOE_CHUNKS = 4
P_QNSA = 0
P_VGLA = 1024
P_RGLA = 2048
P_KVCMP = 3072
P_KSEL = 3584
P_VSEL = 3840
P_KWIN = 4096
P_VWIN = 4352
P_QGLA = 4608
P_KGLA = 5120
P_WIDTH = 5632
S_WIDTH = 256


def _cparams(sem, vmem=VMEM_LIMIT):
    return pltpu.CompilerParams(dimension_semantics=sem, vmem_limit_bytes=vmem)


def _sigmoid(x):
    return 1.0 / (1.0 + jnp.exp(-x))


def _ada_kernel(c_ref, w_ref, b_ref, o_ref):
    c = c_ref[...]
    sc = c * _sigmoid(c)
    o_ref[...] = jnp.dot(sc, w_ref[...], preferred_element_type=F32, precision=HIGHEST) + b_ref[...]


def ada_mod(c, w_ada, b_ada, tn=1024):
    bsz, d = c.shape
    n = w_ada.shape[1]
    return pl.pallas_call(
        _ada_kernel,
        grid=(n // tn,),
        in_specs=[pl.BlockSpec((bsz, d), lambda j: (0, 0)),
                  pl.BlockSpec((d, tn), lambda j: (0, j)),
                  pl.BlockSpec((1, tn), lambda j: (0, j))],
        out_specs=pl.BlockSpec((bsz, tn), lambda j: (0, j)),
        out_shape=jax.ShapeDtypeStruct((bsz, n), F32),
        compiler_params=_cparams(("arbitrary",)),
        name="ada_mod",
    )(c, w_ada, b_ada.reshape(1, n))


def _inproj_kernel(x_ref, mod_ref, w_ref, o_ref, hb_ref):
    @pl.when(pl.program_id(1) == 0)
    def _():
        sc = mod_ref[0, 0:1, :]
        sh = mod_ref[0, 1:2, :]
        hb_ref[...] = (x_ref[...] * (1.0 + sc) + sh).astype(BF16)

    o_ref[...] = jnp.dot(hb_ref[...], w_ref[...], preferred_element_type=F32).astype(o_ref.dtype)


def in_proj(x2, mod1, w_bf16, seq, tm=512, tn=None):
    n, d = x2.shape
    wn = w_bf16.shape[1]
    tn = wn if tn is None else tn
    tm = min(tm, seq)
    per_b = seq // tm
    return pl.pallas_call(
        _inproj_kernel,
        grid=(n // tm, wn // tn),
        in_specs=[pl.BlockSpec((tm, d), lambda i, j: (i, 0)),
                  pl.BlockSpec((1, 2, d), lambda i, j: (i // per_b, 0, 0)),
                  pl.BlockSpec((d, tn), lambda i, j: (0, j))],
        out_specs=pl.BlockSpec((tm, tn), lambda i, j: (i, j)),
        out_shape=jax.ShapeDtypeStruct((n, wn), BF16),
        scratch_shapes=[pltpu.VMEM((tm, d), BF16)],
        compiler_params=_cparams(("arbitrary", "arbitrary")),
        name="in_proj",
    )(x2, mod1, w_bf16)


def _split_weight(w):
    hi = w.astype(BF16)
    lo = (w - hi.astype(F32)).astype(BF16)
    return jnp.concatenate([hi, lo], axis=1)


def _dot_split(a, w2_ref):
    n = w2_ref.shape[1] // 2
    a_hi = a.astype(BF16)
    a_lo = (a - a_hi.astype(F32)).astype(BF16)
    o2 = jnp.dot(a_hi, w2_ref[...], preferred_element_type=F32)
    return o2[:, 0:n] + o2[:, n:2 * n] + jnp.dot(a_lo, w2_ref[:, 0:n], preferred_element_type=F32)


def _inproj_small_kernel(x_ref, mod_ref, w_ref, o_ref):
    sc = mod_ref[0, 0:1, :]
    sh = mod_ref[0, 1:2, :]
    h = x_ref[...] * (1.0 + sc) + sh
    o_ref[...] = _dot_split(h, w_ref)


def in_proj_small(x2, mod1, w_f32, seq, tm=512):
    n, d = x2.shape
    wn = w_f32.shape[1]
    w2 = _split_weight(w_f32)
    tm = min(tm, seq)
    per_b = seq // tm
    return pl.pallas_call(
        _inproj_small_kernel,
        grid=(n // tm,),
        in_specs=[pl.BlockSpec((tm, d), lambda i: (i, 0)),
                  pl.BlockSpec((1, 2, d), lambda i: (i // per_b, 0, 0)),
                  pl.BlockSpec((d, 2 * wn), lambda i: (0, 0))],
        out_specs=pl.BlockSpec((tm, wn), lambda i: (i, 0)),
        out_shape=jax.ShapeDtypeStruct((n, wn), F32),
        compiler_params=_cparams(("arbitrary",)),
        name="in_proj_small",
    )(x2, mod1, w2)


def _gelu_tanh(x):
    return 0.5 * x * (1.0 + jnp.tanh(0.7978845608028654 * (x + 0.044715 * (x * x * x))))


def _compress_kernel(g_ref, pe_ref, w1_ref, b1_ref, w2_ref, o_ref):
    half = CMP_STRIDE * HEAD_DIM
    g = g_ref[0, 0].astype(F32)
    ga = (g + pe_ref[0, 0:1, :]).astype(BF16)
    gb = (g + pe_ref[0, 1:2, :]).astype(BF16)
    u = jnp.dot(ga, w1_ref[0, 0:half, :], preferred_element_type=F32)
    v = jnp.dot(gb, w1_ref[0, half:2 * half, :], preferred_element_type=F32)
    ng = u.shape[0]
    v_next = pltpu.roll(v, ng - 1, 0)
    h = _gelu_tanh(u + v_next + b1_ref[0])
    o_ref[0, 0] = jnp.dot(h.astype(BF16), w2_ref[0], preferred_element_type=F32).astype(o_ref.dtype)


def compress(kvg, pe2, w1, b1, w2):
    bsz, four, ng, gd = kvg.shape
    return pl.pallas_call(
        _compress_kernel,
        grid=(bsz, four),
        in_specs=[pl.BlockSpec((1, 1, ng, gd), lambda b, j: (b, j, 0, 0)),
                  pl.BlockSpec((1, 2, gd), lambda b, j: (j // 2, 0, 0)),
                  pl.BlockSpec((1, 2 * gd, CMP_HIDDEN), lambda b, j: (j // 2, 0, 0)),
                  pl.BlockSpec((1, 1, CMP_HIDDEN), lambda b, j: (j // 2, 0, 0)),
                  pl.BlockSpec((1, CMP_HIDDEN, HEAD_DIM), lambda b, j: (j // 2, 0, 0))],
        out_specs=pl.BlockSpec((1, 1, ng, HEAD_DIM), lambda b, j: (b, j, 0, 0)),
        out_shape=jax.ShapeDtypeStruct((bsz, four, ng, HEAD_DIM), BF16),
        compiler_params=_cparams(("arbitrary", "arbitrary")),
        name="compress",
    )(kvg, pe2, w1, b1, w2)


def _cmp_attn_kernel(q_ref, kc_ref, vct_ref, gate_ref, ovlt_ref, o_ref, imp_ref, *, tq):
    i = pl.program_id(1)
    ncp = kc_ref.shape[2]
    R = NSA_GROUP
    D = HEAD_DIM
    scale = D ** -0.5
    q0 = i * tq
    crow = lax.broadcasted_iota(I32, (ncp, 1), 0)
    pos_col = q0 + jnp.concatenate([lax.broadcasted_iota(I32, (1, tq), 1)] * R, axis=1)
    ok = (crow * CMP_STRIDE + (CMP_BLOCK - 1)) <= pos_col
    any_ok = (pos_col >= CMP_BLOCK - 1).astype(F32)
    a = (crow - q0 // CMP_STRIDE).astype(F32)
    a_hi = jnp.floor(a * (1.0 / 32.0))
    a_lo = a - 32.0 * a_hi
    lane = lax.broadcasted_iota(I32, (1, LANES), 1)
    sub = lax.broadcasted_iota(I32, (LANES, 1), 0)
    kaug = jnp.where(lane == 0, a_hi, jnp.where(lane == 1, a_lo, 0.0)).astype(BF16)
    gates_t = _sigmoid(gate_ref[...]).T
    q = q_ref[...]
    for g in range(NSA_KV_HEADS):
        qa_cols = []
        for r in range(R):
            h = g * R + r
            slope = 2.0 ** (-(h + 1))
            qs_t = (q[:, h * D:(h + 1) * D].astype(F32) * scale).T.astype(BF16)
            aug = jnp.where(sub == 0, 32.0 * CMP_STRIDE * slope, jnp.where(sub == 1, CMP_STRIDE * slope, 0.0))
            qa_cols.append(jnp.concatenate([qs_t, jnp.broadcast_to(aug, (D, tq)).astype(BF16)], axis=0))
        qa = jnp.concatenate(qa_cols, axis=1)
        ka = jnp.concatenate([kc_ref[0, g], kaug], axis=1)
        st = jnp.dot(ka, qa, preferred_element_type=F32)
        st = jnp.where(ok, st, NEG_INF)
        m = jnp.max(st, axis=0, keepdims=True)
        e = jnp.exp(st - m)
        p = e * (any_ok / jnp.sum(e, axis=0, keepdims=True))
        o_t = jnp.dot(vct_ref[0, g], p.astype(BF16), preferred_element_type=F32)
        psum_t = p[:, 0:tq]
        for r in range(1, R):
            psum_t = psum_t + p[:, r * tq:(r + 1) * tq]
        for r in range(R):
            h = g * R + r
            o_ref[g, :, r * tq:(r + 1) * tq] = o_t[:, r * tq:(r + 1) * tq] * gates_t[3 * h:3 * h + 1, :]
        imp_ref[g] = jnp.dot(ovlt_ref[...], psum_t, preferred_element_type=F32, precision=HIGHEST)


def cmp_attention(p_buf, kvc, s_buf, ovl, bsz, seq, tq=128):
    ncp = kvc.shape[2]
    nsel = ovl.shape[1]
    per_b = seq // tq
    G, D, C = NSA_KV_HEADS, HEAD_DIM, NSA_GROUP * tq
    kc = kvc[:, 0:G]
    vct = kvc[:, G:2 * G].transpose(0, 1, 3, 2)
    return pl.pallas_call(
        functools.partial(_cmp_attn_kernel, tq=tq),
        grid=(bsz, per_b),
        in_specs=[pl.BlockSpec((tq, NSA_WIDTH), lambda b, i: (b * per_b + i, P_QNSA // NSA_WIDTH)),
                  pl.BlockSpec((1, G, ncp, D), lambda b, i: (b, 0, 0, 0)),
                  pl.BlockSpec((1, G, D, ncp), lambda b, i: (b, 0, 0, 0)),
                  pl.BlockSpec((tq, LANES), lambda b, i: (b * per_b + i, 0)),
                  pl.BlockSpec((nsel, ncp), lambda b, i: (0, 0))],
        out_specs=[pl.BlockSpec((None, G, None, D, C), lambda b, i: (b, 0, i, 0, 0)),
                   pl.BlockSpec((None, G, nsel, tq), lambda b, i: (b, 0, 0, i))],
        out_shape=[jax.ShapeDtypeStruct((bsz, G, per_b, D, C), F32),
                   jax.ShapeDtypeStruct((bsz, G, nsel, seq), F32)],
        compiler_params=_cparams(("arbitrary", "arbitrary")),
        name="cmp_attention",
    )(p_buf, kc, vct, s_buf, ovl.T)


def _topk_kernel(imp_ref, o_ref, any_ref, *, n_topk, qtile):
    nsel, cols = imp_ref.shape
    pos = pl.program_id(1) * cols + lax.broadcasted_iota(I32, (1, cols), 1)
    cur = pos // SEL_BLOCK
    blk = lax.broadcasted_iota(I32, (nsel, 1), 0)
    blkf = blk.astype(F32)
    forced = jnp.logical_or(blk == 0, jnp.logical_and(blk <= cur, blk > cur - SEL_LOCAL))
    work = jnp.where(forced, FORCE_SCORE, jnp.where(blk > cur, -1.0, imp_ref[...]))
    sel = jnp.zeros((nsel, cols), F32)
    for _ in range(n_topk):
        m = jnp.max(work, axis=0, keepdims=True)
        first = jnp.min(jnp.where(work == m, blkf, float(nsel)), axis=0, keepdims=True)
        pick = blkf == first
        sel = jnp.where(pick, 1.0, sel)
        work = jnp.where(pick, -2.0, work)
    keep = jnp.logical_and(sel > 0.5, blk <= cur)
    res = jnp.where(keep, 0.0, NEG_INF)
    if nsel < LANES:
        res = jnp.concatenate([res, jnp.full((LANES - nsel, cols), NEG_INF, F32)], axis=0)
    o_ref[...] = res.astype(o_ref.dtype)
    hit = jnp.where(res == 0.0, 1.0, 0.0)
    lane = lax.broadcasted_iota(I32, (1, LANES), 1)
    out = jnp.zeros((LANES, LANES), F32)
    for qt in range(cols // qtile):
        col = jnp.max(hit[:, qt * qtile:(qt + 1) * qtile], axis=1, keepdims=True)
        out = jnp.where(lane == qt, col, out)
    any_ref[...] = out


def topk_select(imp_t, qtile, cols=1024):
    nbg, nsel, seq = imp_t.shape
    cols = min(cols, seq)
    nstep = seq // cols
    mask_t, hits = pl.pallas_call(
        functools.partial(_topk_kernel, n_topk=min(SEL_TOPK, nsel), qtile=qtile),
        grid=(nbg, nstep),
        in_specs=[pl.BlockSpec((None, nsel, cols), lambda b, j: (b, 0, j))],
        out_specs=[pl.BlockSpec((None, LANES, cols), lambda b, j: (b, 0, j)),
                   pl.BlockSpec((None, None, LANES, LANES), lambda b, j: (b, j, 0, 0))],
        out_shape=[jax.ShapeDtypeStruct((nbg, LANES, seq), BF16),
                   jax.ShapeDtypeStruct((nbg, nstep, LANES, LANES), F32)],
        compiler_params=_cparams(("arbitrary", "arbitrary")),
        name="topk_select",
    )(imp_t)
    per_step = cols // qtile
    blk_any = hits[:, :, :, :per_step].transpose(0, 1, 3, 2).reshape(nbg * nstep * per_step, LANES)
    return mask_t, blk_any


def _sel_win_kernel(ids_ref, cnt_ref, q_ref, ks_ref, vst_ref, kw_ref, vwt_ref, sb_ref, kau_ref, kad_ref, oc_ref,
                    gate_ref, o_ref, qau_ref, qad_ref, qaw_ref, acc_ref, s_ref, mt_ref, *, tq, tk, seq):
    g = pl.program_id(1)
    i = pl.program_id(2)
    q0 = i * tq
    scale = HEAD_DIM ** -0.5
    R = NSA_GROUP
    C = R * tq
    D = HEAD_DIM
    dn = (((1,), (1,)), ((), ()))
    gslope = jnp.where(g == 0, 1.0, 2.0 ** (-NSA_GROUP))
    slopes = [gslope * 2.0 ** (-(r + 1)) for r in range(R)]

    qrel_col = jnp.concatenate([lax.broadcasted_iota(I32, (1, tq), 1)] * R, axis=1)
    slope_col = jnp.concatenate([jnp.full((1, tq), 1.0, F32) * slopes[r] for r in range(R)], axis=1)

    q = q_ref[...]
    selb_t = sb_ref[...].astype(F32)
    lane = lax.broadcasted_iota(I32, (1, LANES), 1)
    sub = lax.broadcasted_iota(I32, (LANES, 1), 0)
    nrel = ((sub - q0 // SEL_BLOCK) * SEL_BLOCK).astype(F32)
    for r in range(R):
        cols = slice(r * tq, (r + 1) * tq)
        qs_t = (q[:, r * D:(r + 1) * D].astype(F32) * scale).T.astype(BF16)
        sb_t = selb_t + slopes[r] * nrel
        qau_ref[0:D, cols] = qs_t
        qau_ref[D:2 * D, cols] = jnp.where(sub == LANES - 1, slopes[r], sb_t).astype(BF16)
        qad_ref[0:D, cols] = qs_t
        qad_ref[D:2 * D, cols] = sb_t.astype(BF16)
        qaw_ref[0:D, cols] = qs_t
        aw = jnp.where(sub == 0, slopes[r] * SEL_BLOCK, jnp.where(sub == 1, slopes[r], 0.0))
        qaw_ref[D:2 * D, cols] = jnp.broadcast_to(aw, (D, tq)).astype(BF16)

    acc_ref[...] = jnp.zeros_like(acc_ref)

    def online_update(st, t, m, l):
        m_new = jnp.maximum(m, jnp.max(st, axis=0, keepdims=True))
        alpha = jnp.exp(m - m_new)
        p = jnp.exp(st - m_new)
        l_new = alpha * l + jnp.sum(p, axis=0, keepdims=True)
        pv = jnp.dot(vst_ref[t], p.astype(BF16), preferred_element_type=F32)
        acc_ref[...] = acc_ref[...] * alpha + pv
        return m_new, l_new

    S_SPARE, S_NEGINF = 2, 3

    @pl.when(jnp.logical_and(pl.program_id(0) == 0, jnp.logical_and(g == 0, i == 0)))
    def _():
        s_ref[S_NEGINF] = jnp.full((tk, C), -jnp.inf, F32)

    t_last = q0 // tk
    step = (pl.program_id(0) * pl.num_programs(1) + g) * pl.num_programs(2) + i
    cnt = cnt_ref[step]
    id0 = step * (seq // tk)

    def pipelined(u, carry):
        m, l = carry
        tb = ids_ref[id0 + jnp.maximum(u - 1, 0)]
        st = s_ref[jnp.where(u >= 1, (u + 1) % 2, S_NEGINF)]
        m_new = jnp.maximum(m, mt_ref[0:1, :])
        alpha = jnp.exp(m - m_new)
        p = jnp.exp(st - m_new)
        l_new = alpha * l + jnp.sum(p, axis=0, keepdims=True)
        pv = jnp.dot(vst_ref[tb], p.astype(BF16), preferred_element_type=F32)
        acc_ref[...] = acc_ref[...] * alpha + pv
        ta = ids_ref[id0 + jnp.minimum(u, cnt - 1)]
        k0 = pl.multiple_of(ta * tk, tk)
        ka = jnp.concatenate([ks_ref[pl.ds(k0, tk), :], kau_ref[pl.ds(k0, tk), :]], axis=1)
        sa = jnp.dot(ka, qau_ref[...], preferred_element_type=F32)
        s_ref[jnp.where(u < cnt, u % 2, S_SPARE)] = sa
        mt_ref[...] = jnp.broadcast_to(jnp.max(sa, axis=0, keepdims=True), mt_ref.shape)
        return m_new, l_new

    m0 = jnp.full((1, C), NEG_INF, F32)
    l0 = jnp.zeros((1, C), F32)
    mt_ref[...] = jnp.full(mt_ref.shape, -jnp.inf, F32)
    m, l = lax.fori_loop(0, jnp.where(cnt > 0, cnt + 1, 0), pipelined, (m0, l0))

    k0 = pl.multiple_of(t_last * tk, tk)
    ka = jnp.concatenate([ks_ref[pl.ds(k0, tk), :], kad_ref[pl.ds(k0, tk), :]], axis=1)
    st = jnp.dot(ka, qad_ref[...], preferred_element_type=F32)
    krow = lax.broadcasted_iota(I32, (tk, 1), 0)
    koff = (krow % SEL_BLOCK).astype(F32)
    st = st + koff * slope_col
    st = jnp.where(krow + (k0 - q0) <= qrel_col, st, NEG_INF)
    m, l = online_update(st, t_last, m, l)
    inv_l = 1.0 / l

    wk = WINDOW + tq
    w0 = pl.multiple_of(jnp.maximum(q0 - WINDOW, 0), tq)
    wrow = lax.broadcasted_iota(I32, (wk, 1), 0) + (w0 - q0)
    w_hi = jnp.floor(wrow.astype(F32) * (1.0 / SEL_BLOCK))
    w_lo = wrow.astype(F32) - w_hi * SEL_BLOCK
    kaw = jnp.where(lane == 0, w_hi, jnp.where(lane == 1, w_lo, 0.0)).astype(BF16)
    kwa = jnp.concatenate([kw_ref[pl.ds(w0, wk), :], kaw], axis=1)
    sw = jnp.dot(kwa, qaw_ref[...], preferred_element_type=F32)
    dist = qrel_col - wrow
    sw = jnp.where(jnp.logical_and(dist >= 0, dist < WINDOW), sw, NEG_INF)
    mw = jnp.max(sw, axis=0, keepdims=True)
    pw = jnp.exp(sw - mw)
    inv_lw = 1.0 / jnp.sum(pw, axis=0, keepdims=True)
    c0 = w0 // tq
    vwt = jnp.concatenate([vwt_ref[c0 + c] for c in range(wk // tq)], axis=1)
    ow = jnp.dot(vwt, pw.astype(BF16), preferred_element_type=F32)

    gates_t = _sigmoid(gate_ref[...]).T
    osel = acc_ref[...] * inv_l
    ow = ow * inv_lw
    for r in range(R):
        cols = slice(r * tq, (r + 1) * tq)
        g_sel = jnp.where(g == 0, gates_t[3 * r + 1:3 * r + 2, :], gates_t[3 * (R + r) + 1:3 * (R + r) + 2, :])
        g_win = jnp.where(g == 0, gates_t[3 * r + 2:3 * r + 3, :], gates_t[3 * (R + r) + 2:3 * (R + r) + 3, :])
        y_t = oc_ref[:, cols] + g_sel * osel[:, cols] + g_win * ow[:, cols]
        y = y_t.T
        o_ref[:, r * D:(r + 1) * D] = y.astype(o_ref.dtype)


def _key_side_constants(seq):
    nsel = seq // SEL_BLOCK
    key = jnp.arange(seq)
    blk = key // SEL_BLOCK
    lane = jnp.arange(LANES)[None, :]
    onehot = (lane == blk[:, None])
    kad = onehot.astype(BF16)
    kau = jnp.where(lane == LANES - 1, (key % SEL_BLOCK)[:, None].astype(F32),
                    jnp.logical_and(onehot, lane < nsel - SEL_LOCAL).astype(F32)).astype(BF16)
    return kau, kad


def _active_key_tiles(blk_any, seq, tq, tk):
    nt = seq // tk
    per_b = seq // tq
    bpt = tk // SEL_BLOCK
    tile_any = blk_any[:, :nt * bpt].reshape(-1, nt, bpt).max(axis=-1) > 0.5
    t_last = (jnp.arange(blk_any.shape[0], dtype=I32) % per_b) * tq // tk
    tile_idx = jnp.arange(nt, dtype=I32)[None, :]
    active = jnp.logical_and(tile_any, tile_idx < t_last[:, None])
    ids = jnp.argsort(jnp.where(active, tile_idx, nt + tile_idx), axis=-1).astype(I32)
    return ids.reshape(-1), active.sum(axis=-1).astype(I32)


def sel_win_attention(p_buf, selb, blk_any, ocmp, s_buf, bsz, seq, tq=128, tk=512):
    n = p_buf.shape[0]
    tk = min(tk, seq)
    per_b = seq // tq
    gw = NSA_GROUP * HEAD_DIM
    G = NSA_KV_HEADS
    D = HEAD_DIM
    kau, kad = _key_side_constants(seq)
    if blk_any.shape[1] < (seq // tk) * (tk // SEL_BLOCK):
        blk_any = jnp.pad(blk_any, ((0, 0), (0, (seq // tk) * (tk // SEL_BLOCK) - blk_any.shape[1])))
    tile_ids, tile_cnt = _active_key_tiles(blk_any, seq, tq, tk)
    vsel = p_buf[:, P_VSEL:P_VSEL + KV_WIDTH].reshape(bsz, seq // tk, tk, G, D).transpose(0, 3, 1, 4, 2)
    vwin = p_buf[:, P_VWIN:P_VWIN + KV_WIDTH].reshape(bsz, seq // tq, tq, G, D).transpose(0, 3, 1, 4, 2)

    def colblk(off):
        return lambda b, g, i, ids, cnt: (b, off // D + g)

    return pl.pallas_call(
        functools.partial(_sel_win_kernel, tq=tq, tk=tk, seq=seq),
        grid_spec=pltpu.PrefetchScalarGridSpec(
            num_scalar_prefetch=2,
            grid=(bsz, G, per_b),
            in_specs=[pl.BlockSpec((tq, gw), lambda b, g, i, ids, cnt: (b * per_b + i, P_QNSA // gw + g)),
                      pl.BlockSpec((seq, D), colblk(P_KSEL)),
                      pl.BlockSpec((None, None, seq // tk, D, tk), lambda b, g, i, ids, cnt: (b, g, 0, 0, 0)),
                      pl.BlockSpec((seq, D), colblk(P_KWIN)),
                      pl.BlockSpec((None, None, seq // tq, D, tq), lambda b, g, i, ids, cnt: (b, g, 0, 0, 0)),
                      pl.BlockSpec((None, LANES, tq), lambda b, g, i, ids, cnt: (b * G + g, 0, i)),
                      pl.BlockSpec((seq, LANES), lambda b, g, i, ids, cnt: (0, 0)),
                      pl.BlockSpec((seq, LANES), lambda b, g, i, ids, cnt: (0, 0)),
                      pl.BlockSpec((None, None, None, D, gw), lambda b, g, i, ids, cnt: (b, g, i, 0, 0)),
                      pl.BlockSpec((tq, LANES), lambda b, g, i, ids, cnt: (b * per_b + i, 0))],
            out_specs=pl.BlockSpec((tq, gw), lambda b, g, i, ids, cnt: (b * per_b + i, g)),
            scratch_shapes=[pltpu.VMEM((2 * D, NSA_GROUP * tq), BF16),
                            pltpu.VMEM((2 * D, NSA_GROUP * tq), BF16),
                            pltpu.VMEM((2 * D, NSA_GROUP * tq), BF16),
                            pltpu.VMEM((D, NSA_GROUP * tq), F32),
                            pltpu.VMEM((4, tk, NSA_GROUP * tq), F32),
                            pltpu.VMEM((8, NSA_GROUP * tq), F32)]),
        out_shape=jax.ShapeDtypeStruct((n, NSA_WIDTH), BF16),
        compiler_params=_cparams(("arbitrary", "arbitrary", "arbitrary")),
        name="sel_win_attention",
    )(tile_ids, tile_cnt, p_buf, p_buf, vsel, p_buf, vwin, selb, kau, kad, ocmp, s_buf)


def _gla_kernel(q_ref, k_ref, v_ref, r_ref, a_ref, wg_ref, bg_ref, nw_ref, tril_ref, o_ref,
                st_ref, b_ref, *, tc):
    C, SUB, H = GLA_CHUNK, GLA_SUB, GLA_HEADS

    @pl.when(pl.program_id(1) == 0)
    def _():
        st_ref[...] = jnp.zeros_like(st_ref)

    x = jnp.dot(a_ref[...], wg_ref[...], preferred_element_type=F32, precision=HIGHEST) + bg_ref[...]
    la = (jnp.minimum(x, 0.0) - jnp.log(1.0 + jnp.exp(-jnp.abs(x)))) * (1.0 / GLA_GATE_TAU)
    tril = tril_ref[...]
    for c in range(tc // C):
        b_ref[c * C:(c + 1) * C, :] = jnp.dot(tril, la[c * C:(c + 1) * C, :],
                                              preferred_element_type=F32, precision=HIGHEST)

    row16 = lax.broadcasted_iota(I32, (SUB, C), 0)
    lane64 = lax.broadcasted_iota(I32, (SUB, C), 1)
    nw = nw_ref[...]

    def chunk(c, carry):
        r0 = pl.multiple_of(c * C, C)
        for h in range(H):
            bh = b_ref[pl.ds(r0, C), h * GLA_DK:(h + 1) * GLA_DK]
            qc = q_ref[pl.ds(r0, C), h * GLA_DK:(h + 1) * GLA_DK].astype(F32) * (GLA_DK ** -0.5)
            kc = k_ref[pl.ds(r0, C), h * GLA_DK:(h + 1) * GLA_DK].astype(F32)
            vc = v_ref[pl.ds(r0, C), h * GLA_DV:(h + 1) * GLA_DV]
            st = st_ref[h]
            b_last = bh[C - 1:C, :]
            qe = (qc * jnp.exp(bh)).astype(BF16)
            o = lax.dot_general(qe, st.astype(BF16), (((1,), (1,)), ((), ())),
                                preferred_element_type=F32)
            a_rows = []
            for sb in range(C // SUB):
                s0 = sb * SUB
                bi = bh[s0:s0 + SUB, :]
                qi = qc[s0:s0 + SUB, :]
                ki = kc[s0:s0 + SUB, :]
                beta = bh[s0:s0 + 1, :]
                if sb > 0:
                    qd = (qi * jnp.exp(bi - beta)).astype(BF16)
                    kd = (kc * jnp.exp(jnp.minimum(beta - bh, 0.0))).astype(BF16)
                    a_i = lax.dot_general(qd, kd, (((1,), (1,)), ((), ())), preferred_element_type=F32)
                    a_i = jnp.where(lane64 < s0, a_i, 0.0)
                else:
                    a_i = jnp.zeros((SUB, C), F32)
                for s in range(SUB):
                    e = jnp.exp(jnp.minimum(bi - bi[s:s + 1, :], 0.0))
                    col = jnp.sum(qi * ki[s:s + 1, :] * e, axis=-1, keepdims=True)
                    a_i = jnp.where(jnp.logical_and(lane64 == s0 + s, row16 >= s), col, a_i)
                a_rows.append(a_i)
            a_mat = jnp.concatenate(a_rows, axis=0)
            o = o + jnp.dot(a_mat.astype(BF16), vc, preferred_element_type=F32)
            kdec = (kc * jnp.exp(b_last - bh)).astype(BF16)
            upd = lax.dot_general(vc, kdec, (((0,), (0,)), ((), ())), preferred_element_type=F32)
            st_ref[h] = st * jnp.exp(b_last) + upd
            rms = lax.rsqrt(jnp.mean(o * o, axis=-1, keepdims=True) + RMS_EPS)
            rr = r_ref[pl.ds(r0, C), h * GLA_DV:(h + 1) * GLA_DV].astype(F32)
            y = o * rms * nw * (rr * _sigmoid(rr))
            o_ref[pl.ds(r0, C), h * GLA_DV:(h + 1) * GLA_DV] = y.astype(o_ref.dtype)
        return carry

    lax.fori_loop(0, tc // C, chunk, 0)


def gla_attention(p_buf, s_buf, wg_pad, bg, nw, tril, bsz, seq, tc=512):
    n = p_buf.shape[0]
    tc = min(tc, seq)
    per_b = seq // tc
    kw = GLA_HEADS * GLA_DK
    return pl.pallas_call(
        functools.partial(_gla_kernel, tc=tc),
        grid=(bsz, per_b),
        in_specs=[pl.BlockSpec((tc, kw), lambda b, j: (b * per_b + j, P_QGLA // kw)),
                  pl.BlockSpec((tc, kw), lambda b, j: (b * per_b + j, P_KGLA // kw)),
                  pl.BlockSpec((tc, GLA_WIDTH), lambda b, j: (b * per_b + j, P_VGLA // GLA_WIDTH)),
                  pl.BlockSpec((tc, GLA_WIDTH), lambda b, j: (b * per_b + j, P_RGLA // GLA_WIDTH)),
                  pl.BlockSpec((tc, LANES), lambda b, j: (b * per_b + j, 1)),
                  pl.BlockSpec((LANES, kw), lambda b, j: (0, 0)),
                  pl.BlockSpec((1, kw), lambda b, j: (0, 0)),
                  pl.BlockSpec((1, GLA_DV), lambda b, j: (0, 0)),
                  pl.BlockSpec((GLA_CHUNK, GLA_CHUNK), lambda b, j: (0, 0))],
        out_specs=pl.BlockSpec((tc, GLA_WIDTH), lambda b, j: (b * per_b + j, 0)),
        out_shape=jax.ShapeDtypeStruct((n, GLA_WIDTH), BF16),
        scratch_shapes=[pltpu.VMEM((GLA_HEADS, GLA_DV, GLA_DK), F32),
                        pltpu.VMEM((tc, kw), F32)],
        compiler_params=_cparams(("arbitrary", "arbitrary")),
        name="gla_attention",
    )(p_buf, p_buf, p_buf, p_buf, s_buf, wg_pad, bg, nw, tril)


def _pack_words(lo, hi):
    lo_b = pltpu.bitcast(lo.astype(BF16).astype(F32), U32)
    hi_b = pltpu.bitcast(hi.astype(BF16).astype(F32), U32)
    return jnp.bitwise_or(hi_b, jnp.right_shift(lo_b, jnp.uint32(16)))


def _unpack_words(w):
    lo = pltpu.bitcast(jnp.left_shift(w, jnp.uint32(16)), F32)
    hi = pltpu.bitcast(jnp.bitwise_and(w, jnp.uint32(0xFFFF0000)), F32)
    return lo, hi


def _store_packed(ref, v, first_tile=0):
    rows = v.shape[0]
    n_tiles = ref.shape[0] // rows
    for u in range(v.shape[1] // (2 * LANES)):
        w = _pack_words(v[:, 2 * u * LANES:(2 * u + 1) * LANES], v[:, (2 * u + 1) * LANES:(2 * u + 2) * LANES])
        ref[pl.ds(first_tile + u, rows, stride=n_tiles), :] = w


def _load_word_tile(ref, s, rows, base=0):
    n_tiles = WORD_TILES
    return ref[pl.ds(base * n_tiles + s, rows, stride=n_tiles), :]


def _layer_norm(z, g, b):
    mu = jnp.mean(z, axis=-1, keepdims=True)
    zc = z - mu
    var = jnp.mean(zc * zc, axis=-1, keepdims=True)
    return zc * lax.rsqrt(var + LN_EPS) * g + b


def _outproj_kernel(yn_ref, yg_ref, wo_ref, x_ref, mod_ref, lng_ref, lnb_ref, wr_ref, br_ref, ltri_ref,
                    x1_ref, hw_ref, idx_ref, tw_ref, rank_ref, cnt_ref, carry_ref):
    step = pl.program_id(0)

    @pl.when(step == 0)
    def _():
        carry_ref[...] = jnp.zeros_like(carry_ref)

    half = yn_ref.shape[1]
    mix = (jnp.dot(yn_ref[...], wo_ref[0:half, :], preferred_element_type=F32) +
           jnp.dot(yg_ref[...], wo_ref[half:2 * half, :], preferred_element_type=F32))
    g1 = mod_ref[0, 0:1, :]
    sc2 = mod_ref[0, 1:2, :]
    sh2 = mod_ref[0, 2:3, :]
    x1 = _layer_norm(DN_ALPHA * x_ref[...] + g1 * mix, lng_ref[...], lnb_ref[...])
    x1_ref[...] = x1
    h2 = x1 * (1.0 + sc2) + sh2
    _store_packed(hw_ref, h2)

    tm = h2.shape[0]
    ne = br_ref.shape[1]
    logits = _dot_split(h2, wr_ref)[:, 0:ne] + br_ref[...]
    lane = lax.broadcasted_iota(I32, (1, ne), 1).astype(F32)
    out_lane = lax.broadcasted_iota(I32, (1, LANES), 1)
    work = logits
    idx_out = jnp.zeros((tm, LANES), F32)
    w_out = jnp.zeros((tm, LANES), F32)
    onehot = jnp.zeros((tm, ne), F32)
    picks = []
    top0 = None
    den = jnp.zeros((tm, 1), F32)
    for k in range(TOP_K):
        m = jnp.max(work, axis=-1, keepdims=True)
        first = jnp.min(jnp.where(work == m, lane, float(ne)), axis=-1, keepdims=True)
        pick = lane == first
        if k == 0:
            top0 = m
        e = jnp.exp(m - top0)
        den = den + e
        idx_out = jnp.where(out_lane == k, first, idx_out)
        w_out = jnp.where(out_lane == k, e, w_out)
        onehot = jnp.where(pick, 1.0, onehot)
        picks.append(pick)
        work = jnp.where(pick, -jnp.inf, work)
    idx_ref[...] = idx_out.astype(I32)
    tw_ref[...] = w_out * (1.0 / den)

    before = jnp.dot(ltri_ref[...], onehot.astype(BF16), preferred_element_type=F32) + carry_ref[0:1, :]
    rank_out = jnp.zeros((tm, LANES), F32)
    for k in range(TOP_K):
        rk = jnp.sum(jnp.where(picks[k], before, 0.0), axis=-1, keepdims=True)
        rank_out = jnp.where(out_lane == k, rk, rank_out)
    rank_ref[...] = rank_out.astype(I32)
    carry_ref[...] = carry_ref[...] + jnp.sum(onehot, axis=0, keepdims=True)
    cnt_ref[...] = carry_ref[...].astype(I32)


def out_proj_router(ynsa, ygla, wo_bf16, x2, mod2, ln_g, ln_b, w_router, b_router, ltri, seq, tm=512):
    n, d = x2.shape
    tm = min(tm, seq)
    per_b = seq // tm
    ne = w_router.shape[1]
    wr2 = _split_weight(jnp.pad(w_router, ((0, 0), (0, LANES - ne))))
    row = lambda i: (i, 0)
    const = lambda i: (0, 0)
    return pl.pallas_call(
        _outproj_kernel,
        grid=(n // tm,),
        in_specs=[pl.BlockSpec((tm, NSA_WIDTH), row),
                  pl.BlockSpec((tm, GLA_WIDTH), row),
                  pl.BlockSpec((d, d), const),
                  pl.BlockSpec((tm, d), row),
                  pl.BlockSpec((1, 3, d), lambda i: (i // per_b, 0, 0)),
                  pl.BlockSpec((1, d), const),
                  pl.BlockSpec((1, d), const),
                  pl.BlockSpec((d, 2 * LANES), const),
                  pl.BlockSpec((1, ne), const),
                  pl.BlockSpec((tm, tm), const)],
        out_specs=[pl.BlockSpec((tm, d), row),
                   pl.BlockSpec((tm * WORD_TILES, LANES), row),
                   pl.BlockSpec((tm, LANES), row),
                   pl.BlockSpec((tm, LANES), row),
                   pl.BlockSpec((tm, LANES), row),
                   pl.BlockSpec((8, ne), const)],
        out_shape=[jax.ShapeDtypeStruct((n, d), F32),
                   jax.ShapeDtypeStruct((n * WORD_TILES, LANES), U32),
                   jax.ShapeDtypeStruct((n, LANES), I32),
                   jax.ShapeDtypeStruct((n, LANES), F32),
                   jax.ShapeDtypeStruct((n, LANES), I32),
                   jax.ShapeDtypeStruct((8, ne), I32)],
        scratch_shapes=[pltpu.VMEM((8, ne), F32)],
        compiler_params=_cparams(("arbitrary",)),
        name="out_proj_router",
    )(ynsa, ygla, wo_bf16, x2, mod2, ln_g, ln_b, wr2, b_router, ltri)


def _issue_token_copies(idx_at, src_ref, dst_ref, sem, count):
    wt = WORD_TILES

    def issue(r8, carry):
        for u in range(GATHER_UNROLL):
            r = r8 * GATHER_UNROLL + u
            tok = idx_at(r)
            pltpu.make_async_copy(src_ref.at[pl.ds(pl.multiple_of(tok * wt, wt), wt)],
                                  dst_ref.at[pl.ds(pl.multiple_of(r * wt, wt), wt)], sem).start(priority=u % 2)
        return carry

    lax.fori_loop(0, count // GATHER_UNROLL, issue, 0)


def _dispatch_kernel(slot_ref, src_ref, zero_ref, o_ref, sem, *, tc):
    del zero_ref
    wt = WORD_TILES

    def issue(t8, carry):
        for u in range(GATHER_UNROLL // TOP_K):
            t = t8 * (GATHER_UNROLL // TOP_K) + u
            for k in range(TOP_K):
                slot = slot_ref[0, 0, t * TOP_K + k]
                pltpu.make_async_copy(src_ref.at[pl.ds(pl.multiple_of(t * wt, wt), wt)],
                                      o_ref.at[pl.ds(pl.multiple_of(slot * wt, wt), wt)], sem).start(priority=k % 2)
        return carry

    lax.fori_loop(0, tc * TOP_K // GATHER_UNROLL, issue, 0)
    for k in range(TOP_K):
        pltpu.make_async_copy(src_ref, o_ref.at[pl.ds(0, tc * wt)], sem).wait()


def dispatch_rows(slot3, src, n_slots, tc):
    wt = WORD_TILES
    n = src.shape[0] // wt
    zeros = jnp.zeros((n_slots * wt, LANES), src.dtype)
    return pl.pallas_call(
        functools.partial(_dispatch_kernel, tc=tc),
        grid=(n // tc,),
        in_specs=[pl.BlockSpec((1, 1, tc * TOP_K), lambda i: (i, 0, 0), memory_space=pltpu.SMEM),
                  pl.BlockSpec((tc * wt, LANES), lambda i: (i, 0)),
                  pl.BlockSpec(memory_space=pl.ANY)],
        out_specs=pl.BlockSpec(memory_space=pl.ANY),
        out_shape=jax.ShapeDtypeStruct((n_slots * wt, LANES), src.dtype),
        scratch_shapes=[pltpu.SemaphoreType.DMA(())],
        input_output_aliases={2: 0},
        compiler_params=_cparams(("arbitrary",)),
        name="dispatch_scatter",
    )(slot3, src, zeros)


def _moe_kernel(be_ref, nused_ref, nchunk_ref, x_ref, wg_ref, wu_ref, bg_ref, bu_ref, wd_ref, bd_ref, o_ref,
                xb_ref, acc_ref, *, tm):
    i = pl.program_id(0)
    j = pl.program_id(1)
    nj = pl.num_programs(1)
    used = i < nused_ref[0]
    sub = tm // MOE_CHUNKS

    @pl.when(jnp.logical_and(used, j == 0))
    def _():
        for s in range(WORD_TILES):
            lo, hi = _unpack_words(_load_word_tile(x_ref, s, tm))
            xb_ref[:, 2 * s * LANES:(2 * s + 1) * LANES] = lo.astype(BF16)
            xb_ref[:, (2 * s + 1) * LANES:(2 * s + 2) * LANES] = hi.astype(BF16)
        acc_ref[...] = jnp.zeros_like(acc_ref)

    for c in range(1, MOE_CHUNKS + 1):
        @pl.when(jnp.logical_and(used, nchunk_ref[i] == c))
        def _(c=c):
            xb = xb_ref[0:c * sub, :]
            gt = jnp.dot(xb, wg_ref[0].astype(BF16), preferred_element_type=F32) + bg_ref[0]
            up = jnp.dot(xb, wu_ref[0].astype(BF16), preferred_element_type=F32) + bu_ref[0]
            gt = jnp.minimum(gt, SWIGLU_LIMIT)
            up = jnp.clip(up, -SWIGLU_LIMIT, SWIGLU_LIMIT)
            act = (up + 1.0) * gt * _sigmoid(SWIGLU_ALPHA * gt)
            acc_ref[0:c * sub, :] += jnp.dot(act.astype(BF16), wd_ref[0].astype(BF16),
                                             preferred_element_type=F32)

    @pl.when(jnp.logical_and(used, j == nj - 1))
    def _():
        _store_packed(o_ref, acc_ref[...] + bd_ref[0])

    @pl.when(jnp.logical_and(jnp.logical_not(used), j == nj - 1))
    def _():
        o_ref[...] = jnp.zeros_like(o_ref)


def moe_experts(block_expert, nused, block_chunks, xbuf, w_gate_up, b_gate_up, w_down, b_down, tm, tf=512):
    wt = WORD_TILES
    nslots = xbuf.shape[0] // wt
    ne, d, f2 = w_gate_up.shape
    f = f2 // 2
    tf = min(tf, f)
    nfj = f // tf
    nblk = nslots // tm
    b_gu3 = b_gate_up.reshape(ne, 1, f2)
    b_d3 = b_down.reshape(ne, 1, d)

    def jj(i, j, nu):
        return jnp.where(i < nu[0], j, nfj - 1)

    return pl.pallas_call(
        functools.partial(_moe_kernel, tm=tm),
        grid_spec=pltpu.PrefetchScalarGridSpec(
            num_scalar_prefetch=3,
            grid=(nblk, nfj),
            in_specs=[pl.BlockSpec((tm * wt, LANES), lambda i, j, be, nu, nc: (i, 0)),
                      pl.BlockSpec((1, d, tf), lambda i, j, be, nu, nc: (be[i], 0, jj(i, j, nu))),
                      pl.BlockSpec((1, d, tf), lambda i, j, be, nu, nc: (be[i], 0, nfj + jj(i, j, nu))),
                      pl.BlockSpec((1, 1, tf), lambda i, j, be, nu, nc: (be[i], 0, jj(i, j, nu))),
                      pl.BlockSpec((1, 1, tf), lambda i, j, be, nu, nc: (be[i], 0, nfj + jj(i, j, nu))),
                      pl.BlockSpec((1, tf, d), lambda i, j, be, nu, nc: (be[i], jj(i, j, nu), 0)),
                      pl.BlockSpec((1, 1, d), lambda i, j, be, nu, nc: (be[i], 0, 0))],
            out_specs=pl.BlockSpec((tm * wt, LANES), lambda i, j, be, nu, nc: (i, 0)),
            scratch_shapes=[pltpu.VMEM((tm, d), BF16), pltpu.VMEM((tm, d), F32)]),
        out_shape=jax.ShapeDtypeStruct((nslots * wt, LANES), U32),
        compiler_params=_cparams(("arbitrary", "arbitrary")),
        name="moe_experts",
    )(block_expert, nused, block_chunks, xbuf, w_gate_up, w_gate_up, b_gu3, b_gu3, w_down, b_d3)


def _combine_kernel(slot_ref, y_ref, tw_ref, x1_ref, mod_ref, lng_ref, lnb_ref, o_ref, yb_ref, sem, *, tc):
    _issue_token_copies(lambda a: slot_ref[0, 0, a], y_ref, yb_ref, sem, TOP_K * tc)
    pltpu.make_async_copy(y_ref.at[pl.ds(0, TOP_K * tc * WORD_TILES)], yb_ref, sem).wait()

    tw = tw_ref[...]
    wks = [tw[:, k:k + 1] for k in range(TOP_K)]
    cols = []
    for s in range(WORD_TILES):
        f_lo = jnp.zeros((tc, LANES), F32)
        f_hi = jnp.zeros((tc, LANES), F32)
        for k in range(TOP_K):
            lo, hi = _unpack_words(_load_word_tile(yb_ref, s, tc, base=k * tc))
            f_lo = f_lo + wks[k] * lo
            f_hi = f_hi + wks[k] * hi
        cols += [f_lo, f_hi]
    ffn = jnp.concatenate(cols, axis=-1)
    g2 = mod_ref[0, 0:1, :]
    o_ref[...] = _layer_norm(DN_ALPHA * x1_ref[...] + g2 * ffn, lng_ref[...], lnb_ref[...])


def combine(slot3, ybuf, top_w, x1, modg2, ln_g, ln_b, seq, tc=256):
    n, d = x1.shape
    tc = min(tc, seq)
    per_b = seq // tc
    return pl.pallas_call(
        functools.partial(_combine_kernel, tc=tc),
        grid=(n // tc,),
        in_specs=[pl.BlockSpec((1, 1, TOP_K * tc), lambda i: (i, 0, 0), memory_space=pltpu.SMEM),
                  pl.BlockSpec(memory_space=pl.ANY),
                  pl.BlockSpec((tc, LANES), lambda i: (i, 0)),
                  pl.BlockSpec((tc, d), lambda i: (i, 0)),
                  pl.BlockSpec((1, 1, d), lambda i: (i // per_b, 0, 0)),
                  pl.BlockSpec((1, d), lambda i: (0, 0)),
                  pl.BlockSpec((1, d), lambda i: (0, 0))],
        out_specs=pl.BlockSpec((tc, d), lambda i: (i, 0)),
        out_shape=jax.ShapeDtypeStruct((n, d), F32),
        scratch_shapes=[pltpu.VMEM((TOP_K * tc * WORD_TILES, LANES), U32), pltpu.SemaphoreType.DMA(())],
        compiler_params=_cparams(("arbitrary",)),
        name="combine",
    )(slot3, ybuf, top_w, x1, modg2, ln_g, ln_b)


def _split_w_in(w_in):
    offs = [0]
    for w in PROJ_WIDTHS:
        offs.append(offs[-1] + w)
    seg = [w_in[:, offs[k]:offs[k + 1]] for k in range(len(PROJ_WIDTHS))]
    (q_nsa, k_cmp, v_cmp, k_sel, v_sel, k_win, v_win, g_nsa, q_gla, k_gla, v_gla, a_gla, r_gla) = seg
    big = jnp.concatenate([q_nsa, v_gla, r_gla, k_cmp, v_cmp, k_sel, v_sel, k_win, v_win, q_gla, k_gla],
                          axis=1).astype(BF16)
    pad = lambda w: jnp.pad(w, ((0, 0), (0, LANES - w.shape[1])))
    small = jnp.concatenate([pad(g_nsa), pad(a_gla)], axis=1)
    return big, small


def _overlap_matrix(ncp, nsel):
    cs = jnp.arange(ncp)[:, None] * CMP_STRIDE
    ss = jnp.arange(nsel)[None, :] * SEL_BLOCK
    return jnp.logical_and(cs < ss + SEL_BLOCK, cs + CMP_BLOCK > ss).astype(F32)


MOE_TM = 768


def kernel(x, c, w_ada, b_ada, w_in, cmp_pe_k, cmp_w1_k, cmp_b1_k, cmp_w2_k, cmp_pe_v, cmp_w1_v, cmp_b1_v,
           cmp_w2_v, gla_w_gate, gla_b_gate, gla_norm_w, w_out, ln1_g, ln1_b, w_router, b_router, w_gate_up,
           b_gate_up, w_down, b_down, ln2_g, ln2_b):
    return _forward(x, c, w_ada, b_ada, w_in, cmp_pe_k, cmp_w1_k, cmp_b1_k, cmp_w2_k, cmp_pe_v, cmp_w1_v,
                    cmp_b1_v, cmp_w2_v, gla_w_gate, gla_b_gate, gla_norm_w, w_out, ln1_g, ln1_b, w_router,
                    b_router, w_gate_up, b_gate_up, w_down, b_down, ln2_g, ln2_b, moe_tm=MOE_TM)


def _forward(x, c, w_ada, b_ada, w_in, cmp_pe_k, cmp_w1_k, cmp_b1_k, cmp_w2_k, cmp_pe_v, cmp_w1_v, cmp_b1_v,
             cmp_w2_v, gla_w_gate, gla_b_gate, gla_norm_w, w_out, ln1_g, ln1_b, w_router, b_router, w_gate_up,
             b_gate_up, w_down, b_down, ln2_g, ln2_b, *, moe_tm):
    bsz, seq, d = x.shape
    n = bsz * seq
    l = 0
    x2 = x.reshape(n, d)

    mod = ada_mod(c, w_ada[l], b_ada[l])
    sh1, sc1, g1, sh2, sc2, g2 = jnp.split(mod, 6, axis=-1)
    mod1 = jnp.stack([sc1, sh1], axis=1)
    mod2 = jnp.stack([g1, sc2, sh2], axis=1)
    modg2 = g2[:, None, :]

    w_big, w_small = _split_w_in(w_in[l])
    p_buf = in_proj(x2, mod1, w_big, seq)
    s_buf = in_proj_small(x2, mod1, w_small, seq)

    ng = seq // CMP_STRIDE
    kvg = p_buf[:, P_KVCMP:P_KVCMP + 2 * KV_WIDTH].reshape(bsz, ng, CMP_STRIDE, 4, HEAD_DIM)
    kvg = kvg.transpose(0, 3, 1, 2, 4).reshape(bsz, 4, ng, CMP_STRIDE * HEAD_DIM)
    half = CMP_STRIDE * HEAD_DIM
    pe2 = jnp.stack([cmp_pe_k[l].reshape(2, half), cmp_pe_v[l].reshape(2, half)])
    w1 = jnp.stack([cmp_w1_k[l], cmp_w1_v[l]]).astype(BF16)
    b1 = jnp.stack([cmp_b1_k[l], cmp_b1_v[l]])[:, None, :]
    w2 = jnp.stack([cmp_w2_k[l], cmp_w2_v[l]]).astype(BF16)
    kvc = compress(kvg, pe2, w1, b1, w2)

    nsel = seq // SEL_BLOCK
    ovl = _overlap_matrix(ng, nsel)
    ocmp, imp = cmp_attention(p_buf, kvc, s_buf, ovl, bsz, seq)
    selb, blk_any = topk_select(imp.reshape(bsz * NSA_KV_HEADS, nsel, seq), qtile=NSA_TQ)
    y_nsa = sel_win_attention(p_buf, selb, blk_any, ocmp, s_buf, bsz, seq, tq=NSA_TQ)

    wg_pad = jnp.pad(gla_w_gate[l], ((0, LANES - GLA_GATE_RANK), (0, 0)))
    tril = jnp.tril(jnp.ones((GLA_CHUNK, GLA_CHUNK), F32))
    y_gla = gla_attention(p_buf, s_buf, wg_pad, gla_b_gate[l][None, :], gla_norm_w[l][None, :], tril, bsz, seq)

    tm_r = min(512, seq)
    ltri = jnp.tril(jnp.ones((tm_r, tm_r), F32), k=-1).astype(BF16)
    x1, hw, top_idx, top_w, rank, counts = out_proj_router(
        y_nsa, y_gla, w_out[l].astype(BF16), x2, mod2, ln1_g[l][None, :], ln1_b[l][None, :],
        w_router[l], b_router[l][None, :], ltri, seq, tm=tm_r)

    tm = moe_tm
    n_asg = n * TOP_K
    cnt = counts[0]
    padded = (cnt + tm - 1) // tm * tm
    ends_p = jnp.cumsum(padded)
    start_p = ends_p - padded
    idx4 = top_idx[:, :TOP_K]
    slot_of = (start_p[idx4] + rank[:, :TOP_K]).astype(I32)
    n_blocks = -(-n_asg // tm) + N_EXPERTS
    n_slots = n_blocks * tm
    nused = (ends_p[-1] // tm).astype(I32).reshape(1)
    blk_start = jnp.minimum(jnp.arange(n_blocks, dtype=I32), nused - 1) * tm
    block_expert = jnp.minimum(jnp.sum(ends_p[None, :] <= blk_start[:, None], axis=1), N_EXPERTS - 1).astype(I32)
    sub = tm // MOE_CHUNKS
    rows_left = cnt[block_expert] - (blk_start - start_p[block_expert])
    block_chunks = jnp.clip((rows_left + sub - 1) // sub, 1, MOE_CHUNKS).astype(I32)

    tc = min(256, seq)
    xbuf = dispatch_rows(slot_of.reshape(n // tc, 1, tc * TOP_K), hw, n_slots, tc)
    ybuf = moe_experts(block_expert, nused, block_chunks, xbuf, w_gate_up[l], b_gate_up[l], w_down[l], b_down[l], tm)

    slot3 = slot_of.reshape(n // tc, tc, TOP_K).transpose(0, 2, 1).reshape(n // tc, 1, TOP_K * tc)
    out = combine(slot3, ybuf, top_w, x1, modg2, ln2_g[l][None, :], ln2_b[l][None, :], seq, tc=tc)
    return out.reshape(bsz, seq, d)
```

```python
import functools

import jax
import jax.numpy as jnp
from jax import lax
from jax.experimental import pallas as pl
from jax.experimental.pallas import tpu as pltpu

F32 = jnp.float32
BF16 = jnp.bfloat16
U32 = jnp.uint32
I32 = jnp.int32
HIGHEST = lax.Precision.HIGHEST

D_MODEL = 2048
HEAD_DIM = 128
NSA_HEADS = 8
NSA_KV_HEADS = 2
NSA_GROUP = 4
NSA_WIDTH = 1024
KV_WIDTH = 256
CMP_BLOCK = 32
CMP_STRIDE = 16
CMP_HIDDEN = 256
SEL_BLOCK = 64
SEL_TOPK = 16
SEL_LOCAL = 2
WINDOW = 512
GLA_HEADS = 4
GLA_DV = 256
GLA_DK = 128
GLA_WIDTH = 1024
GLA_GATE_RANK = 16
GLA_GATE_TAU = 16.0
GLA_CHUNK = 64
GLA_SUB = 16
N_EXPERTS = 32
TOP_K = 4
D_FF = 2048
SWIGLU_LIMIT = 7.0
SWIGLU_ALPHA = 1.702
LN_EPS = 1e-5
RMS_EPS = 1e-6
NEG_INF = -1e30
FORCE_SCORE = 1e6
DN_ALPHA = 2.0 ** 0.25
PROJ_WIDTHS = (1024, 256, 256, 256, 256, 256, 256, 24, 512, 512, 1024, 16, 1024)

LANES = 128
VMEM_LIMIT = 56 * 1024 * 1024
WORD_TILES = D_MODEL // (2 * LANES)
GATHER_UNROLL = 8
NSA_TQ = 128
MOE_CHUNKS = 4
P_QNSA = 0
P_VGLA = 1024
P_RGLA = 2048
P_KVCMP = 3072
P_KSEL = 3584
P_VSEL = 3840
P_KWIN = 4096
P_VWIN = 4352
P_QGLA = 4608
P_KGLA = 5120
P_WIDTH = 5632
S_WIDTH = 256


def _cparams(sem, vmem=VMEM_LIMIT):
    return pltpu.CompilerParams(dimension_semantics=sem, vmem_limit_bytes=vmem)


def _sigmoid(x):
    return 1.0 / (1.0 + jnp.exp(-x))


def _ada_kernel(c_ref, w_ref, b_ref, o_ref):
    c = c_ref[...]
    sc = c * _sigmoid(c)
    o_ref[...] = jnp.dot(sc, w_ref[...], preferred_element_type=F32, precision=HIGHEST) + b_ref[...]


def ada_mod(c, w_ada, b_ada, tn=1024):
    bsz, d = c.shape
    n = w_ada.shape[1]
    return pl.pallas_call(
        _ada_kernel,
        grid=(n // tn,),
        in_specs=[pl.BlockSpec((bsz, d), lambda j: (0, 0)),
                  pl.BlockSpec((d, tn), lambda j: (0, j)),
                  pl.BlockSpec((1, tn), lambda j: (0, j))],
        out_specs=pl.BlockSpec((bsz, tn), lambda j: (0, j)),
        out_shape=jax.ShapeDtypeStruct((bsz, n), F32),
        compiler_params=_cparams(("arbitrary",)),
        name="ada_mod",
    )(c, w_ada, b_ada.reshape(1, n))


def _inproj_kernel(x_ref, mod_ref, w_ref, w2_ref, o_ref, s_ref):
    sc = mod_ref[0, 0:1, :]
    sh = mod_ref[0, 1:2, :]
    h = x_ref[...] * (1.0 + sc) + sh
    o_ref[...] = jnp.dot(h.astype(BF16), w_ref[...], preferred_element_type=F32).astype(o_ref.dtype)
    s_ref[...] = _dot_split(h, w2_ref)


def in_proj(x2, mod1, w_bf16, w_small_f32, seq, tm=512):
    n, d = x2.shape
    wn = w_bf16.shape[1]
    ws = w_small_f32.shape[1]
    tm = min(tm, seq)
    per_b = seq // tm
    return pl.pallas_call(
        _inproj_kernel,
        grid=(n // tm,),
        in_specs=[pl.BlockSpec((tm, d), lambda i: (i, 0)),
                  pl.BlockSpec((1, 2, d), lambda i: (i // per_b, 0, 0)),
                  pl.BlockSpec((d, wn), lambda i: (0, 0)),
                  pl.BlockSpec((d, 2 * ws), lambda i: (0, 0))],
        out_specs=[pl.BlockSpec((tm, wn), lambda i: (i, 0)),
                   pl.BlockSpec((tm, ws), lambda i: (i, 0))],
        out_shape=[jax.ShapeDtypeStruct((n, wn), BF16),
                   jax.ShapeDtypeStruct((n, ws), F32)],
        compiler_params=_cparams(("arbitrary",)),
        name="in_proj",
    )(x2, mod1, w_bf16, _split_weight(w_small_f32))


def _split_weight(w):
    hi = w.astype(BF16)
    lo = (w - hi.astype(F32)).astype(BF16)
    return jnp.concatenate([hi, lo], axis=1)


def _dot_split(a, w2_ref):
    n = w2_ref.shape[1] // 2
    a_hi = a.astype(BF16)
    a_lo = (a - a_hi.astype(F32)).astype(BF16)
    o2 = jnp.dot(a_hi, w2_ref[...], preferred_element_type=F32)
    return o2[:, 0:n] + o2[:, n:2 * n] + jnp.dot(a_lo, w2_ref[:, 0:n], preferred_element_type=F32)


def _gelu_tanh(x):
    return 0.5 * x * (1.0 + jnp.tanh(0.7978845608028654 * (x + 0.044715 * (x * x * x))))


def _compress_kernel(g_ref, pe_ref, w1_ref, b1_ref, w2_ref, o_ref):
    half = CMP_STRIDE * HEAD_DIM
    g = g_ref[0, 0].astype(F32)
    ga = (g + pe_ref[0, 0:1, :]).astype(BF16)
    gb = (g + pe_ref[0, 1:2, :]).astype(BF16)
    u = jnp.dot(ga, w1_ref[0, 0:half, :], preferred_element_type=F32)
    v = jnp.dot(gb, w1_ref[0, half:2 * half, :], preferred_element_type=F32)
    ng = u.shape[0]
    v_next = pltpu.roll(v, ng - 1, 0)
    h = _gelu_tanh(u + v_next + b1_ref[0])
    o_ref[0, 0] = jnp.dot(h.astype(BF16), w2_ref[0], preferred_element_type=F32).astype(o_ref.dtype)


def compress(kvg, pe2, w1, b1, w2):
    bsz, four, ng, gd = kvg.shape
    return pl.pallas_call(
        _compress_kernel,
        grid=(bsz, four),
        in_specs=[pl.BlockSpec((1, 1, ng, gd), lambda b, j: (b, j, 0, 0)),
                  pl.BlockSpec((1, 2, gd), lambda b, j: (j // 2, 0, 0)),
                  pl.BlockSpec((1, 2 * gd, CMP_HIDDEN), lambda b, j: (j // 2, 0, 0)),
                  pl.BlockSpec((1, 1, CMP_HIDDEN), lambda b, j: (j // 2, 0, 0)),
                  pl.BlockSpec((1, CMP_HIDDEN, HEAD_DIM), lambda b, j: (j // 2, 0, 0))],
        out_specs=pl.BlockSpec((1, 1, ng, HEAD_DIM), lambda b, j: (b, j, 0, 0)),
        out_shape=jax.ShapeDtypeStruct((bsz, four, ng, HEAD_DIM), BF16),
        compiler_params=_cparams(("arbitrary", "arbitrary")),
        name="compress",
    )(kvg, pe2, w1, b1, w2)


def _cmp_attn_kernel(q_ref, kc_ref, vct_ref, gate_ref, ovlt_ref, o_ref, imp_ref, *, tq):
    i = pl.program_id(1)
    ncp = kc_ref.shape[2]
    R = NSA_GROUP
    D = HEAD_DIM
    scale = D ** -0.5
    q0 = i * tq
    crow = lax.broadcasted_iota(I32, (ncp, 1), 0)
    pos_col = q0 + jnp.concatenate([lax.broadcasted_iota(I32, (1, tq), 1)] * R, axis=1)
    ok = (crow * CMP_STRIDE + (CMP_BLOCK - 1)) <= pos_col
    any_ok = (pos_col >= CMP_BLOCK - 1).astype(F32)
    a = (crow - q0 // CMP_STRIDE).astype(F32)
    a_hi = jnp.floor(a * (1.0 / 32.0))
    a_lo = a - 32.0 * a_hi
    lane = lax.broadcasted_iota(I32, (1, LANES), 1)
    sub = lax.broadcasted_iota(I32, (LANES, 1), 0)
    kaug = jnp.where(lane == 0, a_hi, jnp.where(lane == 1, a_lo, 0.0)).astype(BF16)
    gates_t = _sigmoid(gate_ref[...]).T
    q = q_ref[...]
    for g in range(NSA_KV_HEADS):
        qa_cols = []
        for r in range(R):
            h = g * R + r
            slope = 2.0 ** (-(h + 1))
            qs_t = (q[:, h * D:(h + 1) * D].astype(F32) * scale).T.astype(BF16)
            aug = jnp.where(sub == 0, 32.0 * CMP_STRIDE * slope, jnp.where(sub == 1, CMP_STRIDE * slope, 0.0))
            qa_cols.append(jnp.concatenate([qs_t, jnp.broadcast_to(aug, (D, tq)).astype(BF16)], axis=0))
        qa = jnp.concatenate(qa_cols, axis=1)
        ka = jnp.concatenate([kc_ref[0, g], kaug], axis=1)
        st = jnp.dot(ka, qa, preferred_element_type=F32)
        st = jnp.where(ok, st, NEG_INF)
        m = jnp.max(st, axis=0, keepdims=True)
        e = jnp.exp(st - m)
        p = e * (any_ok / jnp.sum(e, axis=0, keepdims=True))
        o_t = jnp.dot(vct_ref[0, g], p.astype(BF16), preferred_element_type=F32)
        psum_t = p[:, 0:tq]
        for r in range(1, R):
            psum_t = psum_t + p[:, r * tq:(r + 1) * tq]
        for r in range(R):
            h = g * R + r
            o_ref[g, :, r * tq:(r + 1) * tq] = o_t[:, r * tq:(r + 1) * tq] * gates_t[3 * h:3 * h + 1, :]
        imp_ref[g] = jnp.dot(ovlt_ref[...], psum_t, preferred_element_type=F32, precision=HIGHEST)


def cmp_attention(p_buf, kvc, s_buf, ovl, bsz, seq, tq=128):
    ncp = kvc.shape[2]
    nsel = ovl.shape[1]
    per_b = seq // tq
    G, D, C = NSA_KV_HEADS, HEAD_DIM, NSA_GROUP * tq
    kc = kvc[:, 0:G]
    vct = kvc[:, G:2 * G].transpose(0, 1, 3, 2)
    return pl.pallas_call(
        functools.partial(_cmp_attn_kernel, tq=tq),
        grid=(bsz, per_b),
        in_specs=[pl.BlockSpec((tq, NSA_WIDTH), lambda b, i: (b * per_b + i, P_QNSA // NSA_WIDTH)),
                  pl.BlockSpec((1, G, ncp, D), lambda b, i: (b, 0, 0, 0)),
                  pl.BlockSpec((1, G, D, ncp), lambda b, i: (b, 0, 0, 0)),
                  pl.BlockSpec((tq, LANES), lambda b, i: (b * per_b + i, 0)),
                  pl.BlockSpec((nsel, ncp), lambda b, i: (0, 0))],
        out_specs=[pl.BlockSpec((None, G, None, D, C), lambda b, i: (b, 0, i, 0, 0)),
                   pl.BlockSpec((None, G, nsel, tq), lambda b, i: (b, 0, 0, i))],
        out_shape=[jax.ShapeDtypeStruct((bsz, G, per_b, D, C), F32),
                   jax.ShapeDtypeStruct((bsz, G, nsel, seq), F32)],
        compiler_params=_cparams(("arbitrary", "arbitrary")),
        name="cmp_attention",
    )(p_buf, kc, vct, s_buf, ovl.T)


def _topk_kernel(imp_ref, o_ref, any_ref, *, n_topk, qtile):
    nsel, cols = imp_ref.shape
    pos = pl.program_id(1) * cols + lax.broadcasted_iota(I32, (1, cols), 1)
    cur = pos // SEL_BLOCK
    blk = lax.broadcasted_iota(I32, (nsel, 1), 0)
    blkf = blk.astype(F32)
    forced = jnp.logical_or(blk == 0, jnp.logical_and(blk <= cur, blk > cur - SEL_LOCAL))
    work = jnp.where(forced, FORCE_SCORE, jnp.where(blk > cur, -1.0, imp_ref[...]))
    sel = jnp.zeros((nsel, cols), F32)
    for _ in range(n_topk):
        m = jnp.max(work, axis=0, keepdims=True)
        first = jnp.min(jnp.where(work == m, blkf, float(nsel)), axis=0, keepdims=True)
        pick = blkf == first
        sel = jnp.where(pick, 1.0, sel)
        work = jnp.where(pick, -2.0, work)
    keep = jnp.logical_and(sel > 0.5, blk <= cur)
    res = jnp.where(keep, 0.0, NEG_INF)
    if nsel < LANES:
        res = jnp.concatenate([res, jnp.full((LANES - nsel, cols), NEG_INF, F32)], axis=0)
    o_ref[...] = res.astype(o_ref.dtype)
    hit = jnp.where(res == 0.0, 1.0, 0.0)
    lane = lax.broadcasted_iota(I32, (1, LANES), 1)
    out = jnp.zeros((LANES, LANES), F32)
    for qt in range(cols // qtile):
        col = jnp.max(hit[:, qt * qtile:(qt + 1) * qtile], axis=1, keepdims=True)
        out = jnp.where(lane == qt, col, out)
    any_ref[...] = out


def topk_select(imp_t, qtile, cols=1024):
    nbg, nsel, seq = imp_t.shape
    cols = min(cols, seq)
    nstep = seq // cols
    mask_t, hits = pl.pallas_call(
        functools.partial(_topk_kernel, n_topk=min(SEL_TOPK, nsel), qtile=qtile),
        grid=(nbg, nstep),
        in_specs=[pl.BlockSpec((None, nsel, cols), lambda b, j: (b, 0, j))],
        out_specs=[pl.BlockSpec((None, LANES, cols), lambda b, j: (b, 0, j)),
                   pl.BlockSpec((None, None, LANES, LANES), lambda b, j: (b, j, 0, 0))],
        out_shape=[jax.ShapeDtypeStruct((nbg, LANES, seq), BF16),
                   jax.ShapeDtypeStruct((nbg, nstep, LANES, LANES), F32)],
        compiler_params=_cparams(("arbitrary", "arbitrary")),
        name="topk_select",
    )(imp_t)
    per_step = cols // qtile
    blk_any = hits[:, :, :, :per_step].transpose(0, 1, 3, 2).reshape(nbg * nstep * per_step, LANES)
    return mask_t, blk_any


def _sel_win_kernel(ids_ref, cnt_ref, q_ref, ks_ref, vst_ref, kw_ref, vwt_ref, sb_ref, kau_ref, kad_ref, oc_ref,
                    gate_ref, o_ref, qau_ref, qad_ref, qaw_ref, acc_ref, s_ref, mt_ref, *, tq, tk, seq):
    g = pl.program_id(1)
    i = pl.program_id(2)
    q0 = i * tq
    scale = HEAD_DIM ** -0.5
    R = NSA_GROUP
    C = R * tq
    D = HEAD_DIM
    dn = (((1,), (1,)), ((), ()))
    gslope = jnp.where(g == 0, 1.0, 2.0 ** (-NSA_GROUP))
    slopes = [gslope * 2.0 ** (-(r + 1)) for r in range(R)]

    qrel_col = jnp.concatenate([lax.broadcasted_iota(I32, (1, tq), 1)] * R, axis=1)
    slope_col = jnp.concatenate([jnp.full((1, tq), 1.0, F32) * slopes[r] for r in range(R)], axis=1)

    q = q_ref[...]
    selb_t = sb_ref[...].astype(F32)
    lane = lax.broadcasted_iota(I32, (1, LANES), 1)
    sub = lax.broadcasted_iota(I32, (LANES, 1), 0)
    nrel = ((sub - q0 // SEL_BLOCK) * SEL_BLOCK).astype(F32)
    for r in range(R):
        cols = slice(r * tq, (r + 1) * tq)
        qs_t = (q[:, r * D:(r + 1) * D].astype(F32) * scale).T.astype(BF16)
        sb_t = selb_t + slopes[r] * nrel
        qau_ref[0:D, cols] = qs_t
        qau_ref[D:2 * D, cols] = jnp.where(sub == LANES - 1, slopes[r], sb_t).astype(BF16)
        qad_ref[0:D, cols] = qs_t
        qad_ref[D:2 * D, cols] = sb_t.astype(BF16)
        qaw_ref[0:D, cols] = qs_t
        aw = jnp.where(sub == 0, slopes[r] * SEL_BLOCK, jnp.where(sub == 1, slopes[r], 0.0))
        qaw_ref[D:2 * D, cols] = jnp.broadcast_to(aw, (D, tq)).astype(BF16)

    acc_ref[...] = jnp.zeros_like(acc_ref)

    def online_update(st, t, m, l):
        m_new = jnp.maximum(m, jnp.max(st, axis=0, keepdims=True))
        alpha = jnp.exp(m - m_new)
        p = jnp.exp(st - m_new)
        l_new = alpha * l + jnp.sum(p, axis=0, keepdims=True)
        pv = jnp.dot(vst_ref[t], p.astype(BF16), preferred_element_type=F32)
        acc_ref[...] = acc_ref[...] * alpha + pv
        return m_new, l_new

    S_SPARE, S_NEGINF = 2, 3

    @pl.when(jnp.logical_and(pl.program_id(0) == 0, jnp.logical_and(g == 0, i == 0)))
    def _():
        s_ref[S_NEGINF] = jnp.full((tk, C), -jnp.inf, F32)

    t_last = q0 // tk
    step = (pl.program_id(0) * pl.num_programs(1) + g) * pl.num_programs(2) + i
    cnt = cnt_ref[step]
    id0 = step * (seq // tk)

    def pipelined(u, carry):
        m, l = carry
        tb = ids_ref[id0 + jnp.maximum(u - 1, 0)]
        st = s_ref[jnp.where(u >= 1, (u + 1) % 2, S_NEGINF)]
        m_new = jnp.maximum(m, mt_ref[0:1, :])
        alpha = jnp.exp(m - m_new)
        p = jnp.exp(st - m_new)
        l_new = alpha * l + jnp.sum(p, axis=0, keepdims=True)
        pv = jnp.dot(vst_ref[tb], p.astype(BF16), preferred_element_type=F32)
        acc_ref[...] = acc_ref[...] * alpha + pv
        ta = ids_ref[id0 + jnp.minimum(u, cnt - 1)]
        k0 = pl.multiple_of(ta * tk, tk)
        ka = jnp.concatenate([ks_ref[pl.ds(k0, tk), :], kau_ref[pl.ds(k0, tk), :]], axis=1)
        sa = jnp.dot(ka, qau_ref[...], preferred_element_type=F32)
        s_ref[jnp.where(u < cnt, u % 2, S_SPARE)] = sa
        mt_ref[...] = jnp.broadcast_to(jnp.max(sa, axis=0, keepdims=True), mt_ref.shape)
        return m_new, l_new

    m0 = jnp.full((1, C), NEG_INF, F32)
    l0 = jnp.zeros((1, C), F32)
    mt_ref[...] = jnp.full(mt_ref.shape, -jnp.inf, F32)
    m, l = lax.fori_loop(0, jnp.where(cnt > 0, cnt + 1, 0), pipelined, (m0, l0))

    k0 = pl.multiple_of(t_last * tk, tk)
    ka = jnp.concatenate([ks_ref[pl.ds(k0, tk), :], kad_ref[pl.ds(k0, tk), :]], axis=1)
    st = jnp.dot(ka, qad_ref[...], preferred_element_type=F32)
    krow = lax.broadcasted_iota(I32, (tk, 1), 0)
    koff = (krow % SEL_BLOCK).astype(F32)
    st = st + koff * slope_col
    st = jnp.where(krow + (k0 - q0) <= qrel_col, st, NEG_INF)
    m, l = online_update(st, t_last, m, l)
    inv_l = 1.0 / l

    wk = WINDOW + tq
    w0 = pl.multiple_of(jnp.maximum(q0 - WINDOW, 0), tq)
    wrow = lax.broadcasted_iota(I32, (wk, 1), 0) + (w0 - q0)
    w_hi = jnp.floor(wrow.astype(F32) * (1.0 / SEL_BLOCK))
    w_lo = wrow.astype(F32) - w_hi * SEL_BLOCK
    kaw = jnp.where(lane == 0, w_hi, jnp.where(lane == 1, w_lo, 0.0)).astype(BF16)
    kwa = jnp.concatenate([kw_ref[pl.ds(w0, wk), :], kaw], axis=1)
    sw = jnp.dot(kwa, qaw_ref[...], preferred_element_type=F32)
    dist = qrel_col - wrow
    sw = jnp.where(jnp.logical_and(dist >= 0, dist < WINDOW), sw, NEG_INF)
    mw = jnp.max(sw, axis=0, keepdims=True)
    pw = jnp.exp(sw - mw)
    inv_lw = 1.0 / jnp.sum(pw, axis=0, keepdims=True)
    c0 = w0 // tq
    vwt = jnp.concatenate([vwt_ref[c0 + c] for c in range(wk // tq)], axis=1)
    ow = jnp.dot(vwt, pw.astype(BF16), preferred_element_type=F32)

    gates_t = _sigmoid(gate_ref[...]).T
    osel = acc_ref[...] * inv_l
    ow = ow * inv_lw
    for r in range(R):
        cols = slice(r * tq, (r + 1) * tq)
        g_sel = jnp.where(g == 0, gates_t[3 * r + 1:3 * r + 2, :], gates_t[3 * (R + r) + 1:3 * (R + r) + 2, :])
        g_win = jnp.where(g == 0, gates_t[3 * r + 2:3 * r + 3, :], gates_t[3 * (R + r) + 2:3 * (R + r) + 3, :])
        y_t = oc_ref[:, cols] + g_sel * osel[:, cols] + g_win * ow[:, cols]
        y = y_t.T
        o_ref[:, r * D:(r + 1) * D] = y.astype(o_ref.dtype)


def _key_side_constants(seq):
    nsel = seq // SEL_BLOCK
    key = jnp.arange(seq)
    blk = key // SEL_BLOCK
    lane = jnp.arange(LANES)[None, :]
    onehot = (lane == blk[:, None])
    kad = onehot.astype(BF16)
    kau = jnp.where(lane == LANES - 1, (key % SEL_BLOCK)[:, None].astype(F32),
                    jnp.logical_and(onehot, lane < nsel - SEL_LOCAL).astype(F32)).astype(BF16)
    return kau, kad


def _active_key_tiles(blk_any, seq, tq, tk):
    nt = seq // tk
    per_b = seq // tq
    bpt = tk // SEL_BLOCK
    tile_any = blk_any[:, :nt * bpt].reshape(-1, nt, bpt).max(axis=-1) > 0.5
    t_last = (jnp.arange(blk_any.shape[0], dtype=I32) % per_b) * tq // tk
    tile_idx = jnp.arange(nt, dtype=I32)[None, :]
    active = jnp.logical_and(tile_any, tile_idx < t_last[:, None])
    ids = jnp.argsort(jnp.where(active, tile_idx, nt + tile_idx), axis=-1).astype(I32)
    return ids.reshape(-1), active.sum(axis=-1).astype(I32)


def sel_win_attention(p_buf, selb, blk_any, ocmp, s_buf, bsz, seq, tq=128, tk=512):
    n = p_buf.shape[0]
    tk = min(tk, seq)
    per_b = seq // tq
    gw = NSA_GROUP * HEAD_DIM
    G = NSA_KV_HEADS
    D = HEAD_DIM
    kau, kad = _key_side_constants(seq)
    if blk_any.shape[1] < (seq // tk) * (tk // SEL_BLOCK):
        blk_any = jnp.pad(blk_any, ((0, 0), (0, (seq // tk) * (tk // SEL_BLOCK) - blk_any.shape[1])))
    tile_ids, tile_cnt = _active_key_tiles(blk_any, seq, tq, tk)
    vsel = p_buf[:, P_VSEL:P_VSEL + KV_WIDTH].reshape(bsz, seq // tk, tk, G, D).transpose(0, 3, 1, 4, 2)
    vwin = p_buf[:, P_VWIN:P_VWIN + KV_WIDTH].reshape(bsz, seq // tq, tq, G, D).transpose(0, 3, 1, 4, 2)

    def colblk(off):
        return lambda b, g, i, ids, cnt: (b, off // D + g)

    return pl.pallas_call(
        functools.partial(_sel_win_kernel, tq=tq, tk=tk, seq=seq),
        grid_spec=pltpu.PrefetchScalarGridSpec(
            num_scalar_prefetch=2,
            grid=(bsz, G, per_b),
            in_specs=[pl.BlockSpec((tq, gw), lambda b, g, i, ids, cnt: (b * per_b + i, P_QNSA // gw + g)),
                      pl.BlockSpec((seq, D), colblk(P_KSEL)),
                      pl.BlockSpec((None, None, seq // tk, D, tk), lambda b, g, i, ids, cnt: (b, g, 0, 0, 0)),
                      pl.BlockSpec((seq, D), colblk(P_KWIN)),
                      pl.BlockSpec((None, None, seq // tq, D, tq), lambda b, g, i, ids, cnt: (b, g, 0, 0, 0)),
                      pl.BlockSpec((None, LANES, tq), lambda b, g, i, ids, cnt: (b * G + g, 0, i)),
                      pl.BlockSpec((seq, LANES), lambda b, g, i, ids, cnt: (0, 0)),
                      pl.BlockSpec((seq, LANES), lambda b, g, i, ids, cnt: (0, 0)),
                      pl.BlockSpec((None, None, None, D, gw), lambda b, g, i, ids, cnt: (b, g, i, 0, 0)),
                      pl.BlockSpec((tq, LANES), lambda b, g, i, ids, cnt: (b * per_b + i, 0))],
            out_specs=pl.BlockSpec((tq, gw), lambda b, g, i, ids, cnt: (b * per_b + i, g)),
            scratch_shapes=[pltpu.VMEM((2 * D, NSA_GROUP * tq), BF16),
                            pltpu.VMEM((2 * D, NSA_GROUP * tq), BF16),
                            pltpu.VMEM((2 * D, NSA_GROUP * tq), BF16),
                            pltpu.VMEM((D, NSA_GROUP * tq), F32),
                            pltpu.VMEM((4, tk, NSA_GROUP * tq), F32),
                            pltpu.VMEM((8, NSA_GROUP * tq), F32)]),
        out_shape=jax.ShapeDtypeStruct((n, NSA_WIDTH), BF16),
        compiler_params=_cparams(("arbitrary", "arbitrary", "arbitrary")),
        name="sel_win_attention",
    )(tile_ids, tile_cnt, p_buf, p_buf, vsel, p_buf, vwin, selb, kau, kad, ocmp, s_buf)


def _gla_kernel(q_ref, k_ref, v_ref, r_ref, a_ref, wg_ref, bg_ref, nw_ref, tril_ref, o_ref,
                st_ref, b_ref, *, tc):
    C, SUB, H = GLA_CHUNK, GLA_SUB, GLA_HEADS

    @pl.when(pl.program_id(1) == 0)
    def _():
        st_ref[...] = jnp.zeros_like(st_ref)

    x = jnp.dot(a_ref[...], wg_ref[...], preferred_element_type=F32, precision=HIGHEST) + bg_ref[...]
    la = (jnp.minimum(x, 0.0) - jnp.log(1.0 + jnp.exp(-jnp.abs(x)))) * (1.0 / GLA_GATE_TAU)
    tril = tril_ref[...]
    for c in range(tc // C):
        b_ref[c * C:(c + 1) * C, :] = jnp.dot(tril, la[c * C:(c + 1) * C, :],
                                              preferred_element_type=F32, precision=HIGHEST)

    row16 = lax.broadcasted_iota(I32, (SUB, C), 0)
    lane64 = lax.broadcasted_iota(I32, (SUB, C), 1)
    nw = nw_ref[...]

    def chunk(c, carry):
        r0 = pl.multiple_of(c * C, C)
        for h in range(H):
            bh = b_ref[pl.ds(r0, C), h * GLA_DK:(h + 1) * GLA_DK]
            qc = q_ref[pl.ds(r0, C), h * GLA_DK:(h + 1) * GLA_DK].astype(F32) * (GLA_DK ** -0.5)
            kc = k_ref[pl.ds(r0, C), h * GLA_DK:(h + 1) * GLA_DK].astype(F32)
            vc = v_ref[pl.ds(r0, C), h * GLA_DV:(h + 1) * GLA_DV]
            st = st_ref[h]
            b_last = bh[C - 1:C, :]
            qe = (qc * jnp.exp(bh)).astype(BF16)
            o = lax.dot_general(qe, st.astype(BF16), (((1,), (1,)), ((), ())),
                                preferred_element_type=F32)
            a_rows = []
            for sb in range(C // SUB):
                s0 = sb * SUB
                bi = bh[s0:s0 + SUB, :]
                qi = qc[s0:s0 + SUB, :]
                ki = kc[s0:s0 + SUB, :]
                beta = bh[s0:s0 + 1, :]
                if sb > 0:
                    qd = (qi * jnp.exp(bi - beta)).astype(BF16)
                    kd = (kc * jnp.exp(jnp.minimum(beta - bh, 0.0))).astype(BF16)
                    a_i = lax.dot_general(qd, kd, (((1,), (1,)), ((), ())), preferred_element_type=F32)
                    a_i = jnp.where(lane64 < s0, a_i, 0.0)
                else:
                    a_i = jnp.zeros((SUB, C), F32)
                for s in range(SUB):
                    e = jnp.exp(jnp.minimum(bi - bi[s:s + 1, :], 0.0))
                    col = jnp.sum(qi * ki[s:s + 1, :] * e, axis=-1, keepdims=True)
                    a_i = jnp.where(jnp.logical_and(lane64 == s0 + s, row16 >= s), col, a_i)
                a_rows.append(a_i)
            a_mat = jnp.concatenate(a_rows, axis=0)
            o = o + jnp.dot(a_mat.astype(BF16), vc, preferred_element_type=F32)
            kdec = (kc * jnp.exp(b_last - bh)).astype(BF16)
            upd = lax.dot_general(vc, kdec, (((0,), (0,)), ((), ())), preferred_element_type=F32)
            st_ref[h] = st * jnp.exp(b_last) + upd
            rms = lax.rsqrt(jnp.mean(o * o, axis=-1, keepdims=True) + RMS_EPS)
            rr = r_ref[pl.ds(r0, C), h * GLA_DV:(h + 1) * GLA_DV].astype(F32)
            y = o * rms * nw * (rr * _sigmoid(rr))
            o_ref[pl.ds(r0, C), h * GLA_DV:(h + 1) * GLA_DV] = y.astype(o_ref.dtype)
        return carry

    lax.fori_loop(0, tc // C, chunk, 0)


def gla_attention(p_buf, s_buf, wg_pad, bg, nw, tril, bsz, seq, tc=512):
    n = p_buf.shape[0]
    tc = min(tc, seq)
    per_b = seq // tc
    kw = GLA_HEADS * GLA_DK
    return pl.pallas_call(
        functools.partial(_gla_kernel, tc=tc),
        grid=(bsz, per_b),
        in_specs=[pl.BlockSpec((tc, kw), lambda b, j: (b * per_b + j, P_QGLA // kw)),
                  pl.BlockSpec((tc, kw), lambda b, j: (b * per_b + j, P_KGLA // kw)),
                  pl.BlockSpec((tc, GLA_WIDTH), lambda b, j: (b * per_b + j, P_VGLA // GLA_WIDTH)),
                  pl.BlockSpec((tc, GLA_WIDTH), lambda b, j: (b * per_b + j, P_RGLA // GLA_WIDTH)),
                  pl.BlockSpec((tc, LANES), lambda b, j: (b * per_b + j, 1)),
                  pl.BlockSpec((LANES, kw), lambda b, j: (0, 0)),
                  pl.BlockSpec((1, kw), lambda b, j: (0, 0)),
                  pl.BlockSpec((1, GLA_DV), lambda b, j: (0, 0)),
                  pl.BlockSpec((GLA_CHUNK, GLA_CHUNK), lambda b, j: (0, 0))],
        out_specs=pl.BlockSpec((tc, GLA_WIDTH), lambda b, j: (b * per_b + j, 0)),
        out_shape=jax.ShapeDtypeStruct((n, GLA_WIDTH), BF16),
        scratch_shapes=[pltpu.VMEM((GLA_HEADS, GLA_DV, GLA_DK), F32),
                        pltpu.VMEM((tc, kw), F32)],
        compiler_params=_cparams(("arbitrary", "arbitrary")),
        name="gla_attention",
    )(p_buf, p_buf, p_buf, p_buf, s_buf, wg_pad, bg, nw, tril)


def _pack_words(lo, hi):
    lo_b = pltpu.bitcast(lo.astype(BF16).astype(F32), U32)
    hi_b = pltpu.bitcast(hi.astype(BF16).astype(F32), U32)
    return jnp.bitwise_or(hi_b, jnp.right_shift(lo_b, jnp.uint32(16)))


def _unpack_words(w):
    lo = pltpu.bitcast(jnp.left_shift(w, jnp.uint32(16)), F32)
    hi = pltpu.bitcast(jnp.bitwise_and(w, jnp.uint32(0xFFFF0000)), F32)
    return lo, hi


def _store_packed(ref, v, first_tile=0):
    rows = v.shape[0]
    n_tiles = ref.shape[0] // rows
    for u in range(v.shape[1] // (2 * LANES)):
        w = _pack_words(v[:, 2 * u * LANES:(2 * u + 1) * LANES], v[:, (2 * u + 1) * LANES:(2 * u + 2) * LANES])
        ref[pl.ds(first_tile + u, rows, stride=n_tiles), :] = w


def _load_word_tile(ref, s, rows, base=0):
    n_tiles = WORD_TILES
    return ref[pl.ds(base * n_tiles + s, rows, stride=n_tiles), :]


def _layer_norm(z, g, b):
    mu = jnp.mean(z, axis=-1, keepdims=True)
    zc = z - mu
    var = jnp.mean(zc * zc, axis=-1, keepdims=True)
    return zc * lax.rsqrt(var + LN_EPS) * g + b


def _outproj_kernel(yn_ref, yg_ref, wo_ref, x_ref, mod_ref, lng_ref, lnb_ref, wr_ref, br_ref, ltri_ref,
                    x1_ref, hw_ref, idx_ref, tw_ref, rank_ref, cnt_ref, carry_ref):
    step = pl.program_id(0)

    @pl.when(step == 0)
    def _():
        carry_ref[...] = jnp.zeros_like(carry_ref)

    half = yn_ref.shape[1]
    mix = (jnp.dot(yn_ref[...], wo_ref[0:half, :], preferred_element_type=F32) +
           jnp.dot(yg_ref[...], wo_ref[half:2 * half, :], preferred_element_type=F32))
    g1 = mod_ref[0, 0:1, :]
    sc2 = mod_ref[0, 1:2, :]
    sh2 = mod_ref[0, 2:3, :]
    x1 = _layer_norm(DN_ALPHA * x_ref[...] + g1 * mix, lng_ref[...], lnb_ref[...])
    x1_ref[...] = x1
    h2 = x1 * (1.0 + sc2) + sh2
    _store_packed(hw_ref, h2)

    tm = h2.shape[0]
    ne = br_ref.shape[1]
    logits = _dot_split(h2, wr_ref)[:, 0:ne] + br_ref[...]
    lane = lax.broadcasted_iota(I32, (1, ne), 1).astype(F32)
    out_lane = lax.broadcasted_iota(I32, (1, LANES), 1)
    work = logits
    idx_out = jnp.zeros((tm, LANES), F32)
    w_out = jnp.zeros((tm, LANES), F32)
    onehot = jnp.zeros((tm, ne), F32)
    picks = []
    top0 = None
    den = jnp.zeros((tm, 1), F32)
    for k in range(TOP_K):
        m = jnp.max(work, axis=-1, keepdims=True)
        first = jnp.min(jnp.where(work == m, lane, float(ne)), axis=-1, keepdims=True)
        pick = lane == first
        if k == 0:
            top0 = m
        e = jnp.exp(m - top0)
        den = den + e
        idx_out = jnp.where(out_lane == k, first, idx_out)
        w_out = jnp.where(out_lane == k, e, w_out)
        onehot = jnp.where(pick, 1.0, onehot)
        picks.append(pick)
        work = jnp.where(pick, -jnp.inf, work)
    idx_ref[...] = idx_out.astype(I32)
    tw_ref[...] = w_out * (1.0 / den)

    before = jnp.dot(ltri_ref[...], onehot.astype(BF16), preferred_element_type=F32) + carry_ref[0:1, :]
    rank_out = jnp.zeros((tm, LANES), F32)
    for k in range(TOP_K):
        rk = jnp.sum(jnp.where(picks[k], before, 0.0), axis=-1, keepdims=True)
        rank_out = jnp.where(out_lane == k, rk, rank_out)
    rank_ref[...] = rank_out.astype(I32)
    carry_ref[...] = carry_ref[...] + jnp.sum(onehot, axis=0, keepdims=True)
    cnt_ref[...] = carry_ref[...].astype(I32)


def out_proj_router(ynsa, ygla, wo_bf16, x2, mod2, ln_g, ln_b, w_router, b_router, ltri, seq, tm=512):
    n, d = x2.shape
    tm = min(tm, seq)
    per_b = seq // tm
    ne = w_router.shape[1]
    wr2 = _split_weight(jnp.pad(w_router, ((0, 0), (0, LANES - ne))))
    row = lambda i: (i, 0)
    const = lambda i: (0, 0)
    return pl.pallas_call(
        _outproj_kernel,
        grid=(n // tm,),
        in_specs=[pl.BlockSpec((tm, NSA_WIDTH), row),
                  pl.BlockSpec((tm, GLA_WIDTH), row),
                  pl.BlockSpec((d, d), const),
                  pl.BlockSpec((tm, d), row),
                  pl.BlockSpec((1, 3, d), lambda i: (i // per_b, 0, 0)),
                  pl.BlockSpec((1, d), const),
                  pl.BlockSpec((1, d), const),
                  pl.BlockSpec((d, 2 * LANES), const),
                  pl.BlockSpec((1, ne), const),
                  pl.BlockSpec((tm, tm), const)],
        out_specs=[pl.BlockSpec((tm, d), row),
                   pl.BlockSpec((tm * WORD_TILES, LANES), row),
                   pl.BlockSpec((tm, LANES), row),
                   pl.BlockSpec((tm, LANES), row),
                   pl.BlockSpec((tm, LANES), row),
                   pl.BlockSpec((8, ne), const)],
        out_shape=[jax.ShapeDtypeStruct((n, d), F32),
                   jax.ShapeDtypeStruct((n * WORD_TILES, LANES), U32),
                   jax.ShapeDtypeStruct((n, LANES), I32),
                   jax.ShapeDtypeStruct((n, LANES), F32),
                   jax.ShapeDtypeStruct((n, LANES), I32),
                   jax.ShapeDtypeStruct((8, ne), I32)],
        scratch_shapes=[pltpu.VMEM((8, ne), F32)],
        compiler_params=_cparams(("arbitrary",)),
        name="out_proj_router",
    )(ynsa, ygla, wo_bf16, x2, mod2, ln_g, ln_b, wr2, b_router, ltri)


def _issue_token_copies(idx_at, src_ref, dst_ref, sem, count):
    wt = WORD_TILES

    def issue(r8, carry):
        for u in range(GATHER_UNROLL):
            r = r8 * GATHER_UNROLL + u
            tok = idx_at(r)
            pltpu.make_async_copy(src_ref.at[pl.ds(pl.multiple_of(tok * wt, wt), wt)],
                                  dst_ref.at[pl.ds(pl.multiple_of(r * wt, wt), wt)], sem).start(priority=u % 2)
        return carry

    lax.fori_loop(0, count // GATHER_UNROLL, issue, 0)


def _dispatch_kernel(slot_ref, src_ref, zero_ref, o_ref, sem, *, tc):
    del zero_ref
    wt = WORD_TILES

    def issue(t8, carry):
        for u in range(GATHER_UNROLL // TOP_K):
            t = t8 * (GATHER_UNROLL // TOP_K) + u
            for k in range(TOP_K):
                slot = slot_ref[0, 0, t * TOP_K + k]
                pltpu.make_async_copy(src_ref.at[pl.ds(pl.multiple_of(t * wt, wt), wt)],
                                      o_ref.at[pl.ds(pl.multiple_of(slot * wt, wt), wt)], sem).start(priority=k % 2)
        return carry

    lax.fori_loop(0, tc * TOP_K // GATHER_UNROLL, issue, 0)
    for k in range(TOP_K):
        pltpu.make_async_copy(src_ref, o_ref.at[pl.ds(0, tc * wt)], sem).wait()


def dispatch_rows(slot3, src, n_slots, tc):
    wt = WORD_TILES
    n = src.shape[0] // wt
    zeros = jnp.zeros((n_slots * wt, LANES), src.dtype)
    return pl.pallas_call(
        functools.partial(_dispatch_kernel, tc=tc),
        grid=(n // tc,),
        in_specs=[pl.BlockSpec((1, 1, tc * TOP_K), lambda i: (i, 0, 0), memory_space=pltpu.SMEM),
                  pl.BlockSpec((tc * wt, LANES), lambda i: (i, 0)),
                  pl.BlockSpec(memory_space=pl.ANY)],
        out_specs=pl.BlockSpec(memory_space=pl.ANY),
        out_shape=jax.ShapeDtypeStruct((n_slots * wt, LANES), src.dtype),
        scratch_shapes=[pltpu.SemaphoreType.DMA(())],
        input_output_aliases={2: 0},
        compiler_params=_cparams(("arbitrary",)),
        name="dispatch_scatter",
    )(slot3, src, zeros)


def _moe_kernel(be_ref, nused_ref, nchunk_ref, x_ref, wg_ref, wu_ref, bg_ref, bu_ref, wd_ref, bd_ref, o_ref,
                xb_ref, acc_ref, *, tm):
    i = pl.program_id(0)
    j = pl.program_id(1)
    nj = pl.num_programs(1)
    used = i < nused_ref[0]
    sub = tm // MOE_CHUNKS

    @pl.when(jnp.logical_and(used, j == 0))
    def _():
        for s in range(WORD_TILES):
            lo, hi = _unpack_words(_load_word_tile(x_ref, s, tm))
            xb_ref[:, 2 * s * LANES:(2 * s + 1) * LANES] = lo.astype(BF16)
            xb_ref[:, (2 * s + 1) * LANES:(2 * s + 2) * LANES] = hi.astype(BF16)
        acc_ref[...] = jnp.zeros_like(acc_ref)

    for c in range(1, MOE_CHUNKS + 1):
        @pl.when(jnp.logical_and(used, nchunk_ref[i] == c))
        def _(c=c):
            xb = xb_ref[0:c * sub, :]
            gt = jnp.dot(xb, wg_ref[0].astype(BF16), preferred_element_type=F32) + bg_ref[0]
            up = jnp.dot(xb, wu_ref[0].astype(BF16), preferred_element_type=F32) + bu_ref[0]
            gt = jnp.minimum(gt, SWIGLU_LIMIT)
            up = jnp.clip(up, -SWIGLU_LIMIT, SWIGLU_LIMIT)
            act = (up + 1.0) * gt * _sigmoid(SWIGLU_ALPHA * gt)
            acc_ref[0:c * sub, :] += jnp.dot(act.astype(BF16), wd_ref[0].astype(BF16),
                                             preferred_element_type=F32)

    @pl.when(jnp.logical_and(used, j == nj - 1))
    def _():
        _store_packed(o_ref, acc_ref[...] + bd_ref[0])

    @pl.when(jnp.logical_and(jnp.logical_not(used), j == nj - 1))
    def _():
        o_ref[...] = jnp.zeros_like(o_ref)


def moe_experts(block_expert, nused, block_chunks, xbuf, w_gate_up, b_gate_up, w_down, b_down, tm, tf=512):
    wt = WORD_TILES
    nslots = xbuf.shape[0] // wt
    ne, d, f2 = w_gate_up.shape
    f = f2 // 2
    tf = min(tf, f)
    nfj = f // tf
    nblk = nslots // tm
    b_gu3 = b_gate_up.reshape(ne, 1, f2)
    b_d3 = b_down.reshape(ne, 1, d)

    def jj(i, j, nu):
        return jnp.where(i < nu[0], j, nfj - 1)

    return pl.pallas_call(
        functools.partial(_moe_kernel, tm=tm),
        grid_spec=pltpu.PrefetchScalarGridSpec(
            num_scalar_prefetch=3,
            grid=(nblk, nfj),
            in_specs=[pl.BlockSpec((tm * wt, LANES), lambda i, j, be, nu, nc: (i, 0)),
                      pl.BlockSpec((1, d, tf), lambda i, j, be, nu, nc: (be[i], 0, jj(i, j, nu))),
                      pl.BlockSpec((1, d, tf), lambda i, j, be, nu, nc: (be[i], 0, nfj + jj(i, j, nu))),
                      pl.BlockSpec((1, 1, tf), lambda i, j, be, nu, nc: (be[i], 0, jj(i, j, nu))),
                      pl.BlockSpec((1, 1, tf), lambda i, j, be, nu, nc: (be[i], 0, nfj + jj(i, j, nu))),
                      pl.BlockSpec((1, tf, d), lambda i, j, be, nu, nc: (be[i], jj(i, j, nu), 0)),
                      pl.BlockSpec((1, 1, d), lambda i, j, be, nu, nc: (be[i], 0, 0))],
            out_specs=pl.BlockSpec((tm * wt, LANES), lambda i, j, be, nu, nc: (i, 0)),
            scratch_shapes=[pltpu.VMEM((tm, d), BF16), pltpu.VMEM((tm, d), F32)]),
        out_shape=jax.ShapeDtypeStruct((nslots * wt, LANES), U32),
        compiler_params=_cparams(("arbitrary", "arbitrary")),
        name="moe_experts",
    )(block_expert, nused, block_chunks, xbuf, w_gate_up, w_gate_up, b_gu3, b_gu3, w_down, b_d3)


def _combine_kernel(slot_ref, y_ref, tw_ref, x1_ref, mod_ref, lng_ref, lnb_ref, o_ref, yb_ref, sem, *, tc):
    _issue_token_copies(lambda a: slot_ref[0, 0, a], y_ref, yb_ref, sem, TOP_K * tc)
    pltpu.make_async_copy(y_ref.at[pl.ds(0, TOP_K * tc * WORD_TILES)], yb_ref, sem).wait()

    tw = tw_ref[...]
    wks = [tw[:, k:k + 1] for k in range(TOP_K)]
    cols = []
    for s in range(WORD_TILES):
        f_lo = jnp.zeros((tc, LANES), F32)
        f_hi = jnp.zeros((tc, LANES), F32)
        for k in range(TOP_K):
            lo, hi = _unpack_words(_load_word_tile(yb_ref, s, tc, base=k * tc))
            f_lo = f_lo + wks[k] * lo
            f_hi = f_hi + wks[k] * hi
        cols += [f_lo, f_hi]
    ffn = jnp.concatenate(cols, axis=-1)
    g2 = mod_ref[0, 0:1, :]
    o_ref[...] = _layer_norm(DN_ALPHA * x1_ref[...] + g2 * ffn, lng_ref[...], lnb_ref[...])


def combine(slot3, ybuf, top_w, x1, modg2, ln_g, ln_b, seq, tc=256):
    n, d = x1.shape
    tc = min(tc, seq)
    per_b = seq // tc
    return pl.pallas_call(
        functools.partial(_combine_kernel, tc=tc),
        grid=(n // tc,),
        in_specs=[pl.BlockSpec((1, 1, TOP_K * tc), lambda i: (i, 0, 0), memory_space=pltpu.SMEM),
                  pl.BlockSpec(memory_space=pl.ANY),
                  pl.BlockSpec((tc, LANES), lambda i: (i, 0)),
                  pl.BlockSpec((tc, d), lambda i: (i, 0)),
                  pl.BlockSpec((1, 1, d), lambda i: (i // per_b, 0, 0)),
                  pl.BlockSpec((1, d), lambda i: (0, 0)),
                  pl.BlockSpec((1, d), lambda i: (0, 0))],
        out_specs=pl.BlockSpec((tc, d), lambda i: (i, 0)),
        out_shape=jax.ShapeDtypeStruct((n, d), F32),
        scratch_shapes=[pltpu.VMEM((TOP_K * tc * WORD_TILES, LANES), U32), pltpu.SemaphoreType.DMA(())],
        compiler_params=_cparams(("arbitrary",)),
        name="combine",
    )(slot3, ybuf, top_w, x1, modg2, ln_g, ln_b)


def _split_w_in(w_in):
    offs = [0]
    for w in PROJ_WIDTHS:
        offs.append(offs[-1] + w)
    seg = [w_in[:, offs[k]:offs[k + 1]] for k in range(len(PROJ_WIDTHS))]
    (q_nsa, k_cmp, v_cmp, k_sel, v_sel, k_win, v_win, g_nsa, q_gla, k_gla, v_gla, a_gla, r_gla) = seg
    big = jnp.concatenate([q_nsa, v_gla, r_gla, k_cmp, v_cmp, k_sel, v_sel, k_win, v_win, q_gla, k_gla],
                          axis=1).astype(BF16)
    pad = lambda w: jnp.pad(w, ((0, 0), (0, LANES - w.shape[1])))
    small = jnp.concatenate([pad(g_nsa), pad(a_gla)], axis=1)
    return big, small


def _overlap_matrix(ncp, nsel):
    cs = jnp.arange(ncp)[:, None] * CMP_STRIDE
    ss = jnp.arange(nsel)[None, :] * SEL_BLOCK
    return jnp.logical_and(cs < ss + SEL_BLOCK, cs + CMP_BLOCK > ss).astype(F32)


MOE_TM = 896


def kernel(x, c, w_ada, b_ada, w_in, cmp_pe_k, cmp_w1_k, cmp_b1_k, cmp_w2_k, cmp_pe_v, cmp_w1_v, cmp_b1_v,
           cmp_w2_v, gla_w_gate, gla_b_gate, gla_norm_w, w_out, ln1_g, ln1_b, w_router, b_router, w_gate_up,
           b_gate_up, w_down, b_down, ln2_g, ln2_b):
    return _forward(x, c, w_ada, b_ada, w_in, cmp_pe_k, cmp_w1_k, cmp_b1_k, cmp_w2_k, cmp_pe_v, cmp_w1_v,
                    cmp_b1_v, cmp_w2_v, gla_w_gate, gla_b_gate, gla_norm_w, w_out, ln1_g, ln1_b, w_router,
                    b_router, w_gate_up, b_gate_up, w_down, b_down, ln2_g, ln2_b, moe_tm=MOE_TM)


def _forward(x, c, w_ada, b_ada, w_in, cmp_pe_k, cmp_w1_k, cmp_b1_k, cmp_w2_k, cmp_pe_v, cmp_w1_v, cmp_b1_v,
             cmp_w2_v, gla_w_gate, gla_b_gate, gla_norm_w, w_out, ln1_g, ln1_b, w_router, b_router, w_gate_up,
             b_gate_up, w_down, b_down, ln2_g, ln2_b, *, moe_tm):
    bsz, seq, d = x.shape
    n = bsz * seq
    l = 0
    x2 = x.reshape(n, d)

    mod = ada_mod(c, w_ada[l], b_ada[l])
    sh1, sc1, g1, sh2, sc2, g2 = jnp.split(mod, 6, axis=-1)
    mod1 = jnp.stack([sc1, sh1], axis=1)
    mod2 = jnp.stack([g1, sc2, sh2], axis=1)
    modg2 = g2[:, None, :]

    w_big, w_small = _split_w_in(w_in[l])
    p_buf, s_buf = in_proj(x2, mod1, w_big, w_small, seq)

    ng = seq // CMP_STRIDE
    kvg = p_buf[:, P_KVCMP:P_KVCMP + 2 * KV_WIDTH].reshape(bsz, ng, CMP_STRIDE, 4, HEAD_DIM)
    kvg = kvg.transpose(0, 3, 1, 2, 4).reshape(bsz, 4, ng, CMP_STRIDE * HEAD_DIM)
    half = CMP_STRIDE * HEAD_DIM
    pe2 = jnp.stack([cmp_pe_k[l].reshape(2, half), cmp_pe_v[l].reshape(2, half)])
    w1 = jnp.stack([cmp_w1_k[l], cmp_w1_v[l]]).astype(BF16)
    b1 = jnp.stack([cmp_b1_k[l], cmp_b1_v[l]])[:, None, :]
    w2 = jnp.stack([cmp_w2_k[l], cmp_w2_v[l]]).astype(BF16)
    kvc = compress(kvg, pe2, w1, b1, w2)

    nsel = seq // SEL_BLOCK
    ovl = _overlap_matrix(ng, nsel)
    ocmp, imp = cmp_attention(p_buf, kvc, s_buf, ovl, bsz, seq)
    selb, blk_any = topk_select(imp.reshape(bsz * NSA_KV_HEADS, nsel, seq), qtile=NSA_TQ)
    y_nsa = sel_win_attention(p_buf, selb, blk_any, ocmp, s_buf, bsz, seq, tq=NSA_TQ)

    wg_pad = jnp.pad(gla_w_gate[l], ((0, LANES - GLA_GATE_RANK), (0, 0)))
    tril = jnp.tril(jnp.ones((GLA_CHUNK, GLA_CHUNK), F32))
    y_gla = gla_attention(p_buf, s_buf, wg_pad, gla_b_gate[l][None, :], gla_norm_w[l][None, :], tril, bsz, seq)

    tm_r = min(512, seq)
    ltri = jnp.tril(jnp.ones((tm_r, tm_r), F32), k=-1).astype(BF16)
    x1, hw, top_idx, top_w, rank, counts = out_proj_router(
        y_nsa, y_gla, w_out[l].astype(BF16), x2, mod2, ln1_g[l][None, :], ln1_b[l][None, :],
        w_router[l], b_router[l][None, :], ltri, seq, tm=tm_r)

    tm = moe_tm
    n_asg = n * TOP_K
    cnt = counts[0]
    padded = (cnt + tm - 1) // tm * tm
    ends_p = jnp.cumsum(padded)
    start_p = ends_p - padded
    idx4 = top_idx[:, :TOP_K]
    slot_of = (start_p[idx4] + rank[:, :TOP_K]).astype(I32)
    n_blocks = -(-n_asg // tm) + N_EXPERTS
    n_slots = n_blocks * tm
    nused = (ends_p[-1] // tm).astype(I32).reshape(1)
    blk_start = jnp.minimum(jnp.arange(n_blocks, dtype=I32), nused - 1) * tm
    block_expert = jnp.minimum(jnp.sum(ends_p[None, :] <= blk_start[:, None], axis=1), N_EXPERTS - 1).astype(I32)
    sub = tm // MOE_CHUNKS
    rows_left = cnt[block_expert] - (blk_start - start_p[block_expert])
    block_chunks = jnp.clip((rows_left + sub - 1) // sub, 1, MOE_CHUNKS).astype(I32)

    tc = min(256, seq)
    xbuf = dispatch_rows(slot_of.reshape(n // tc, 1, tc * TOP_K), hw, n_slots, tc)
    ybuf = moe_experts(block_expert, nused, block_chunks, xbuf, w_gate_up[l], b_gate_up[l], w_down[l], b_down[l], tm)

    slot3 = slot_of.reshape(n // tc, tc, TOP_K).transpose(0, 2, 1).reshape(n // tc, 1, TOP_K * tc)
    out = combine(slot3, ybuf, top_w, x1, modg2, ln2_g[l][None, :], ln2_b[l][None, :], seq, tc=tc)
    return out.reshape(bsz, seq, d)
```
